```python
import math
import jax
import jax.numpy as jnp
from jax import lax
import numpy as np

D_MODEL = 2048
BATCH = 2
SEQ = 16384
DEPTH = 2

PLE_DIM = 256
N_BRANCH = 4
BRANCH_WIDTH = 1024

SSD_HEADS = 16
SSD_HEAD_DIM = 64
SSD_INNER = SSD_HEADS * SSD_HEAD_DIM
SSD_GROUPS = 2
SSD_STATE = 128
SSD_CONV = 4
SSD_CONV_DIM = SSD_INNER + 2 * SSD_GROUPS * SSD_STATE
SSD_CHUNK = 128

S5_WIDTH = 1024
S5_GROUP = 16
S5_GROUPS = S5_WIDTH // S5_GROUP
S5_STATE = 64
S5_DT_MIN = 1e-3
S5_DT_MAX = 1e-1

RET_HEADS = 8
RET_QK_DIM = 64
RET_V_DIM = 128
RET_QK = RET_HEADS * RET_QK_DIM
RET_WIDTH = RET_HEADS * RET_V_DIM
RET_CHUNK = 128
ROPE_BASE = 10000.0

GLA_HEADS = 4
GLA_QK_DIM = 128
GLA_V_DIM = 256
GLA_QK = GLA_HEADS * GLA_QK_DIM
GLA_WIDTH = GLA_HEADS * GLA_V_DIM
GLA_GATE_RANK = 16
GLA_TAU = 16.0
GLA_CHUNK = 64

N_EXPERTS = 32
TOP_K = 4
D_EXPERT = 1024
SWIGLU_ALPHA = 1.702
SWIGLU_LIMIT = 7.0
MOE_BLOCK = 256

DEEPNORM_ALPHA = (2 * DEPTH) ** 0.25
DEEPNORM_BETA = (8 * DEPTH) ** -0.25
LN_EPS = 1e-5
RMS_EPS = 1e-6

IN_SPLITS = (SSD_INNER, SSD_CONV_DIM, SSD_HEADS,
             S5_WIDTH,
             RET_QK, RET_QK, RET_WIDTH, RET_WIDTH,
             GLA_QK, GLA_QK, GLA_WIDTH, GLA_WIDTH,
             GLA_GATE_RANK,
             N_BRANCH * D_MODEL)
N_IN = sum(IN_SPLITS)

kernel_name = 'hybrid_ssd_s5_retention_gla_moe_trunk'


def layer_norm(x, w, b):
    xf = x.astype(jnp.float32)
    mu = jnp.mean(xf, -1, keepdims=True)
    var = jnp.mean(jnp.square(xf - mu), -1, keepdims=True)
    return ((xf - mu) * lax.rsqrt(var + LN_EPS) * w + b).astype(x.dtype)


def rms_norm(x, w):
    xf = x.astype(jnp.float32)
    return xf * lax.rsqrt(jnp.mean(jnp.square(xf), -1, keepdims=True) + RMS_EPS) * w


def chunk_recurrence(chunk_inputs, chunk_decay):
    inputs = jnp.moveaxis(chunk_inputs, 1, 0)
    decays = jnp.moveaxis(chunk_decay, 1, 0)

    def step(h, inp):
        s, d = inp
        return h * d + s, h

    _, h_prev = lax.scan(step, jnp.zeros_like(inputs[0]), (inputs, decays))
    return jnp.moveaxis(h_prev, 0, 1)


def causal_depthwise_conv(x, w, b):
    k, c = w.shape
    y = lax.conv_general_dilated(x, w[:, None, :], (1,), [(k - 1, 0)],
                                 dimension_numbers=('NWC', 'WIO', 'NWC'), feature_group_count=c)
    return y + b


def ssd_mixer(z, xbc, dt_raw, conv_w, conv_b, dt_bias, a_log, d_skip, norm_w):
    f32 = jnp.float32
    bsz, seq, _ = z.shape
    nc, L = seq // SSD_CHUNK, SSD_CHUNK
    R = SSD_HEADS // SSD_GROUPS
    xbc = jax.nn.silu(causal_depthwise_conv(xbc.astype(f32), conv_w.astype(f32), conv_b.astype(f32)))
    xs, bm, cm = jnp.split(xbc, [SSD_INNER, SSD_INNER + SSD_GROUPS * SSD_STATE], axis=-1)
    dt = jax.nn.softplus(dt_raw.astype(f32) + dt_bias.astype(f32))
    log_a = dt * (-jnp.exp(a_log.astype(f32)))
    xdt = xs.reshape(bsz, nc, L, SSD_GROUPS, R, SSD_HEAD_DIM) * dt.reshape(bsz, nc, L, SSD_GROUPS, R, 1)
    bm = bm.reshape(bsz, nc, L, SSD_GROUPS, SSD_STATE)
    cm = cm.reshape(bsz, nc, L, SSD_GROUPS, SSD_STATE)
    a_cs = jnp.cumsum(log_a.reshape(bsz, nc, L, SSD_GROUPS, R), axis=2)
    causal = jnp.tril(jnp.ones((L, L), bool))
    seg = a_cs[:, :, :, None] - a_cs[:, :, None, :]
    decay = jnp.exp(jnp.where(causal[None, None, :, :, None, None], seg, -jnp.inf))
    cb = jnp.einsum('bclgn,bcsgn->bclsg', cm, bm)
    y_diag = jnp.einsum('bclsg,bclsgr,bcsgrp->bclgrp', cb, decay, xdt)
    to_end = jnp.exp(a_cs[:, :, -1:] - a_cs)
    states = jnp.einsum('bclgn,bclgr,bclgrp->bcgrpn', bm, to_end, xdt)
    h_in = chunk_recurrence(states, jnp.exp(a_cs[:, :, -1])[..., None, None])
    y_off = jnp.einsum('bclgn,bcgrpn,bclgr->bclgrp', cm, h_in, jnp.exp(a_cs))
    y = (y_diag + y_off).reshape(bsz, seq, SSD_HEADS, SSD_HEAD_DIM)
    y = y + d_skip.astype(f32)[:, None] * xs.reshape(bsz, seq, SSD_HEADS, SSD_HEAD_DIM)
    group = SSD_INNER // SSD_GROUPS
    y = y.reshape(bsz, seq, SSD_GROUPS, group) * jax.nn.silu(z.astype(f32)).reshape(bsz, seq, SSD_GROUPS, group)
    return rms_norm(y, norm_w.astype(f32).reshape(SSD_GROUPS, group)).reshape(bsz, seq, SSD_INNER)


def complex_affine_combine(earlier, later):
    a1r, a1i, b1r, b1i = earlier
    a2r, a2i, b2r, b2i = later
    return (a2r * a1r - a2i * a1i, a2r * a1i + a2i * a1r,
            a2r * b1r - a2i * b1i + b2r, a2r * b1i + a2i * b1r + b2i)


def s5_mixer(u, lam_re, lam_im, log_dt, b_re, b_im, c_re, c_im, d_skip, w_glu, b_glu):
    f32 = jnp.float32
    bsz, seq, _ = u.shape
    uf = u.astype(f32)
    ug = uf.reshape(bsz, seq, S5_GROUPS, S5_GROUP)
    lr = jnp.minimum(lam_re.astype(f32), -1e-4)[None, :]
    li = lam_im.astype(f32)[None, :]
    dt = jnp.exp(log_dt.astype(f32))[:, None]
    mag = jnp.exp(lr * dt)
    ab_re, ab_im = mag * jnp.cos(li * dt), mag * jnp.sin(li * dt)
    den = lr * lr + li * li
    nr, ni = ab_re - 1.0, ab_im
    f_re, f_im = (nr * lr + ni * li) / den, (ni * lr - nr * li) / den
    b_re, b_im = b_re.astype(f32), b_im.astype(f32)
    bb_re = f_re[..., None] * b_re - f_im[..., None] * b_im
    bb_im = f_re[..., None] * b_im + f_im[..., None] * b_re
    bu_re = jnp.einsum('bsgc,gpc->sbgp', ug, bb_re)
    bu_im = jnp.einsum('bsgc,gpc->sbgp', ug, bb_im)
    a_re = jnp.broadcast_to(ab_re[None, None], (seq, 1, S5_GROUPS, S5_STATE))
    a_im = jnp.broadcast_to(ab_im[None, None], (seq, 1, S5_GROUPS, S5_STATE))
    _, _, h_re, h_im = lax.associative_scan(complex_affine_combine, (a_re, a_im, bu_re, bu_im), axis=0)
    y = (jnp.einsum('sbgp,gcp->bsgc', h_re, c_re.astype(f32))
         - jnp.einsum('sbgp,gcp->bsgc', h_im, c_im.astype(f32)))
    y = jax.nn.gelu(y.reshape(bsz, seq, S5_WIDTH) + d_skip.astype(f32) * uf)
    return y * jax.nn.sigmoid(y @ w_glu.astype(f32) + b_glu.astype(f32))


def apply_rotary(x, positions):
    half = x.shape[-1] // 2
    inv_freq = 1.0 / (ROPE_BASE ** (jnp.arange(half, dtype=jnp.float32) / half))
    ang = positions.astype(jnp.float32)[..., None] * inv_freq
    cos, sin = jnp.cos(ang)[:, :, None, :], jnp.sin(ang)[:, :, None, :]
    x1, x2 = x[..., :half], x[..., half:]
    return jnp.concatenate([x1 * cos - x2 * sin, x2 * cos + x1 * sin], -1)


def retention_mixer(q, k, v, g, positions, norm_w):
    f32 = jnp.float32
    bsz, seq, _ = q.shape
    nc, L = seq // RET_CHUNK, RET_CHUNK
    q = apply_rotary(q.astype(f32).reshape(bsz, seq, RET_HEADS, RET_QK_DIM), positions)
    k = apply_rotary(k.astype(f32).reshape(bsz, seq, RET_HEADS, RET_QK_DIM), positions) * RET_QK_DIM ** -0.5
    q = q.reshape(bsz, nc, L, RET_HEADS, RET_QK_DIM)
    k = k.reshape(bsz, nc, L, RET_HEADS, RET_QK_DIM)
    v = v.astype(f32).reshape(bsz, nc, L, RET_HEADS, RET_V_DIM)
    log_gamma = jnp.log1p(-jnp.exp2(-5.0 - jnp.arange(RET_HEADS, dtype=f32)))
    pos = jnp.arange(L, dtype=f32)
    rel = (pos[:, None] - pos[None, :])[..., None]
    intra = jnp.exp(jnp.where(rel >= 0, rel * log_gamma, -jnp.inf))
    scores = jnp.einsum('bclhd,bcshd->bclsh', q, k) * intra
    y = jnp.einsum('bclsh,bcshe->bclhe', scores, v)
    kv = jnp.einsum('bclhd,lh,bclhe->bchde', k, jnp.exp((L - 1 - pos)[:, None] * log_gamma), v)
    chunk_decay = jnp.broadcast_to(jnp.exp(L * log_gamma)[:, None, None], (1, nc, RET_HEADS, 1, 1))
    r_in = chunk_recurrence(kv, chunk_decay)
    y = y + jnp.einsum('bclhd,bchde,lh->bclhe', q, r_in, jnp.exp((pos + 1)[:, None] * log_gamma))
    y = y.reshape(bsz, seq, RET_HEADS, RET_V_DIM)
    mu = jnp.mean(y, -1, keepdims=True)
    var = jnp.mean(jnp.square(y - mu), -1, keepdims=True)
    y = ((y - mu) * lax.rsqrt(var + LN_EPS)).reshape(bsz, seq, RET_WIDTH) * norm_w.astype(f32)
    return jax.nn.silu(g.astype(f32)) * y


def gla_mixer(q, k, v, r, gate_code, w_alpha, b_alpha, norm_w):
    f32 = jnp.float32
    bsz, seq, _ = q.shape
    nc, L = seq // GLA_CHUNK, GLA_CHUNK
    log_alpha = jax.nn.log_sigmoid(gate_code.astype(f32) @ w_alpha.astype(f32) + b_alpha.astype(f32)) / GLA_TAU
    shp = (bsz, nc, L, GLA_HEADS, GLA_QK_DIM)
    q = q.astype(f32).reshape(shp) * GLA_QK_DIM ** -0.5
    k = k.astype(f32).reshape(shp)
    v = v.astype(f32).reshape(bsz, nc, L, GLA_HEADS, GLA_V_DIM)
    b = jnp.cumsum(log_alpha.reshape(shp), axis=2)
    q_dec = q * jnp.exp(b)
    k_inv = k * jnp.exp(-b)
    causal = jnp.tril(jnp.ones((L, L), bool))[..., None]
    attn = jnp.where(causal, jnp.einsum('bclhd,bcshd->bclsh', q_dec, k_inv), 0.0)
    y = jnp.einsum('bclsh,bcshe->bclhe', attn, v)
    kv = jnp.einsum('bclhd,bclhe->bchde', k * jnp.exp(b[:, :, -1:] - b), v)
    s_in = chunk_recurrence(kv, jnp.exp(b[:, :, -1])[..., None])
    y = y + jnp.einsum('bclhd,bchde->bclhe', q_dec, s_in)
    y = rms_norm(y.reshape(bsz, seq, GLA_HEADS, GLA_V_DIM), norm_w.astype(f32).reshape(GLA_HEADS, GLA_V_DIM))
    return y.reshape(bsz, seq, GLA_WIDTH) * jax.nn.silu(r.astype(f32))


def hybrid_token_mixer(h, positions, w_in, ssd_conv_w, ssd_conv_b, ssd_dt_bias, ssd_a_log, ssd_d, ssd_norm_w,
                       s5_lambda_re, s5_lambda_im, s5_log_dt, s5_b_re, s5_b_im, s5_c_re, s5_c_im, s5_d,
                       s5_w_glu, s5_b_glu, ret_norm_w, gla_w_alpha, gla_b_alpha, gla_norm_w, w_branch, w_out):
    bsz, seq, _ = h.shape
    proj = h @ w_in
    (z, xbc, dt_raw, u, rq, rk, rv, rg, gq, gk, gv, gr, g_code, gate_logits) = jnp.split(
        proj, np.cumsum(IN_SPLITS)[:-1].tolist(), axis=-1)
    y_a = ssd_mixer(z, xbc, dt_raw, ssd_conv_w, ssd_conv_b, ssd_dt_bias, ssd_a_log, ssd_d, ssd_norm_w)
    y_b = s5_mixer(u, s5_lambda_re, s5_lambda_im, s5_log_dt, s5_b_re, s5_b_im, s5_c_re, s5_c_im, s5_d,
                   s5_w_glu, s5_b_glu)
    y_c = retention_mixer(rq, rk, rv, rg, positions, ret_norm_w)
    y_d = gla_mixer(gq, gk, gv, gr, g_code, gla_w_alpha, gla_b_alpha, gla_norm_w)
    ys = jnp.stack([y_a, y_b, y_c, y_d], axis=2)
    branches = jnp.einsum('bsnc,ncd->bsnd', ys, w_branch.astype(jnp.float32))
    gates = jax.nn.sigmoid(gate_logits.astype(jnp.float32).reshape(bsz, seq, N_BRANCH, D_MODEL))
    merged = jnp.sum(gates * branches, axis=2)
    return (merged @ w_out.astype(jnp.float32)).astype(h.dtype)


def moe_ffn(h, router_w, router_b, w_gate_up, b_gate_up, w_down, b_down):
    f32 = jnp.float32
    bsz, seq, d = h.shape
    n_tok = bsz * seq
    xf = h.reshape(n_tok, d)
    logits = (xf @ router_w + router_b).astype(f32)
    top_logits, top_idx = lax.top_k(logits, TOP_K)
    gate = jax.nn.softmax(top_logits, axis=-1)
    n_assign = n_tok * TOP_K
    e_flat = top_idx.reshape(n_assign)
    order = jnp.argsort(e_flat)
    e_sorted = e_flat[order]
    tok_sorted = (order // TOP_K).astype(jnp.int32)
    counts = jnp.bincount(e_flat, length=N_EXPERTS)
    padded = (counts + MOE_BLOCK - 1) // MOE_BLOCK * MOE_BLOCK
    pad_end = jnp.cumsum(padded)
    pad_start = pad_end - padded
    grp_start = jnp.cumsum(counts) - counts
    dest = pad_start[e_sorted] + jnp.arange(n_assign) - grp_start[e_sorted]
    n_blocks = -(-n_assign // MOE_BLOCK) + N_EXPERTS
    n_rows = n_blocks * MOE_BLOCK
    row_token = jnp.zeros((n_rows,), jnp.int32).at[dest].set(tok_sorted)
    xb = xf[row_token].reshape(n_blocks, MOE_BLOCK, d)
    block_expert = jnp.minimum(jnp.searchsorted(pad_end // MOE_BLOCK, jnp.arange(n_blocks), side='right'),
                               N_EXPERTS - 1)

    def expert_block(args):
        xblk, e = args
        hgu = xblk @ w_gate_up[e] + b_gate_up[e]
        g, u = hgu[:, :D_EXPERT], hgu[:, D_EXPERT:]
        g = jnp.minimum(g, SWIGLU_LIMIT)
        u = jnp.clip(u, -SWIGLU_LIMIT, SWIGLU_LIMIT)
        act = (u + 1.0) * (g * jax.nn.sigmoid(SWIGLU_ALPHA * g))
        return act @ w_down[e] + b_down[e]

    yb = lax.map(expert_block, (xb, block_expert))
    y_rows = yb.reshape(n_rows, d)[dest].astype(f32)
    w_sorted = gate.reshape(n_assign)[order]
    out = jnp.zeros((n_tok, d), f32).at[tok_sorted].add(y_rows * w_sorted[:, None])
    return out.reshape(bsz, seq, d).astype(h.dtype)


def per_layer_embedding(h, p_i, w_gate, w_proj):
    return (jax.nn.sigmoid(h @ w_gate) * (p_i @ w_proj)).astype(h.dtype)


def setup_inputs(seed: int = 0) -> dict:
    key = jax.random.key(seed)
    ks = iter(jax.random.split(key, 64))
    f32 = jnp.float32
    L = DEPTH

    def nrm(shape, scale):
        return jax.random.normal(next(ks), shape, f32) * scale

    def unif(shape, lo, hi):
        return jax.random.uniform(next(ks), shape, f32, lo, hi)

    x = nrm((BATCH, SEQ, D_MODEL), 1.0)
    p = nrm((DEPTH, BATCH, SEQ, PLE_DIM), 1.0)
    offsets = jax.random.randint(next(ks), (BATCH, 1), 0, 4096, jnp.int32)
    positions = offsets + jnp.arange(SEQ, dtype=jnp.int32)[None, :]
    w_in = nrm((L, D_MODEL, N_IN), D_MODEL ** -0.5)
    ssd_conv_w = nrm((L, SSD_CONV, SSD_CONV_DIM), SSD_CONV ** -0.5)
    ssd_conv_b = nrm((L, SSD_CONV_DIM), 0.02)
    dt0 = jnp.exp(unif((L, SSD_HEADS), math.log(1e-3), math.log(1e-1)))
    ssd_dt_bias = dt0 + jnp.log(-jnp.expm1(-dt0))
    ssd_a_log = jnp.log(unif((L, SSD_HEADS), 1.0, 16.0))
    ssd_d = 1.0 + nrm((L, SSD_HEADS), 0.1)
    ssd_norm_w = 1.0 + nrm((L, SSD_INNER), 0.02)
    n = jnp.arange(S5_STATE, dtype=f32)
    s5_lambda_re = -0.5 + nrm((L, S5_STATE), 0.01)
    s5_lambda_im = math.pi * n + nrm((L, S5_STATE), 0.01)
    s5_log_dt = unif((L, S5_GROUPS), math.log(S5_DT_MIN), math.log(S5_DT_MAX))
    s5_b_re = nrm((L, S5_GROUPS, S5_STATE, S5_GROUP), (2 * S5_GROUP) ** -0.5)
    s5_b_im = nrm((L, S5_GROUPS, S5_STATE, S5_GROUP), (2 * S5_GROUP) ** -0.5)
    s5_c_re = nrm((L, S5_GROUPS, S5_GROUP, S5_STATE), (2 * S5_STATE) ** -0.5)
    s5_c_im = nrm((L, S5_GROUPS, S5_GROUP, S5_STATE), (2 * S5_STATE) ** -0.5)
    s5_d = nrm((L, S5_WIDTH), 1.0)
    s5_w_glu = nrm((L, S5_WIDTH, S5_WIDTH), S5_WIDTH ** -0.5)
    s5_b_glu = nrm((L, S5_WIDTH), 0.02)
    ret_norm_w = 1.0 + nrm((L, RET_WIDTH), 0.02)
    gla_w_alpha = nrm((L, GLA_GATE_RANK, GLA_QK), GLA_GATE_RANK ** -0.5)
    gla_b_alpha = nrm((L, GLA_QK), 0.5)
    gla_norm_w = 1.0 + nrm((L, GLA_WIDTH), 0.02)
    w_branch = nrm((L, N_BRANCH, BRANCH_WIDTH, D_MODEL), BRANCH_WIDTH ** -0.5)
    w_out = nrm((L, D_MODEL, D_MODEL), DEEPNORM_BETA * D_MODEL ** -0.5)
    ln1_w = 1.0 + nrm((L, D_MODEL), 0.02)
    ln1_b = nrm((L, D_MODEL), 0.02)
    router_w = nrm((L, D_MODEL, N_EXPERTS), D_MODEL ** -0.5)
    router_b = nrm((L, N_EXPERTS), 0.01)
    moe_w_gate_up = nrm((L, N_EXPERTS, D_MODEL, 2 * D_EXPERT), D_MODEL ** -0.5)
    moe_b_gate_up = nrm((L, N_EXPERTS, 2 * D_EXPERT), 0.02)
    moe_w_down = nrm((L, N_EXPERTS, D_EXPERT, D_MODEL), DEEPNORM_BETA * D_EXPERT ** -0.5)
    moe_b_down = nrm((L, N_EXPERTS, D_MODEL), 0.02)
    ln2_w = 1.0 + nrm((L, D_MODEL), 0.02)
    ln2_b = nrm((L, D_MODEL), 0.02)
    ple_w_gate = nrm((L, D_MODEL, D_MODEL), D_MODEL ** -0.5)
    ple_w_proj = nrm((L, PLE_DIM, D_MODEL), DEEPNORM_BETA * PLE_DIM ** -0.5)
    ln3_w = 1.0 + nrm((L, D_MODEL), 0.02)
    ln3_b = nrm((L, D_MODEL), 0.02)
    return {
        'x': x, 'p': p, 'positions': positions, 'w_in': w_in,
        'ssd_conv_w': ssd_conv_w, 'ssd_conv_b': ssd_conv_b, 'ssd_dt_bias': ssd_dt_bias,
        'ssd_a_log': ssd_a_log, 'ssd_d': ssd_d, 'ssd_norm_w': ssd_norm_w,
        's5_lambda_re': s5_lambda_re, 's5_lambda_im': s5_lambda_im, 's5_log_dt': s5_log_dt,
        's5_b_re': s5_b_re, 's5_b_im': s5_b_im, 's5_c_re': s5_c_re, 's5_c_im': s5_c_im,
        's5_d': s5_d, 's5_w_glu': s5_w_glu, 's5_b_glu': s5_b_glu,
        'ret_norm_w': ret_norm_w,
        'gla_w_alpha': gla_w_alpha, 'gla_b_alpha': gla_b_alpha, 'gla_norm_w': gla_norm_w,
        'w_branch': w_branch, 'w_out': w_out, 'ln1_w': ln1_w, 'ln1_b': ln1_b,
        'router_w': router_w, 'router_b': router_b, 'moe_w_gate_up': moe_w_gate_up,
        'moe_b_gate_up': moe_b_gate_up, 'moe_w_down': moe_w_down, 'moe_b_down': moe_b_down,
        'ln2_w': ln2_w, 'ln2_b': ln2_b,
        'ple_w_gate': ple_w_gate, 'ple_w_proj': ple_w_proj, 'ln3_w': ln3_w, 'ln3_b': ln3_b,
    }


def reference(x, p, positions, w_in, ssd_conv_w, ssd_conv_b, ssd_dt_bias, ssd_a_log, ssd_d, ssd_norm_w,
              s5_lambda_re, s5_lambda_im, s5_log_dt, s5_b_re, s5_b_im, s5_c_re, s5_c_im, s5_d, s5_w_glu,
              s5_b_glu, ret_norm_w, gla_w_alpha, gla_b_alpha, gla_norm_w, w_branch, w_out, ln1_w, ln1_b,
              router_w, router_b, moe_w_gate_up, moe_b_gate_up, moe_w_down, moe_b_down, ln2_w, ln2_b,
              ple_w_gate, ple_w_proj, ln3_w, ln3_b):
    h = x
    for i in range(DEPTH):
        mix = hybrid_token_mixer(h, positions, w_in[i], ssd_conv_w[i], ssd_conv_b[i], ssd_dt_bias[i],
                                 ssd_a_log[i], ssd_d[i], ssd_norm_w[i], s5_lambda_re[i], s5_lambda_im[i],
                                 s5_log_dt[i], s5_b_re[i], s5_b_im[i], s5_c_re[i], s5_c_im[i], s5_d[i],
                                 s5_w_glu[i], s5_b_glu[i], ret_norm_w[i], gla_w_alpha[i], gla_b_alpha[i],
                                 gla_norm_w[i], w_branch[i], w_out[i])
        h = layer_norm(DEEPNORM_ALPHA * h + mix, ln1_w[i], ln1_b[i])
        ffn = moe_ffn(h, router_w[i], router_b[i], moe_w_gate_up[i], moe_b_gate_up[i], moe_w_down[i], moe_b_down[i])
        h = layer_norm(DEEPNORM_ALPHA * h + ffn, ln2_w[i], ln2_b[i])
        ple = per_layer_embedding(h, p[i], ple_w_gate[i], ple_w_proj[i])
        h = layer_norm(DEEPNORM_ALPHA * h + ple, ln3_w[i], ln3_b[i])
    return h
```

```python
import functools
import math

import numpy as np
import jax
import jax.numpy as jnp
from jax import lax
from jax.experimental import pallas as pl
from jax.experimental.pallas import tpu as pltpu

F32 = jnp.float32
BF16 = jnp.bfloat16
HIGHEST = lax.Precision.HIGHEST

D_MODEL = 2048
DEPTH = 2
PLE_DIM = 256
N_BRANCH = 4
BRANCH_WIDTH = 1024

SSD_HEADS = 16
SSD_HEAD_DIM = 64
SSD_INNER = SSD_HEADS * SSD_HEAD_DIM
SSD_GROUPS = 2
SSD_STATE = 128
SSD_CONV = 4
SSD_CHUNK = 128

S5_WIDTH = 1024
S5_GROUP = 16
S5_GROUPS = S5_WIDTH // S5_GROUP
S5_STATE = 64
S5_CHUNK = 16
S5_TILE_GROUPS = 8

RET_HEADS = 8
RET_QK_DIM = 64
RET_V_DIM = 128
RET_QK = RET_HEADS * RET_QK_DIM
RET_WIDTH = RET_HEADS * RET_V_DIM
RET_CHUNK = 128
ROPE_BASE = 10000.0

GLA_HEADS = 4
GLA_QK_DIM = 128
GLA_V_DIM = 256
GLA_QK = GLA_HEADS * GLA_QK_DIM
GLA_WIDTH = GLA_HEADS * GLA_V_DIM
GLA_GATE_RANK = 16
GLA_TAU = 16.0
GLA_CHUNK = 64
GLA_BLOCK = 128

N_EXPERTS = 32
TOP_K = 4
D_EXPERT = 1024
SWIGLU_ALPHA = 1.702
SWIGLU_LIMIT = 7.0
MOE_BLOCK = 256

DEEPNORM_ALPHA = (2 * DEPTH) ** 0.25
LN_EPS = 1e-5
RMS_EPS = 1e-6

LANES = 128
VMEM_LIMIT = 56 * 1024 * 1024

_OFF_Z = 0
_OFF_XS = 1024
_OFF_BC = 2048
_OFF_DT = 2560
_OFF_U = 2576
_OFF_RQ = 3600
_OFF_RK = 4112
_OFF_RV = 4624
_OFF_RG = 5648
_OFF_GQ = 6672
_OFF_GK = 7184
_OFF_GV = 7696
_OFF_GR = 8720
_OFF_GC = 9744
_OFF_GATE = 9760

_CB_Z, _CB_XS, _CB_RV, _CB_RG, _CB_GV, _CB_GR = 8, 9, 10, 11, 12, 13
_CB_BC, _CB_RQ, _CB_RK, _CB_GQ, _CB_GK = 28, 29, 30, 31, 32
N_MAIN = 16896


def _rotary_perm():
    perm = np.zeros(RET_QK, np.int32)
    half = RET_QK_DIM // 2
    for h in range(RET_HEADS):
        for d in range(RET_QK_DIM):
            new = (h // 2) * 128 + (d // half) * 64 + (h % 2) * half + (d % half)
            perm[new] = h * RET_QK_DIM + d
    return perm


def _main_columns():
    r = np.arange
    perm = _rotary_perm()
    return np.concatenate([
        _OFF_GATE + r(N_BRANCH * D_MODEL),
        _OFF_Z + r(1024), _OFF_XS + r(1024), _OFF_RV + r(1024), _OFF_RG + r(1024),
        _OFF_GV + r(1024), _OFF_GR + r(1024),
        _OFF_BC + r(512), _OFF_RQ + perm, _OFF_RK + perm, _OFF_GQ + r(512), _OFF_GK + r(512),
    ]).astype(np.int32)


def _cparams(sem):
    return pltpu.CompilerParams(dimension_semantics=sem, vmem_limit_bytes=VMEM_LIMIT)


def _silu(x):
    return x * jax.nn.sigmoid(x)


def _softplus(x):
    return jnp.maximum(x, 0.0) + jnp.log1p(jnp.exp(-jnp.abs(x)))


def _layer_norm(x, w, b):
    mu = jnp.mean(x, -1, keepdims=True)
    xc = x - mu
    var = jnp.mean(xc * xc, -1, keepdims=True)
    return xc * lax.rsqrt(var + LN_EPS) * w + b


def _mm_kernel(a_ref, b_ref, o_ref, *, precision):
    o_ref[...] = jnp.dot(a_ref[...], b_ref[...], preferred_element_type=F32,
                         precision=precision).astype(o_ref.dtype)


def _matmul(a, b, out_dtype, tm, tn, precision=None, name="matmul"):
    m, k = a.shape
    n = b.shape[1]
    tm, tn = min(tm, m), min(tn, n)
    return pl.pallas_call(
        functools.partial(_mm_kernel, precision=precision),
        grid=(m // tm, n // tn),
        in_specs=[pl.BlockSpec((tm, k), lambda i, j: (i, 0)),
                  pl.BlockSpec((k, tn), lambda i, j: (0, j))],
        out_specs=pl.BlockSpec((tm, tn), lambda i, j: (i, j)),
        out_shape=jax.ShapeDtypeStruct((m, n), out_dtype),
        compiler_params=_cparams(("parallel", "parallel")),
        name=name,
    )(a, b)


def _causal_conv(x, tail, w, b):
    n_tap = SSD_CONV
    acc = x * w[n_tap - 1:n_tap] + b
    x8 = x[0:8]
    acc8 = x8 * w[n_tap - 1:n_tap] + b
    row8 = lax.broadcasted_iota(jnp.int32, x8.shape, 0)
    for s in range(1, n_tap):
        wk = w[n_tap - 1 - s:n_tap - s]
        acc = acc + pltpu.roll(x, s, 0) * wk
        v8 = jnp.where(row8 < s, pltpu.roll(tail, s, 0), pltpu.roll(x8, s, 0))
        acc8 = acc8 + v8 * wk
    return jnp.concatenate([acc8, acc[8:]], axis=0)


def _ssd_kernel(z_ref, xs_ref, bc_ref, sm_ref, cwx_ref, cwbc_ref, cbx_ref, cbbc_ref, dtb_ref,
                nega_ref, dsk_ref, nw_ref, exp_ref, tril_ref, o_ref, tailx_ref, tailbc_ref, ht_ref):
    L = SSD_CHUNK

    @pl.when(pl.program_id(1) == 0)
    def _():
        tailx_ref[...] = jnp.zeros_like(tailx_ref)
        tailbc_ref[...] = jnp.zeros_like(tailbc_ref)
        ht_ref[...] = jnp.zeros_like(ht_ref)

    x_raw = xs_ref[...].astype(F32)
    bc_raw = bc_ref[...].astype(F32)
    xs = _silu(_causal_conv(x_raw, tailx_ref[...], cwx_ref[...], cbx_ref[...]))
    bc = _silu(_causal_conv(bc_raw, tailbc_ref[...], cwbc_ref[...], cbbc_ref[...]))
    tailx_ref[...] = x_raw[L - 8:L]
    tailbc_ref[...] = bc_raw[L - 8:L]

    dt = _softplus(sm_ref[...] + dtb_ref[...])
    loga = dt * nega_ref[...]
    acs = jnp.dot(tril_ref[...], loga, preferred_element_type=F32, precision=HIGHEST)
    acs_t = acs.T
    expand = exp_ref[...]
    dt_e = jnp.dot(dt, expand, preferred_element_type=F32, precision=HIGHEST)
    acs_e = jnp.dot(acs, expand, preferred_element_type=F32, precision=HIGHEST)
    last_e = acs_e[L - 1:L]
    xdt = xs * dt_e
    xdt_b = xdt.astype(BF16)
    xw_b = (xdt * jnp.exp(last_e - acs_e)).astype(BF16)
    dec_in = jnp.exp(acs_e)
    dec_chunk = jnp.exp(last_e)

    row = lax.broadcasted_iota(jnp.int32, (L, L), 0)
    col = lax.broadcasted_iota(jnp.int32, (L, L), 1)
    causal = row >= col
    lo_half = col < SSD_HEAD_DIM
    heads_per_group = SSD_HEADS // SSD_GROUPS
    n_bc = SSD_GROUPS * SSD_STATE

    y_tiles = []
    for g in range(SSD_GROUPS):
        bm = bc[:, g * SSD_STATE:(g + 1) * SSD_STATE]
        cm_b = bc[:, n_bc + g * SSD_STATE:n_bc + (g + 1) * SSD_STATE].astype(BF16)
        bm_t_b = bm.T.astype(BF16)
        cb = jnp.dot(cm_b, bm_t_b, preferred_element_type=F32)
        for j in range(g * heads_per_group // 2, (g + 1) * heads_per_group // 2):
            sl = slice(j * LANES, (j + 1) * LANES)
            xt = xdt_b[:, sl]
            y_pair = None
            for sub in range(2):
                h = 2 * j + sub
                seg = acs[:, h:h + 1] - acs_t[h:h + 1, :]
                m = (cb * jnp.exp(jnp.where(causal, seg, -jnp.inf))).astype(BF16)
                xh = jnp.where(lo_half if sub == 0 else ~lo_half, xt, jnp.zeros_like(xt))
                yd = jnp.dot(m, xh, preferred_element_type=F32)
                y_pair = yd if y_pair is None else y_pair + yd
            h_t = ht_ref[:, sl]
            y_off = jnp.dot(cm_b, h_t.astype(BF16), preferred_element_type=F32) * dec_in[:, sl]
            st = jnp.dot(bm_t_b, xw_b[:, sl], preferred_element_type=F32)
            ht_ref[:, sl] = h_t * dec_chunk[:, sl] + st
            y_tiles.append(y_pair + y_off)
    y = jnp.concatenate(y_tiles, axis=1) + dsk_ref[...] * xs
    y = y * _silu(z_ref[...].astype(F32))
    gw = SSD_INNER // SSD_GROUPS
    outs = []
    for g in range(SSD_GROUPS):
        yg = y[:, g * gw:(g + 1) * gw]
        ms = jnp.mean(yg * yg, -1, keepdims=True)
        outs.append(yg * lax.rsqrt(ms + RMS_EPS) * nw_ref[:, g * gw:(g + 1) * gw])
    o_ref[...] = jnp.concatenate(outs, axis=1).astype(o_ref.dtype)


def _ssd_mixer(proj, small, bsz, seq, conv_w, conv_b, dt_bias, a_log, d_skip, norm_w):
    L = SSD_CHUNK
    nc = seq // L
    pad_lanes = LANES - SSD_HEADS
    cw = jnp.pad(conv_w.astype(F32), ((0, 8 - SSD_CONV), (0, 0)))
    cwx, cwbc = cw[:, :SSD_INNER], cw[:, SSD_INNER:]
    cb = conv_b.astype(F32)[None, :]
    cbx, cbbc = cb[:, :SSD_INNER], cb[:, SSD_INNER:]
    dtb = jnp.pad(dt_bias.astype(F32), (0, pad_lanes))[None, :]
    nega = jnp.pad(-jnp.exp(a_log.astype(F32)), (0, pad_lanes))[None, :]
    dsk = jnp.repeat(d_skip.astype(F32), SSD_HEAD_DIM)[None, :]
    nw = norm_w.astype(F32)[None, :]
    expand = (np.arange(LANES)[:, None] == (np.arange(SSD_INNER)[None, :] // SSD_HEAD_DIM)).astype(np.float32)
    tril = np.tril(np.ones((L, L), np.float32))
    rows = lambda b, c: b * nc + c
    full = lambda shape: pl.BlockSpec(shape, lambda b, c: (0,) * len(shape))
    return pl.pallas_call(
        _ssd_kernel,
        grid=(bsz, nc),
        in_specs=[
            pl.BlockSpec((L, 1024), lambda b, c: (rows(b, c), _CB_Z)),
            pl.BlockSpec((L, 1024), lambda b, c: (rows(b, c), _CB_XS)),
            pl.BlockSpec((L, 512), lambda b, c: (rows(b, c), _CB_BC)),
            pl.BlockSpec((L, LANES), lambda b, c: (rows(b, c), 0)),
            full((8, SSD_INNER)), full((8, 512)), full((1, SSD_INNER)), full((1, 512)),
            full((1, LANES)), full((1, LANES)), full((1, SSD_INNER)), full((1, SSD_INNER)),
            full((LANES, SSD_INNER)), full((L, L)),
        ],
        out_specs=pl.BlockSpec((L, SSD_INNER), lambda b, c: (rows(b, c), 0)),
        out_shape=jax.ShapeDtypeStruct((bsz * seq, SSD_INNER), BF16),
        scratch_shapes=[pltpu.VMEM((8, SSD_INNER), F32), pltpu.VMEM((8, 512), F32),
                        pltpu.VMEM((SSD_STATE, SSD_INNER), F32)],
        compiler_params=_cparams(("parallel", "arbitrary")),
        name="ssd_mixer",
    )(proj, proj, proj, small, cwx, cwbc, cbx, cbbc, dtb, nega, dsk, nw, jnp.asarray(expand), jnp.asarray(tril))


def _s5_scan_kernel(u_ref, t_ref, w_ref, v_ref, a1_ref, a2_ref, y_ref, sc_ref, hin_ref, hc_ref, *, rows):
    R = rows

    @pl.when(pl.program_id(2) == 0)
    def _():
        hc_ref[...] = jnp.zeros_like(hc_ref)

    parts = [u_ref[pl.ds(s, R, stride=S5_CHUNK), :].astype(BF16) for s in range(S5_CHUNK)]
    uf = jnp.concatenate(parts, axis=1)
    y_intra = jnp.dot(uf, t_ref[0], preferred_element_type=F32)
    sc_ref[...] = jnp.dot(uf, w_ref[0], preferred_element_type=F32)
    a1 = a1_ref[0]
    a2 = a2_ref[0]
    half = S5_TILE_GROUPS * S5_STATE

    def step(r, h):
        hin_ref[pl.ds(r, 1), :] = h
        return a1 * h + a2 * pltpu.roll(h, half, 1) + sc_ref[pl.ds(r, 1), :]

    hc_ref[...] = lax.fori_loop(0, R, step, hc_ref[...])
    y = y_intra + jnp.dot(hin_ref[...].astype(BF16), v_ref[0], preferred_element_type=F32)
    for l in range(S5_CHUNK):
        y_ref[pl.ds(l, R, stride=S5_CHUNK), :] = y[:, l * LANES:(l + 1) * LANES]


def _s5_tables(lam_re, lam_im, log_dt, b_re, b_im, c_re, c_im):
    f32 = F32
    lc = S5_CHUNK
    tg = S5_TILE_GROUPS
    nt = S5_GROUPS // tg
    lr = jnp.minimum(lam_re.astype(f32), -1e-4)[None, :]
    li = lam_im.astype(f32)[None, :]
    dt = jnp.exp(log_dt.astype(f32))[:, None]
    mag = jnp.exp(lr * dt)
    ab_re, ab_im = mag * jnp.cos(li * dt), mag * jnp.sin(li * dt)
    den = lr * lr + li * li
    nr, ni = ab_re - 1.0, ab_im
    f_re, f_im = (nr * lr + ni * li) / den, (ni * lr - nr * li) / den
    b_re, b_im = b_re.astype(f32), b_im.astype(f32)
    bb_re = f_re[..., None] * b_re - f_im[..., None] * b_im
    bb_im = f_re[..., None] * b_im + f_im[..., None] * b_re
    tau = jnp.arange(lc + 1, dtype=f32)[:, None, None]
    pw_mag = jnp.exp(tau * (lr * dt)[None])
    pw_re = pw_mag * jnp.cos(tau * (li * dt)[None])
    pw_im = pw_mag * jnp.sin(tau * (li * dt)[None])
    c_re, c_im = c_re.astype(f32), c_im.astype(f32)
    cl_re = c_re[None] * pw_re[:, :, None, :] - c_im[None] * pw_im[:, :, None, :]
    cl_im = c_re[None] * pw_im[:, :, None, :] + c_im[None] * pw_re[:, :, None, :]
    hp = lax.Precision.HIGHEST
    kern = (jnp.einsum('tgop,gpi->tgoi', cl_re[:lc], bb_re, precision=hp)
            - jnp.einsum('tgop,gpi->tgoi', cl_im[:lc], bb_im, precision=hp))
    s_idx = np.arange(lc)[:, None]
    l_idx = np.arange(lc)[None, :]
    lag = np.clip(l_idx - s_idx, 0, lc - 1)
    kt = kern[lag] * jnp.asarray((l_idx >= s_idx).astype(np.float32))[:, :, None, None, None]
    kt = kt.reshape(lc, lc, nt, tg, S5_GROUP, S5_GROUP)
    eye = jnp.eye(tg, dtype=f32)
    toep = jnp.einsum('slTgoi,gh->Tsgilho', kt, eye).reshape(nt, lc * LANES, lc * LANES)
    rev = pw_re[lc - 1 - np.arange(lc)], pw_im[lc - 1 - np.arange(lc)]
    wr = rev[0][..., None] * bb_re[None] - rev[1][..., None] * bb_im[None]
    wi = rev[0][..., None] * bb_im[None] + rev[1][..., None] * bb_re[None]
    wst = jnp.stack([wr, wi], 0).reshape(2, lc, nt, tg, S5_STATE, S5_GROUP)
    w_in = jnp.einsum('rsTgpi,gh->Tsgirhp', wst, eye).reshape(nt, lc * LANES, 2 * tg * S5_STATE)
    vst = jnp.stack([cl_re[1:], -cl_im[1:]], 0).reshape(2, lc, nt, tg, S5_GROUP, S5_STATE)
    v_out = jnp.einsum('rlTgop,gh->Trgplho', vst, eye).reshape(nt, 2 * tg * S5_STATE, lc * LANES)
    a_re = pw_re[lc].reshape(nt, 1, tg * S5_STATE)
    a_im = pw_im[lc].reshape(nt, 1, tg * S5_STATE)
    a1 = jnp.concatenate([a_re, a_re], -1)
    a2 = jnp.concatenate([-a_im, a_im], -1)
    return toep.astype(BF16), w_in.astype(BF16), v_out.astype(BF16), a1, a2


def _s5_scan(u, bsz, seq, tables, rows):
    toep, w_in, v_out, a1, a2 = tables
    nt = S5_GROUPS // S5_TILE_GROUPS
    lc = S5_CHUNK
    rows = min(rows, seq // lc)
    nblk = seq // (lc * rows)
    tok = rows * lc
    ns = 2 * S5_TILE_GROUPS * S5_STATE
    return pl.pallas_call(
        functools.partial(_s5_scan_kernel, rows=rows),
        grid=(nt, bsz, nblk),
        in_specs=[
            pl.BlockSpec((tok, LANES), lambda t, b, r: (b * nblk + r, t)),
            pl.BlockSpec((1, lc * LANES, lc * LANES), lambda t, b, r: (t, 0, 0)),
            pl.BlockSpec((1, lc * LANES, ns), lambda t, b, r: (t, 0, 0)),
            pl.BlockSpec((1, ns, lc * LANES), lambda t, b, r: (t, 0, 0)),
            pl.BlockSpec((1, 1, ns), lambda t, b, r: (t, 0, 0)),
            pl.BlockSpec((1, 1, ns), lambda t, b, r: (t, 0, 0)),
        ],
        out_specs=pl.BlockSpec((tok, LANES), lambda t, b, r: (b * nblk + r, t)),
        out_shape=jax.ShapeDtypeStruct((bsz * seq, S5_WIDTH), F32),
        scratch_shapes=[pltpu.VMEM((rows, ns), F32), pltpu.VMEM((rows, ns), F32), pltpu.VMEM((1, ns), F32)],
        compiler_params=_cparams(("parallel", "parallel", "arbitrary")),
        name="s5_scan",
    )(u, toep, w_in, v_out, a1, a2)


def _s5_post_kernel(y_ref, u_ref, d_ref, w_ref, b_ref, o_ref):
    g = jax.nn.gelu(y_ref[...] + d_ref[...] * u_ref[...], approximate=True)
    gate = jnp.dot(g.astype(BF16), w_ref[...], preferred_element_type=F32) + b_ref[...]
    o_ref[...] = (g * jax.nn.sigmoid(gate)).astype(o_ref.dtype)


def _s5_post(y, u, d_skip, w_glu, b_glu, tm=512):
    t = y.shape[0]
    tm = min(tm, t)
    row = pl.BlockSpec((tm, S5_WIDTH), lambda i: (i, 0))
    vec = pl.BlockSpec((1, S5_WIDTH), lambda i: (0, 0))
    return pl.pallas_call(
        _s5_post_kernel,
        grid=(t // tm,),
        in_specs=[row, row, vec, pl.BlockSpec((S5_WIDTH, S5_WIDTH), lambda i: (0, 0)), vec],
        out_specs=row,
        out_shape=jax.ShapeDtypeStruct((t, S5_WIDTH), BF16),
        compiler_params=_cparams(("parallel",)),
        name="s5_post",
    )(y, u, d_skip.astype(F32)[None, :], w_glu.astype(BF16), b_glu.astype(F32)[None, :])


def _ret_kernel(q_ref, k_ref, v_ref, g_ref, cos_ref, sin_ref, intra_ref, dend_ref, osc_ref, nw_ref,
                o_ref, r_ref, *, chunk_decay):
    L = RET_CHUNK

    @pl.when(pl.program_id(1) == 0)
    def _():
        r_ref[...] = jnp.zeros_like(r_ref)

    cos = cos_ref[...]
    sin = sin_ref[...]
    lane = lax.broadcasted_iota(jnp.int32, (L, LANES), 1)
    rowi = lax.broadcasted_iota(jnp.int32, (LANES, L), 0)
    half = RET_QK_DIM // 2
    outs = []
    for t in range(RET_HEADS // 2):
        sl = slice(t * LANES, (t + 1) * LANES)
        qt = q_ref[:, sl].astype(F32)
        kt = k_ref[:, sl].astype(F32)
        qr = qt * cos + pltpu.roll(qt, 64, 1) * sin
        kr = (kt * cos + pltpu.roll(kt, 64, 1) * sin) * (RET_QK_DIM ** -0.5)
        kr_b = kr.astype(BF16)
        kr_t = kr.T
        for sub in range(2):
            h = 2 * t + sub
            vs = slice(h * RET_V_DIM, (h + 1) * RET_V_DIM)
            qm = jnp.where(((lane // half) % 2) == sub, qr, 0.0).astype(BF16)
            s = lax.dot_general(qm, kr_b, (((1,), (1,)), ((), ())), preferred_element_type=F32)
            p = (s * intra_ref[h]).astype(BF16)
            vh = v_ref[:, vs]
            state = r_ref[h]
            y_in = jnp.dot(p, vh, preferred_element_type=F32)
            y_off = jnp.dot(qm, state.astype(BF16), preferred_element_type=F32) * osc_ref[:, vs]
            ktm = jnp.where(((rowi // half) % 2) == sub, kr_t * dend_ref[h:h + 1, :], 0.0).astype(BF16)
            r_ref[h] = state * chunk_decay[h] + jnp.dot(ktm, vh, preferred_element_type=F32)
            y = y_in + y_off
            mu = jnp.mean(y, -1, keepdims=True)
            yc = y - mu
            var = jnp.mean(yc * yc, -1, keepdims=True)
            yn = yc * lax.rsqrt(var + LN_EPS) * nw_ref[:, vs]
            outs.append(_silu(g_ref[:, vs].astype(F32)) * yn)
    o_ref[...] = jnp.concatenate(outs, axis=1).astype(o_ref.dtype)


def _ret_mixer(proj, bsz, seq, positions, norm_w):
    L = RET_CHUNK
    nc = seq // L
    half = RET_QK_DIM // 2
    inv_freq = 1.0 / (ROPE_BASE ** (jnp.arange(half, dtype=F32) / half))
    ang = positions.astype(F32).reshape(bsz * seq, 1) * inv_freq[None, :]
    cos32, sin32 = jnp.cos(ang), jnp.sin(ang)
    cos_t = jnp.concatenate([cos32] * 4, axis=1)
    sin_t = jnp.concatenate([-sin32, -sin32, sin32, sin32], axis=1)
    log_gamma = np.log1p(-np.exp2(-5.0 - np.arange(RET_HEADS, dtype=np.float64)))
    pos = np.arange(L, dtype=np.float64)
    rel = pos[:, None] - pos[None, :]
    intra = np.where(rel[None] >= 0, np.exp(rel[None] * log_gamma[:, None, None]), 0.0).astype(np.float32)
    dend = np.exp((L - 1 - pos)[None, :] * log_gamma[:, None]).astype(np.float32)
    osc = np.repeat(np.exp((pos + 1)[:, None] * log_gamma[None, :]), RET_V_DIM, axis=1).astype(np.float32)
    chunk_decay = tuple(float(v) for v in np.exp(L * log_gamma))
    rows = lambda b, c: b * nc + c
    full = lambda shape: pl.BlockSpec(shape, lambda b, c: (0,) * len(shape))
    return pl.pallas_call(
        functools.partial(_ret_kernel, chunk_decay=chunk_decay),
        grid=(bsz, nc),
        in_specs=[
            pl.BlockSpec((L, 512), lambda b, c: (rows(b, c), _CB_RQ)),
            pl.BlockSpec((L, 512), lambda b, c: (rows(b, c), _CB_RK)),
            pl.BlockSpec((L, 1024), lambda b, c: (rows(b, c), _CB_RV)),
            pl.BlockSpec((L, 1024), lambda b, c: (rows(b, c), _CB_RG)),
            pl.BlockSpec((L, LANES), lambda b, c: (rows(b, c), 0)),
            pl.BlockSpec((L, LANES), lambda b, c: (rows(b, c), 0)),
            full((RET_HEADS, L, L)), full((RET_HEADS, L)), full((L, RET_WIDTH)), full((1, RET_WIDTH)),
        ],
        out_specs=pl.BlockSpec((L, RET_WIDTH), lambda b, c: (rows(b, c), 0)),
        out_shape=jax.ShapeDtypeStruct((bsz * seq, RET_WIDTH), BF16),
        scratch_shapes=[pltpu.VMEM((RET_HEADS, LANES, RET_V_DIM), F32)],
        compiler_params=_cparams(("parallel", "arbitrary")),
        name="retention_mixer",
    )(proj, proj, proj, proj, cos_t, sin_t, jnp.asarray(intra), jnp.asarray(dend), jnp.asarray(osc),
      norm_w.astype(F32)[None, :])


def _gla_kernel(q_ref, k_ref, v_ref, r_ref, sm_ref, wa_ref, ba_ref, tblk_ref, nw_ref, o_ref, s_ref):
    LB = GLA_BLOCK
    LC = GLA_CHUNK

    @pl.when(pl.program_id(1) == 0)
    def _():
        s_ref[...] = jnp.zeros_like(s_ref)

    logit = jnp.dot(sm_ref[...], wa_ref[...], preferred_element_type=F32, precision=HIGHEST) + ba_ref[...]
    la = (jnp.minimum(logit, 0.0) - jnp.log1p(jnp.exp(-jnp.abs(logit)))) * (1.0 / GLA_TAU)
    b = jnp.dot(tblk_ref[...], la, preferred_element_type=F32, precision=HIGHEST)
    rowf = lax.broadcasted_iota(jnp.int32, (LB, GLA_QK), 0)
    b_end = jnp.where(rowf < LC, b[LC - 1:LC], b[LB - 1:LB])
    q = q_ref[...].astype(F32) * (GLA_QK_DIM ** -0.5)
    k = k_ref[...].astype(F32)
    q_dec = q * jnp.exp(b)
    k_inv = (k * jnp.exp(-b)).astype(BF16)
    k_dec = k * jnp.exp(b_end - b)
    row = lax.broadcasted_iota(jnp.int32, (LB, LB), 0)
    col = lax.broadcasted_iota(jnp.int32, (LB, LB), 1)
    blk_causal = (row >= col) & ((row // LC) == (col // LC))
    first_rows = row < LC
    first_vrows = lax.broadcasted_iota(jnp.int32, (LB, GLA_V_DIM), 0) < LC
    outs = []
    for h in range(GLA_HEADS):
        sl = slice(h * GLA_QK_DIM, (h + 1) * GLA_QK_DIM)
        vs = slice(h * GLA_V_DIM, (h + 1) * GLA_V_DIM)
        qd = q_dec[:, sl]
        qd_b = qd.astype(BF16)
        att = lax.dot_general(qd_b, k_inv[:, sl], (((1,), (1,)), ((), ())), preferred_element_type=F32)
        att = jnp.where(blk_causal, att, 0.0).astype(BF16)
        vh = v_ref[:, vs]
        y = jnp.dot(att, vh, preferred_element_type=F32)
        s0 = s_ref[h]
        kd_t = k_dec[:, sl].T.astype(BF16)
        kv0 = jnp.dot(kd_t, jnp.where(first_vrows, vh, jnp.zeros_like(vh)), preferred_element_type=F32)
        kv1 = jnp.dot(kd_t, jnp.where(first_vrows, jnp.zeros_like(vh), vh), preferred_element_type=F32)
        b_t = b[:, sl].T
        s1 = s0 * jnp.exp(b_t[:, LC - 1:LC]) + kv0
        s_ref[h] = s1 * jnp.exp(b_t[:, LB - 1:LB]) + kv1
        y = y + jnp.dot(jnp.where(first_rows, qd, 0.0).astype(BF16), s0.astype(BF16), preferred_element_type=F32)
        y = y + jnp.dot(jnp.where(first_rows, 0.0, qd).astype(BF16), s1.astype(BF16), preferred_element_type=F32)
        ms = jnp.mean(y * y, -1, keepdims=True)
        yn = y * lax.rsqrt(ms + RMS_EPS) * nw_ref[:, vs]
        outs.append(yn * _silu(r_ref[:, vs].astype(F32)))
    o_ref[...] = jnp.concatenate(outs, axis=1).astype(o_ref.dtype)


def _gla_mixer(proj, small, bsz, seq, w_alpha, b_alpha, norm_w):
    LB = GLA_BLOCK
    nb = seq // LB
    wa = jnp.zeros((LANES, GLA_QK), F32).at[SSD_HEADS:SSD_HEADS + GLA_GATE_RANK].set(w_alpha.astype(F32))
    idx = np.arange(LB)
    tblk = ((idx[:, None] >= idx[None, :]) & ((idx[:, None] // GLA_CHUNK) == (idx[None, :] // GLA_CHUNK)))
    rows = lambda b, c: b * nb + c
    full = lambda shape: pl.BlockSpec(shape, lambda b, c: (0,) * len(shape))
    return pl.pallas_call(
        _gla_kernel,
        grid=(bsz, nb),
        in_specs=[
            pl.BlockSpec((LB, 512), lambda b, c: (rows(b, c), _CB_GQ)),
            pl.BlockSpec((LB, 512), lambda b, c: (rows(b, c), _CB_GK)),
            pl.BlockSpec((LB, 1024), lambda b, c: (rows(b, c), _CB_GV)),
            pl.BlockSpec((LB, 1024), lambda b, c: (rows(b, c), _CB_GR)),
            pl.BlockSpec((LB, LANES), lambda b, c: (rows(b, c), 0)),
            full((LANES, GLA_QK)), full((1, GLA_QK)), full((LB, LB)), full((1, GLA_WIDTH)),
        ],
        out_specs=pl.BlockSpec((LB, GLA_WIDTH), lambda b, c: (rows(b, c), 0)),
        out_shape=jax.ShapeDtypeStruct((bsz * seq, GLA_WIDTH), BF16),
        scratch_shapes=[pltpu.VMEM((GLA_HEADS, GLA_QK_DIM, GLA_V_DIM), F32)],
        compiler_params=_cparams(("parallel", "arbitrary")),
        name="gla_mixer",
    )(proj, proj, proj, proj, small, wa, b_alpha.astype(F32)[None, :], jnp.asarray(tblk.astype(np.float32)),
      norm_w.astype(F32)[None, :])


def _merge_kernel(ya_ref, yb_ref, yc_ref, yd_ref, g0_ref, g1_ref, g2_ref, g3_ref, wb_ref, o_ref):
    acc = None
    for n, (y_ref, g_ref) in enumerate(((ya_ref, g0_ref), (yb_ref, g1_ref), (yc_ref, g2_ref), (yd_ref, g3_ref))):
        br = jnp.dot(y_ref[...], wb_ref[n], preferred_element_type=F32)
        term = jax.nn.sigmoid(g_ref[...].astype(F32)) * br
        acc = term if acc is None else acc + term
    o_ref[...] = acc.astype(o_ref.dtype)


def _merge(ys, proj, w_branch, tm=512, tn=512):
    t = ys[0].shape[0]
    tm = min(tm, t)
    nj = D_MODEL // tn
    yspec = pl.BlockSpec((tm, BRANCH_WIDTH), lambda i, j: (i, 0))
    gspec = lambda n: pl.BlockSpec((tm, tn), lambda i, j: (i, n * nj + j))
    return pl.pallas_call(
        _merge_kernel,
        grid=(t // tm, nj),
        in_specs=[yspec] * 4 + [gspec(n) for n in range(N_BRANCH)]
                 + [pl.BlockSpec((N_BRANCH, BRANCH_WIDTH, tn), lambda i, j: (0, 0, j))],
        out_specs=pl.BlockSpec((tm, tn), lambda i, j: (i, j)),
        out_shape=jax.ShapeDtypeStruct((t, D_MODEL), BF16),
        compiler_params=_cparams(("parallel", "parallel")),
        name="branch_merge",
    )(*ys, proj, proj, proj, proj, w_branch.astype(BF16))


def _proj_ln_kernel(m_ref, w_ref, h_ref, lw_ref, lb_ref, o_ref, ob_ref):
    mix = jnp.dot(m_ref[...], w_ref[...], preferred_element_type=F32)
    out = _layer_norm(DEEPNORM_ALPHA * h_ref[...] + mix, lw_ref[...], lb_ref[...])
    o_ref[...] = out
    ob_ref[...] = out.astype(BF16)


def _proj_ln(merged, w_out, h, ln_w, ln_b, tm=512):
    t = h.shape[0]
    tm = min(tm, t)
    row = pl.BlockSpec((tm, D_MODEL), lambda i: (i, 0))
    vec = pl.BlockSpec((1, D_MODEL), lambda i: (0, 0))
    return pl.pallas_call(
        _proj_ln_kernel,
        grid=(t // tm,),
        in_specs=[row, pl.BlockSpec((D_MODEL, D_MODEL), lambda i: (0, 0)), row, vec, vec],
        out_specs=[row, row],
        out_shape=[jax.ShapeDtypeStruct((t, D_MODEL), F32), jax.ShapeDtypeStruct((t, D_MODEL), BF16)],
        compiler_params=_cparams(("parallel",)),
        name="out_proj_ln",
    )(merged, w_out.astype(BF16), h, ln_w.astype(F32)[None, :], ln_b.astype(F32)[None, :])


def _router_kernel(h_ref, w_ref, b_ref, idx_ref, gate_ref):
    logits = lax.dot_general(w_ref[...], h_ref[...], (((1,), (1,)), ((), ())),
                             preferred_element_type=F32, precision=HIGHEST) + b_ref[...]
    eid = lax.broadcasted_iota(jnp.int32, logits.shape, 0)
    vals = logits
    top_v, top_i = [], []
    for _ in range(TOP_K):
        m = jnp.max(vals, axis=0, keepdims=True)
        sel = jnp.min(jnp.where(vals == m, eid, N_EXPERTS), axis=0, keepdims=True)
        top_v.append(m)
        top_i.append(sel)
        vals = jnp.where(eid == sel, -jnp.inf, vals)
    ex = [jnp.exp(v - top_v[0]) for v in top_v]
    den = ex[0] + ex[1] + ex[2] + ex[3]
    zi = jnp.zeros_like(top_i[0])
    zf = jnp.zeros_like(den)
    idx_ref[...] = jnp.concatenate(top_i + [zi] * (8 - TOP_K), axis=0)
    gate_ref[...] = jnp.concatenate([e / den for e in ex] + [zf] * (8 - TOP_K), axis=0)


def _router(h, router_w, router_b, tm=512):
    t = h.shape[0]
    tm = min(tm, t)
    return pl.pallas_call(
        _router_kernel,
        grid=(t // tm,),
        in_specs=[pl.BlockSpec((tm, D_MODEL), lambda i: (i, 0)),
                  pl.BlockSpec((N_EXPERTS, D_MODEL), lambda i: (0, 0)),
                  pl.BlockSpec((N_EXPERTS, 1), lambda i: (0, 0))],
        out_specs=[pl.BlockSpec((8, tm), lambda i: (0, i)), pl.BlockSpec((8, tm), lambda i: (0, i))],
        out_shape=[jax.ShapeDtypeStruct((8, t), jnp.int32), jax.ShapeDtypeStruct((8, t), F32)],
        compiler_params=_cparams(("parallel",)),
        name="router_topk",
    )(h, router_w.astype(F32).T, router_b.astype(F32)[:, None])


def _expert_kernel(be_ref, nb_ref, tok_ref, x_ref, wgu_ref, bgu_ref, wd_ref, bd_ref, o_ref,
                   idx_ref, xbuf_ref, isem, rsem):
    i = pl.program_id(0)
    n_used = nb_ref[0]
    blk = MOE_BLOCK

    def fetch_rows(block, slot):
        cp = pltpu.make_async_copy(tok_ref.at[block], idx_ref.at[slot], isem.at[slot])
        cp.start()
        cp.wait()

        def issue(r, carry):
            pltpu.make_async_copy(x_ref.at[pl.ds(idx_ref[slot, r], 1)], xbuf_ref.at[slot, pl.ds(r, 1)],
                                  rsem.at[slot]).start()
            return carry

        lax.fori_loop(0, blk, issue, 0)

    @pl.when(i == 0)
    def _():
        fetch_rows(0, 0)

    slot = i % 2

    @pl.when(i < n_used)
    def _():
        pltpu.make_async_copy(x_ref.at[pl.ds(0, blk)], xbuf_ref.at[slot], rsem.at[slot]).wait()

    @pl.when(i + 1 < n_used)
    def _():
        fetch_rows(i + 1, 1 - slot)

    @pl.when(i < n_used)
    def _():
        xb = xbuf_ref[slot].astype(BF16)
        hgu = jnp.dot(xb, wgu_ref[0], preferred_element_type=F32) + bgu_ref[0]
        g = jnp.minimum(hgu[:, :D_EXPERT], SWIGLU_LIMIT)
        u = jnp.clip(hgu[:, D_EXPERT:], -SWIGLU_LIMIT, SWIGLU_LIMIT)
        act = (u + 1.0) * (g * jax.nn.sigmoid(SWIGLU_ALPHA * g))
        o_ref[...] = jnp.dot(act.astype(BF16), wd_ref[0], preferred_element_type=F32) + bd_ref[0]

    @pl.when(i >= n_used)
    def _():
        o_ref[...] = jnp.zeros_like(o_ref)


def _experts(h, block_expert, n_used, row_token, w_gate_up, b_gate_up, w_down, b_down):
    n_blocks = row_token.shape[0]
    blk = MOE_BLOCK
    grid_spec = pltpu.PrefetchScalarGridSpec(
        num_scalar_prefetch=2,
        grid=(n_blocks,),
        in_specs=[
            pl.BlockSpec(memory_space=pl.ANY),
            pl.BlockSpec(memory_space=pl.ANY),
            pl.BlockSpec((1, D_MODEL, 2 * D_EXPERT), lambda i, be, nb: (be[i], 0, 0)),
            pl.BlockSpec((1, 1, 2 * D_EXPERT), lambda i, be, nb: (be[i], 0, 0)),
            pl.BlockSpec((1, D_EXPERT, D_MODEL), lambda i, be, nb: (be[i], 0, 0)),
            pl.BlockSpec((1, 1, D_MODEL), lambda i, be, nb: (be[i], 0, 0)),
        ],
        out_specs=pl.BlockSpec((blk, D_MODEL), lambda i, be, nb: (i, 0)),
        scratch_shapes=[pltpu.SMEM((2, blk), jnp.int32), pltpu.VMEM((2, blk, D_MODEL), F32),
                        pltpu.SemaphoreType.DMA((2,)), pltpu.SemaphoreType.DMA((2,))],
    )
    return pl.pallas_call(
        _expert_kernel,
        grid_spec=grid_spec,
        out_shape=jax.ShapeDtypeStruct((n_blocks * blk, D_MODEL), F32),
        compiler_params=_cparams(("arbitrary",)),
        name="expert_ffn",
    )(block_expert, n_used, row_token, h, w_gate_up.astype(BF16), b_gate_up.astype(F32)[:, None, :],
      w_down.astype(BF16), b_down.astype(F32)[:, None, :])


def _combine_kernel(dest_ref, yb_ref, gate_ref, h_ref, lw_ref, lb_ref, o_ref, ob_ref, idx_ref, buf_ref, isem, rsem,
                    *, tm):
    i = pl.program_id(0)
    n = pl.num_programs(0)

    def fetch_rows(tile, slot):
        cp = pltpu.make_async_copy(dest_ref.at[tile], idx_ref.at[slot], isem.at[slot])
        cp.start()
        cp.wait()

        def issue(r, carry):
            for k in range(TOP_K):
                pltpu.make_async_copy(yb_ref.at[pl.ds(idx_ref[slot, k * tm + r], 1)],
                                      buf_ref.at[slot, k, pl.ds(r, 1)], rsem.at[slot]).start()
            return carry

        lax.fori_loop(0, tm, issue, 0)

    @pl.when(i == 0)
    def _():
        fetch_rows(0, 0)

    slot = i % 2
    for k in range(TOP_K):
        pltpu.make_async_copy(yb_ref.at[pl.ds(0, tm)], buf_ref.at[slot, k], rsem.at[slot]).wait()

    @pl.when(i + 1 < n)
    def _():
        fetch_rows(i + 1, 1 - slot)

    gates = gate_ref[...]
    ffn = gates[:, 0:1] * buf_ref[slot, 0]
    for k in range(1, TOP_K):
        ffn = ffn + gates[:, k:k + 1] * buf_ref[slot, k]
    out = _layer_norm(DEEPNORM_ALPHA * h_ref[...] + ffn, lw_ref[...], lb_ref[...])
    o_ref[...] = out
    ob_ref[...] = out.astype(BF16)


def _combine(dest, yb, gates, h, ln_w, ln_b, tm=256):
    t = h.shape[0]
    tm = min(tm, t)
    row = pl.BlockSpec((tm, D_MODEL), lambda i: (i, 0))
    vec = pl.BlockSpec((1, D_MODEL), lambda i: (0, 0))
    dest_tiles = dest.reshape(TOP_K, t // tm, tm).transpose(1, 0, 2).reshape(t // tm, TOP_K * tm)
    return pl.pallas_call(
        functools.partial(_combine_kernel, tm=tm),
        grid=(t // tm,),
        in_specs=[pl.BlockSpec(memory_space=pl.ANY), pl.BlockSpec(memory_space=pl.ANY),
                  pl.BlockSpec((tm, 8), lambda i: (i, 0)), row, vec, vec],
        out_specs=[row, row],
        out_shape=[jax.ShapeDtypeStruct((t, D_MODEL), F32), jax.ShapeDtypeStruct((t, D_MODEL), BF16)],
        scratch_shapes=[pltpu.SMEM((2, TOP_K * tm), jnp.int32), pltpu.VMEM((2, TOP_K, tm, D_MODEL), F32),
                        pltpu.SemaphoreType.DMA((2,)), pltpu.SemaphoreType.DMA((2,))],
        compiler_params=_cparams(("arbitrary",)),
        name="moe_combine_ln",
    )(dest_tiles, yb, gates, h, ln_w.astype(F32)[None, :], ln_b.astype(F32)[None, :])


def _moe(h, router_w, router_b, w_gate_up, b_gate_up, w_down, b_down, ln_w, ln_b):
    t = h.shape[0]
    blk = MOE_BLOCK
    top_idx, top_gate = _router(h, router_w, router_b)
    e_flat = top_idx[:TOP_K].reshape(-1)
    n_assign = TOP_K * t
    onehot = (e_flat[:, None] == jnp.arange(N_EXPERTS, dtype=jnp.int32)[None, :]).astype(jnp.int32)
    csum = jnp.cumsum(onehot, axis=0)
    counts = csum[-1]
    rank = jnp.sum((csum - onehot) * onehot, axis=1)
    padded = (counts + blk - 1) // blk * blk
    pad_end = jnp.cumsum(padded)
    pad_start = pad_end - padded
    grp_start = jnp.cumsum(counts) - counts
    dest = (jnp.sum(onehot * pad_start[None, :], axis=1) + rank).astype(jnp.int32)
    n_blocks = n_assign // blk + N_EXPERTS
    blocks = jnp.arange(n_blocks, dtype=jnp.int32)
    block_expert = jnp.minimum(jnp.searchsorted(pad_end // blk, blocks, side='right'), N_EXPERTS - 1).astype(jnp.int32)
    n_used = (pad_end[-1] // blk).astype(jnp.int32).reshape(1)
    order_key = jnp.sort(e_flat * n_assign + jnp.arange(n_assign, dtype=jnp.int32))
    tok_sorted = (order_key % n_assign) % t
    tok_sorted = jnp.concatenate([tok_sorted, jnp.zeros((blk,), jnp.int32)])
    src = jnp.clip(grp_start[block_expert] + blocks * blk - pad_start[block_expert], 0, n_assign)
    row_token = jax.vmap(lambda s: lax.dynamic_slice(tok_sorted, (s,), (blk,)))(src)
    yb = _experts(h, block_expert, n_used, row_token, w_gate_up, b_gate_up, w_down, b_down)
    return _combine(dest.reshape(TOP_K, t), yb, top_gate.T, h, ln_w, ln_b)


def _ple_kernel(hb_ref, h_ref, p_ref, wg_ref, wp_ref, lw_ref, lb_ref, o_ref, ob_ref):
    gate = jax.nn.sigmoid(jnp.dot(hb_ref[...], wg_ref[...], preferred_element_type=F32))
    emb = jnp.dot(p_ref[...].astype(BF16), wp_ref[...], preferred_element_type=F32)
    out = _layer_norm(DEEPNORM_ALPHA * h_ref[...] + gate * emb, lw_ref[...], lb_ref[...])
    o_ref[...] = out
    ob_ref[...] = out.astype(BF16)


def _ple(hb, h, p_i, w_gate, w_proj, ln_w, ln_b, tm=512):
    t = h.shape[0]
    tm = min(tm, t)
    row = pl.BlockSpec((tm, D_MODEL), lambda i: (i, 0))
    vec = pl.BlockSpec((1, D_MODEL), lambda i: (0, 0))
    return pl.pallas_call(
        _ple_kernel,
        grid=(t // tm,),
        in_specs=[row, row, pl.BlockSpec((tm, PLE_DIM), lambda i: (i, 0)),
                  pl.BlockSpec((D_MODEL, D_MODEL), lambda i: (0, 0)),
                  pl.BlockSpec((PLE_DIM, D_MODEL), lambda i: (0, 0)), vec, vec],
        out_specs=[row, row],
        out_shape=[jax.ShapeDtypeStruct((t, D_MODEL), F32), jax.ShapeDtypeStruct((t, D_MODEL), BF16)],
        compiler_params=_cparams(("parallel",)),
        name="ple_ln",
    )(hb, h, p_i, w_gate.astype(BF16), w_proj.astype(BF16), ln_w.astype(F32)[None, :], ln_b.astype(F32)[None, :])


def _token_mixer(h, hb, bsz, seq, positions, w_in, ssd_conv_w, ssd_conv_b, ssd_dt_bias, ssd_a_log, ssd_d,
                 ssd_norm_w, s5_lambda_re, s5_lambda_im, s5_log_dt, s5_b_re, s5_b_im, s5_c_re, s5_c_im, s5_d,
                 s5_w_glu, s5_b_glu, ret_norm_w, gla_w_alpha, gla_b_alpha, gla_norm_w, w_branch, s5_rows):
    w_main = w_in[:, _main_columns()].astype(BF16)
    w_u = w_in[:, _OFF_U:_OFF_U + S5_WIDTH].astype(BF16)
    w_small = jnp.concatenate([w_in[:, _OFF_DT:_OFF_DT + SSD_HEADS], w_in[:, _OFF_GC:_OFF_GC + GLA_GATE_RANK],
                               jnp.zeros((D_MODEL, LANES - SSD_HEADS - GLA_GATE_RANK), w_in.dtype)], axis=1)
    proj = _matmul(hb, w_main, BF16, 1024, 1536, name="in_proj_main")
    u = _matmul(hb, w_u, F32, 1024, 1024, name="in_proj_s5")
    small = _matmul(h, w_small.astype(F32), F32, 512, LANES, precision=HIGHEST, name="in_proj_small")
    y_a = _ssd_mixer(proj, small, bsz, seq, ssd_conv_w, ssd_conv_b, ssd_dt_bias, ssd_a_log, ssd_d, ssd_norm_w)
    tables = _s5_tables(s5_lambda_re, s5_lambda_im, s5_log_dt, s5_b_re, s5_b_im, s5_c_re, s5_c_im)
    y_b = _s5_post(_s5_scan(u, bsz, seq, tables, s5_rows), u, s5_d, s5_w_glu, s5_b_glu)
    y_c = _ret_mixer(proj, bsz, seq, positions, ret_norm_w)
    y_d = _gla_mixer(proj, small, bsz, seq, gla_w_alpha, gla_b_alpha, gla_norm_w)
    return _merge((y_a, y_b, y_c, y_d), proj, w_branch)


def kernel(x, p, positions, w_in, ssd_conv_w, ssd_conv_b, ssd_dt_bias, ssd_a_log, ssd_d, ssd_norm_w, s5_lambda_re, s5_lambda_im, s5_log_dt, s5_b_re, s5_b_im, s5_c_re, s5_c_im, s5_d, s5_w_glu, s5_b_glu, ret_norm_w, gla_w_alpha, gla_b_alpha, gla_norm_w, w_branch, w_out, ln1_w, ln1_b, router_w, router_b, moe_w_gate_up, moe_b_gate_up, moe_w_down, moe_b_down, ln2_w, ln2_b, ple_w_gate, ple_w_proj, ln3_w, ln3_b):
    s5_rows = 256
    bsz, seq, d = x.shape
    t = bsz * seq
    h = x.reshape(t, d).astype(F32)
    hb = h.astype(BF16)
    for i in range(DEPTH):
        merged = _token_mixer(h, hb, bsz, seq, positions, w_in[i], ssd_conv_w[i], ssd_conv_b[i], ssd_dt_bias[i],
                              ssd_a_log[i], ssd_d[i], ssd_norm_w[i], s5_lambda_re[i], s5_lambda_im[i],
                              s5_log_dt[i], s5_b_re[i], s5_b_im[i], s5_c_re[i], s5_c_im[i], s5_d[i],
                              s5_w_glu[i], s5_b_glu[i], ret_norm_w[i], gla_w_alpha[i], gla_b_alpha[i],
                              gla_norm_w[i], w_branch[i], s5_rows)
        h, hb = _proj_ln(merged, w_out[i], h, ln1_w[i], ln1_b[i])
        h, hb = _moe(h, router_w[i], router_b[i], moe_w_gate_up[i], moe_b_gate_up[i], moe_w_down[i],
                     moe_b_down[i], ln2_w[i], ln2_b[i])
        h, hb = _ple(hb, h, p[i].reshape(t, PLE_DIM), ple_w_gate[i], ple_w_proj[i], ln3_w[i], ln3_b[i])
    return h.reshape(bsz, seq, d).astype(x.dtype)
```

```python
import functools
import math

import numpy as np
import jax
import jax.numpy as jnp
from jax import lax
from jax.experimental import pallas as pl
from jax.experimental.pallas import tpu as pltpu

F32 = jnp.float32
BF16 = jnp.bfloat16
HIGHEST = lax.Precision.HIGHEST

D_MODEL = 2048
DEPTH = 2
PLE_DIM = 256
N_BRANCH = 4
BRANCH_WIDTH = 1024

SSD_HEADS = 16
SSD_HEAD_DIM = 64
SSD_INNER = SSD_HEADS * SSD_HEAD_DIM
SSD_GROUPS = 2
SSD_STATE = 128
SSD_CONV = 4
SSD_CHUNK = 128

S5_WIDTH = 1024
S5_GROUP = 16
S5_GROUPS = S5_WIDTH // S5_GROUP
S5_STATE = 64
S5_CHUNK = 16
S5_TILE_GROUPS = 8

RET_HEADS = 8
RET_QK_DIM = 64
RET_V_DIM = 128
RET_QK = RET_HEADS * RET_QK_DIM
RET_WIDTH = RET_HEADS * RET_V_DIM
RET_CHUNK = 128
ROPE_BASE = 10000.0

GLA_HEADS = 4
GLA_QK_DIM = 128
GLA_V_DIM = 256
GLA_QK = GLA_HEADS * GLA_QK_DIM
GLA_WIDTH = GLA_HEADS * GLA_V_DIM
GLA_GATE_RANK = 16
GLA_TAU = 16.0
GLA_CHUNK = 64
GLA_BLOCK = 128

N_EXPERTS = 32
TOP_K = 4
D_EXPERT = 1024
SWIGLU_ALPHA = 1.702
SWIGLU_LIMIT = 7.0
MOE_BLOCK = 256

DEEPNORM_ALPHA = (2 * DEPTH) ** 0.25
LN_EPS = 1e-5
RMS_EPS = 1e-6

LANES = 128
VMEM_LIMIT = 56 * 1024 * 1024

_OFF_Z = 0
_OFF_XS = 1024
_OFF_BC = 2048
_OFF_DT = 2560
_OFF_U = 2576
_OFF_RQ = 3600
_OFF_RK = 4112
_OFF_RV = 4624
_OFF_RG = 5648
_OFF_GQ = 6672
_OFF_GK = 7184
_OFF_GV = 7696
_OFF_GR = 8720
_OFF_GC = 9744
_OFF_GATE = 9760

_CB_Z, _CB_XS, _CB_RV, _CB_RG, _CB_GV, _CB_GR = 8, 9, 10, 11, 12, 13
_CB_BC, _CB_RQ, _CB_RK, _CB_GQ, _CB_GK = 28, 29, 30, 31, 32
N_MAIN = 16896


def _main_weight(w_in):
    def cols(off, n):
        return w_in[:, off:off + n].astype(BF16)

    def rotary(off):
        half = RET_QK_DIM // 2
        w = cols(off, RET_QK).reshape(D_MODEL, RET_HEADS // 2, 2, 2, half)
        return w.transpose(0, 1, 3, 2, 4).reshape(D_MODEL, RET_QK)

    return jnp.concatenate([
        cols(_OFF_GATE, N_BRANCH * D_MODEL), cols(_OFF_Z, 1024), cols(_OFF_XS, 1024), cols(_OFF_RV, 1024),
        cols(_OFF_RG, 1024), cols(_OFF_GV, 1024), cols(_OFF_GR, 1024), cols(_OFF_BC, 512),
        rotary(_OFF_RQ), rotary(_OFF_RK), cols(_OFF_GQ, 512), cols(_OFF_GK, 512)], axis=1)


def _cparams(sem):
    return pltpu.CompilerParams(dimension_semantics=sem, vmem_limit_bytes=VMEM_LIMIT)


def _silu(x):
    return x * jax.nn.sigmoid(x)


def _softplus(x):
    return jnp.maximum(x, 0.0) + jnp.log1p(jnp.exp(-jnp.abs(x)))


def _layer_norm(x, w, b):
    mu = jnp.mean(x, -1, keepdims=True)
    xc = x - mu
    var = jnp.mean(xc * xc, -1, keepdims=True)
    return xc * lax.rsqrt(var + LN_EPS) * w + b


def _mm_kernel(a_ref, b_ref, o_ref, *, precision):
    o_ref[...] = jnp.dot(a_ref[...], b_ref[...], preferred_element_type=F32,
                         precision=precision).astype(o_ref.dtype)


def _matmul(a, b, out_dtype, tm, tn, precision=None, name="matmul"):
    m, k = a.shape
    n = b.shape[1]
    tm, tn = min(tm, m), min(tn, n)
    return pl.pallas_call(
        functools.partial(_mm_kernel, precision=precision),
        grid=(m // tm, n // tn),
        in_specs=[pl.BlockSpec((tm, k), lambda i, j: (i, 0)),
                  pl.BlockSpec((k, tn), lambda i, j: (0, j))],
        out_specs=pl.BlockSpec((tm, tn), lambda i, j: (i, j)),
        out_shape=jax.ShapeDtypeStruct((m, n), out_dtype),
        compiler_params=_cparams(("parallel", "parallel")),
        name=name,
    )(a, b)


def _causal_conv(x, tail, w, b):
    n_tap = SSD_CONV
    acc = x * w[n_tap - 1:n_tap] + b
    x8 = x[0:8]
    acc8 = x8 * w[n_tap - 1:n_tap] + b
    row8 = lax.broadcasted_iota(jnp.int32, x8.shape, 0)
    for s in range(1, n_tap):
        wk = w[n_tap - 1 - s:n_tap - s]
        acc = acc + pltpu.roll(x, s, 0) * wk
        v8 = jnp.where(row8 < s, pltpu.roll(tail, s, 0), pltpu.roll(x8, s, 0))
        acc8 = acc8 + v8 * wk
    return jnp.concatenate([acc8, acc[8:]], axis=0)


def _ssd_kernel(z_ref, xs_ref, bc_ref, sm_ref, cwx_ref, cwbc_ref, cbx_ref, cbbc_ref, dtb_ref,
                nega_ref, dsk_ref, nw_ref, exp_ref, tril_ref, o_ref, tailx_ref, tailbc_ref, ht_ref):
    L = SSD_CHUNK

    @pl.when(pl.program_id(1) == 0)
    def _():
        tailx_ref[...] = jnp.zeros_like(tailx_ref)
        tailbc_ref[...] = jnp.zeros_like(tailbc_ref)
        ht_ref[...] = jnp.zeros_like(ht_ref)

    x_raw = xs_ref[...].astype(F32)
    bc_raw = bc_ref[...].astype(F32)
    xs = _silu(_causal_conv(x_raw, tailx_ref[...], cwx_ref[...], cbx_ref[...]))
    bc = _silu(_causal_conv(bc_raw, tailbc_ref[...], cwbc_ref[...], cbbc_ref[...]))
    tailx_ref[...] = x_raw[L - 8:L]
    tailbc_ref[...] = bc_raw[L - 8:L]

    dt = _softplus(sm_ref[...] + dtb_ref[...])
    loga = dt * nega_ref[...]
    acs = jnp.dot(tril_ref[...], loga, preferred_element_type=F32, precision=HIGHEST)
    acs_t = acs.T
    expand = exp_ref[...]
    dt_e = jnp.dot(dt, expand, preferred_element_type=F32, precision=HIGHEST)
    acs_e = jnp.dot(acs, expand, preferred_element_type=F32, precision=HIGHEST)
    last_e = acs_e[L - 1:L]
    xdt = xs * dt_e
    xdt_b = xdt.astype(BF16)
    xw_b = (xdt * jnp.exp(last_e - acs_e)).astype(BF16)
    dec_in = jnp.exp(acs_e)
    dec_chunk = jnp.exp(last_e)

    row = lax.broadcasted_iota(jnp.int32, (L, L), 0)
    col = lax.broadcasted_iota(jnp.int32, (L, L), 1)
    causal = row >= col
    lo_half = col < SSD_HEAD_DIM
    heads_per_group = SSD_HEADS // SSD_GROUPS
    n_bc = SSD_GROUPS * SSD_STATE

    y_tiles = []
    for g in range(SSD_GROUPS):
        bm = bc[:, g * SSD_STATE:(g + 1) * SSD_STATE]
        cm_b = bc[:, n_bc + g * SSD_STATE:n_bc + (g + 1) * SSD_STATE].astype(BF16)
        bm_t_b = bm.T.astype(BF16)
        cb = jnp.dot(cm_b, bm_t_b, preferred_element_type=F32)
        for j in range(g * heads_per_group // 2, (g + 1) * heads_per_group // 2):
            sl = slice(j * LANES, (j + 1) * LANES)
            xt = xdt_b[:, sl]
            y_pair = None
            for sub in range(2):
                h = 2 * j + sub
                seg = acs[:, h:h + 1] - acs_t[h:h + 1, :]
                m = (cb * jnp.exp(jnp.where(causal, seg, -jnp.inf))).astype(BF16)
                xh = jnp.where(lo_half if sub == 0 else ~lo_half, xt, jnp.zeros_like(xt))
                yd = jnp.dot(m, xh, preferred_element_type=F32)
                y_pair = yd if y_pair is None else y_pair + yd
            h_t = ht_ref[:, sl]
            y_off = jnp.dot(cm_b, h_t.astype(BF16), preferred_element_type=F32) * dec_in[:, sl]
            st = jnp.dot(bm_t_b, xw_b[:, sl], preferred_element_type=F32)
            ht_ref[:, sl] = h_t * dec_chunk[:, sl] + st
            y_tiles.append(y_pair + y_off)
    y = jnp.concatenate(y_tiles, axis=1) + dsk_ref[...] * xs
    y = y * _silu(z_ref[...].astype(F32))
    gw = SSD_INNER // SSD_GROUPS
    outs = []
    for g in range(SSD_GROUPS):
        yg = y[:, g * gw:(g + 1) * gw]
        ms = jnp.mean(yg * yg, -1, keepdims=True)
        outs.append(yg * lax.rsqrt(ms + RMS_EPS) * nw_ref[:, g * gw:(g + 1) * gw])
    o_ref[...] = jnp.concatenate(outs, axis=1).astype(o_ref.dtype)


def _ssd_mixer(proj, small, bsz, seq, conv_w, conv_b, dt_bias, a_log, d_skip, norm_w):
    L = SSD_CHUNK
    nc = seq // L
    pad_lanes = LANES - SSD_HEADS
    cw = jnp.pad(conv_w.astype(F32), ((0, 8 - SSD_CONV), (0, 0)))
    cwx, cwbc = cw[:, :SSD_INNER], cw[:, SSD_INNER:]
    cb = conv_b.astype(F32)[None, :]
    cbx, cbbc = cb[:, :SSD_INNER], cb[:, SSD_INNER:]
    dtb = jnp.pad(dt_bias.astype(F32), (0, pad_lanes))[None, :]
    nega = jnp.pad(-jnp.exp(a_log.astype(F32)), (0, pad_lanes))[None, :]
    dsk = jnp.repeat(d_skip.astype(F32), SSD_HEAD_DIM)[None, :]
    nw = norm_w.astype(F32)[None, :]
    expand = (np.arange(LANES)[:, None] == (np.arange(SSD_INNER)[None, :] // SSD_HEAD_DIM)).astype(np.float32)
    tril = np.tril(np.ones((L, L), np.float32))
    rows = lambda b, c: b * nc + c
    full = lambda shape: pl.BlockSpec(shape, lambda b, c: (0,) * len(shape))
    return pl.pallas_call(
        _ssd_kernel,
        grid=(bsz, nc),
        in_specs=[
            pl.BlockSpec((L, 1024), lambda b, c: (rows(b, c), _CB_Z)),
            pl.BlockSpec((L, 1024), lambda b, c: (rows(b, c), _CB_XS)),
            pl.BlockSpec((L, 512), lambda b, c: (rows(b, c), _CB_BC)),
            pl.BlockSpec((L, LANES), lambda b, c: (rows(b, c), 0)),
            full((8, SSD_INNER)), full((8, 512)), full((1, SSD_INNER)), full((1, 512)),
            full((1, LANES)), full((1, LANES)), full((1, SSD_INNER)), full((1, SSD_INNER)),
            full((LANES, SSD_INNER)), full((L, L)),
        ],
        out_specs=pl.BlockSpec((L, SSD_INNER), lambda b, c: (rows(b, c), 0)),
        out_shape=jax.ShapeDtypeStruct((bsz * seq, SSD_INNER), BF16),
        scratch_shapes=[pltpu.VMEM((8, SSD_INNER), F32), pltpu.VMEM((8, 512), F32),
                        pltpu.VMEM((SSD_STATE, SSD_INNER), F32)],
        compiler_params=_cparams(("parallel", "arbitrary")),
        name="ssd_mixer",
    )(proj, proj, proj, small, cwx, cwbc, cbx, cbbc, dtb, nega, dsk, nw, jnp.asarray(expand), jnp.asarray(tril))


def _s5_scan_kernel(u_ref, bd_ref, w_ref, v_ref, a1_ref, a2_ref, y_ref, toep_ref, sc_ref, hin_ref, hc_ref, *, rows):
    R = rows

    @pl.when((pl.program_id(1) == 0) & (pl.program_id(2) == 0))
    def _():
        toep_ref[...] = jnp.zeros_like(toep_ref)
        for s in range(S5_CHUNK):
            for l in range(s, S5_CHUNK):
                toep_ref[s * LANES:(s + 1) * LANES, l * LANES:(l + 1) * LANES] = bd_ref[0, l - s]

    @pl.when(pl.program_id(2) == 0)
    def _():
        hc_ref[...] = jnp.zeros_like(hc_ref)

    parts = [u_ref[pl.ds(s, R, stride=S5_CHUNK), :].astype(BF16) for s in range(S5_CHUNK)]
    uf = jnp.concatenate(parts, axis=1)
    y_intra = jnp.dot(uf, toep_ref[...], preferred_element_type=F32)
    sc_ref[...] = jnp.dot(uf, w_ref[0], preferred_element_type=F32)
    a1 = a1_ref[0]
    a2 = a2_ref[0]
    half = S5_TILE_GROUPS * S5_STATE

    def step(r, h):
        hin_ref[pl.ds(r, 1), :] = h
        return a1 * h + a2 * pltpu.roll(h, half, 1) + sc_ref[pl.ds(r, 1), :]

    hc_ref[...] = lax.fori_loop(0, R, step, hc_ref[...])
    y = y_intra + jnp.dot(hin_ref[...].astype(BF16), v_ref[0], preferred_element_type=F32)
    for l in range(S5_CHUNK):
        y_ref[pl.ds(l, R, stride=S5_CHUNK), :] = y[:, l * LANES:(l + 1) * LANES]


def _s5_tables(lam_re, lam_im, log_dt, b_re, b_im, c_re, c_im):
    f32 = F32
    lc = S5_CHUNK
    tg = S5_TILE_GROUPS
    nt = S5_GROUPS // tg
    lr = jnp.minimum(lam_re.astype(f32), -1e-4)[None, :]
    li = lam_im.astype(f32)[None, :]
    dt = jnp.exp(log_dt.astype(f32))[:, None]
    mag = jnp.exp(lr * dt)
    ab_re, ab_im = mag * jnp.cos(li * dt), mag * jnp.sin(li * dt)
    den = lr * lr + li * li
    nr, ni = ab_re - 1.0, ab_im
    f_re, f_im = (nr * lr + ni * li) / den, (ni * lr - nr * li) / den
    b_re, b_im = b_re.astype(f32), b_im.astype(f32)
    bb_re = f_re[..., None] * b_re - f_im[..., None] * b_im
    bb_im = f_re[..., None] * b_im + f_im[..., None] * b_re
    tau = jnp.arange(lc + 1, dtype=f32)[:, None, None]
    pw_mag = jnp.exp(tau * (lr * dt)[None])
    pw_re = pw_mag * jnp.cos(tau * (li * dt)[None])
    pw_im = pw_mag * jnp.sin(tau * (li * dt)[None])
    c_re, c_im = c_re.astype(f32), c_im.astype(f32)
    cl_re = c_re[None] * pw_re[:, :, None, :] - c_im[None] * pw_im[:, :, None, :]
    cl_im = c_re[None] * pw_im[:, :, None, :] + c_im[None] * pw_re[:, :, None, :]
    hp = lax.Precision.HIGHEST
    kern = (jnp.einsum('tgop,gpi->tgoi', cl_re[:lc], bb_re, precision=hp)
            - jnp.einsum('tgop,gpi->tgoi', cl_im[:lc], bb_im, precision=hp))
    eye = jnp.eye(tg, dtype=f32)
    toep = jnp.einsum('tTgoi,gh->Ttgiho', kern.reshape(lc, nt, tg, S5_GROUP, S5_GROUP), eye)
    toep = toep.reshape(nt, lc, LANES, LANES)
    rev = pw_re[lc - 1 - np.arange(lc)], pw_im[lc - 1 - np.arange(lc)]
    wr = rev[0][..., None] * bb_re[None] - rev[1][..., None] * bb_im[None]
    wi = rev[0][..., None] * bb_im[None] + rev[1][..., None] * bb_re[None]
    wst = jnp.stack([wr, wi], 0).reshape(2, lc, nt, tg, S5_STATE, S5_GROUP)
    w_in = jnp.einsum('rsTgpi,gh->Tsgirhp', wst, eye).reshape(nt, lc * LANES, 2 * tg * S5_STATE)
    vst = jnp.stack([cl_re[1:], -cl_im[1:]], 0).reshape(2, lc, nt, tg, S5_GROUP, S5_STATE)
    v_out = jnp.einsum('rlTgop,gh->Trgplho', vst, eye).reshape(nt, 2 * tg * S5_STATE, lc * LANES)
    a_re = pw_re[lc].reshape(nt, 1, tg * S5_STATE)
    a_im = pw_im[lc].reshape(nt, 1, tg * S5_STATE)
    a1 = jnp.concatenate([a_re, a_re], -1)
    a2 = jnp.concatenate([-a_im, a_im], -1)
    return toep.astype(BF16), w_in.astype(BF16), v_out.astype(BF16), a1, a2


def _s5_scan(u, bsz, seq, tables, rows):
    toep, w_in, v_out, a1, a2 = tables
    nt = S5_GROUPS // S5_TILE_GROUPS
    lc = S5_CHUNK
    rows = min(rows, seq // lc)
    nblk = seq // (lc * rows)
    tok = rows * lc
    ns = 2 * S5_TILE_GROUPS * S5_STATE
    return pl.pallas_call(
        functools.partial(_s5_scan_kernel, rows=rows),
        grid=(nt, bsz, nblk),
        in_specs=[
            pl.BlockSpec((tok, LANES), lambda t, b, r: (b * nblk + r, t)),
            pl.BlockSpec((1, lc, LANES, LANES), lambda t, b, r: (t, 0, 0, 0)),
            pl.BlockSpec((1, lc * LANES, ns), lambda t, b, r: (t, 0, 0)),
            pl.BlockSpec((1, ns, lc * LANES), lambda t, b, r: (t, 0, 0)),
            pl.BlockSpec((1, 1, ns), lambda t, b, r: (t, 0, 0)),
            pl.BlockSpec((1, 1, ns), lambda t, b, r: (t, 0, 0)),
        ],
        out_specs=pl.BlockSpec((tok, LANES), lambda t, b, r: (b * nblk + r, t)),
        out_shape=jax.ShapeDtypeStruct((bsz * seq, S5_WIDTH), F32),
        scratch_shapes=[pltpu.VMEM((lc * LANES, lc * LANES), BF16), pltpu.VMEM((rows, ns), F32),
                        pltpu.VMEM((rows, ns), F32), pltpu.VMEM((1, ns), F32)],
        compiler_params=_cparams(("arbitrary", "arbitrary", "arbitrary")),
        name="s5_scan",
    )(u, toep, w_in, v_out, a1, a2)


def _s5_post_kernel(y_ref, u_ref, d_ref, w_ref, b_ref, o_ref):
    g = jax.nn.gelu(y_ref[...] + d_ref[...] * u_ref[...], approximate=True)
    gate = jnp.dot(g.astype(BF16), w_ref[...], preferred_element_type=F32) + b_ref[...]
    o_ref[...] = (g * jax.nn.sigmoid(gate)).astype(o_ref.dtype)


def _s5_post(y, u, d_skip, w_glu, b_glu, tm=512):
    t = y.shape[0]
    tm = min(tm, t)
    row = pl.BlockSpec((tm, S5_WIDTH), lambda i: (i, 0))
    vec = pl.BlockSpec((1, S5_WIDTH), lambda i: (0, 0))
    return pl.pallas_call(
        _s5_post_kernel,
        grid=(t // tm,),
        in_specs=[row, row, vec, pl.BlockSpec((S5_WIDTH, S5_WIDTH), lambda i: (0, 0)), vec],
        out_specs=row,
        out_shape=jax.ShapeDtypeStruct((t, S5_WIDTH), BF16),
        compiler_params=_cparams(("parallel",)),
        name="s5_post",
    )(y, u, d_skip.astype(F32)[None, :], w_glu.astype(BF16), b_glu.astype(F32)[None, :])


def _ret_kernel(q_ref, k_ref, v_ref, g_ref, cos_ref, sin_ref, intra_ref, dend_ref, osc_ref, nw_ref,
                o_ref, r_ref, *, chunk_decay):
    L = RET_CHUNK

    @pl.when(pl.program_id(1) == 0)
    def _():
        r_ref[...] = jnp.zeros_like(r_ref)

    cos = cos_ref[...]
    sin = sin_ref[...]
    lane = lax.broadcasted_iota(jnp.int32, (L, LANES), 1)
    rowi = lax.broadcasted_iota(jnp.int32, (LANES, L), 0)
    half = RET_QK_DIM // 2
    outs = []
    for t in range(RET_HEADS // 2):
        sl = slice(t * LANES, (t + 1) * LANES)
        qt = q_ref[:, sl].astype(F32)
        kt = k_ref[:, sl].astype(F32)
        qr = qt * cos + pltpu.roll(qt, 64, 1) * sin
        kr = (kt * cos + pltpu.roll(kt, 64, 1) * sin) * (RET_QK_DIM ** -0.5)
        kr_b = kr.astype(BF16)
        kr_t = kr.T
        for sub in range(2):
            h = 2 * t + sub
            vs = slice(h * RET_V_DIM, (h + 1) * RET_V_DIM)
            qm = jnp.where(((lane // half) % 2) == sub, qr, 0.0).astype(BF16)
            s = lax.dot_general(qm, kr_b, (((1,), (1,)), ((), ())), preferred_element_type=F32)
            p = (s * intra_ref[h]).astype(BF16)
            vh = v_ref[:, vs]
            state = r_ref[h]
            y_in = jnp.dot(p, vh, preferred_element_type=F32)
            y_off = jnp.dot(qm, state.astype(BF16), preferred_element_type=F32) * osc_ref[:, vs]
            ktm = jnp.where(((rowi // half) % 2) == sub, kr_t * dend_ref[h:h + 1, :], 0.0).astype(BF16)
            r_ref[h] = state * chunk_decay[h] + jnp.dot(ktm, vh, preferred_element_type=F32)
            y = y_in + y_off
            mu = jnp.mean(y, -1, keepdims=True)
            yc = y - mu
            var = jnp.mean(yc * yc, -1, keepdims=True)
            yn = yc * lax.rsqrt(var + LN_EPS) * nw_ref[:, vs]
            outs.append(_silu(g_ref[:, vs].astype(F32)) * yn)
    o_ref[...] = jnp.concatenate(outs, axis=1).astype(o_ref.dtype)


def _ret_mixer(proj, bsz, seq, positions, norm_w):
    L = RET_CHUNK
    nc = seq // L
    half = RET_QK_DIM // 2
    inv_freq = 1.0 / (ROPE_BASE ** (jnp.arange(half, dtype=F32) / half))
    ang = positions.astype(F32).reshape(bsz * seq, 1) * inv_freq[None, :]
    cos32, sin32 = jnp.cos(ang), jnp.sin(ang)
    cos_t = jnp.concatenate([cos32] * 4, axis=1)
    sin_t = jnp.concatenate([-sin32, -sin32, sin32, sin32], axis=1)
    log_gamma = np.log1p(-np.exp2(-5.0 - np.arange(RET_HEADS, dtype=np.float64)))
    pos = np.arange(L, dtype=np.float64)
    rel = pos[:, None] - pos[None, :]
    intra = np.where(rel[None] >= 0, np.exp(rel[None] * log_gamma[:, None, None]), 0.0).astype(np.float32)
    dend = np.exp((L - 1 - pos)[None, :] * log_gamma[:, None]).astype(np.float32)
    osc = np.repeat(np.exp((pos + 1)[:, None] * log_gamma[None, :]), RET_V_DIM, axis=1).astype(np.float32)
    chunk_decay = tuple(float(v) for v in np.exp(L * log_gamma))
    rows = lambda b, c: b * nc + c
    full = lambda shape: pl.BlockSpec(shape, lambda b, c: (0,) * len(shape))
    return pl.pallas_call(
        functools.partial(_ret_kernel, chunk_decay=chunk_decay),
        grid=(bsz, nc),
        in_specs=[
            pl.BlockSpec((L, 512), lambda b, c: (rows(b, c), _CB_RQ)),
            pl.BlockSpec((L, 512), lambda b, c: (rows(b, c), _CB_RK)),
            pl.BlockSpec((L, 1024), lambda b, c: (rows(b, c), _CB_RV)),
            pl.BlockSpec((L, 1024), lambda b, c: (rows(b, c), _CB_RG)),
            pl.BlockSpec((L, LANES), lambda b, c: (rows(b, c), 0)),
            pl.BlockSpec((L, LANES), lambda b, c: (rows(b, c), 0)),
            full((RET_HEADS, L, L)), full((RET_HEADS, L)), full((L, RET_WIDTH)), full((1, RET_WIDTH)),
        ],
        out_specs=pl.BlockSpec((L, RET_WIDTH), lambda b, c: (rows(b, c), 0)),
        out_shape=jax.ShapeDtypeStruct((bsz * seq, RET_WIDTH), BF16),
        scratch_shapes=[pltpu.VMEM((RET_HEADS, LANES, RET_V_DIM), F32)],
        compiler_params=_cparams(("parallel", "arbitrary")),
        name="retention_mixer",
    )(proj, proj, proj, proj, cos_t, sin_t, jnp.asarray(intra), jnp.asarray(dend), jnp.asarray(osc),
      norm_w.astype(F32)[None, :])


def _gla_kernel(q_ref, k_ref, v_ref, r_ref, sm_ref, wa_ref, ba_ref, tblk_ref, nw_ref, o_ref, s_ref):
    LB = GLA_BLOCK
    LC = GLA_CHUNK

    @pl.when(pl.program_id(1) == 0)
    def _():
        s_ref[...] = jnp.zeros_like(s_ref)

    logit = jnp.dot(sm_ref[...], wa_ref[...], preferred_element_type=F32, precision=HIGHEST) + ba_ref[...]
    la = (jnp.minimum(logit, 0.0) - jnp.log1p(jnp.exp(-jnp.abs(logit)))) * (1.0 / GLA_TAU)
    b = jnp.dot(tblk_ref[...], la, preferred_element_type=F32, precision=HIGHEST)
    rowf = lax.broadcasted_iota(jnp.int32, (LB, GLA_QK), 0)
    b_end = jnp.where(rowf < LC, b[LC - 1:LC], b[LB - 1:LB])
    q = q_ref[...].astype(F32) * (GLA_QK_DIM ** -0.5)
    k = k_ref[...].astype(F32)
    q_dec = q * jnp.exp(b)
    k_inv = (k * jnp.exp(-b)).astype(BF16)
    k_dec = k * jnp.exp(b_end - b)
    row = lax.broadcasted_iota(jnp.int32, (LB, LB), 0)
    col = lax.broadcasted_iota(jnp.int32, (LB, LB), 1)
    blk_causal = (row >= col) & ((row // LC) == (col // LC))
    first_rows = row < LC
    first_vrows = lax.broadcasted_iota(jnp.int32, (LB, GLA_V_DIM), 0) < LC
    outs = []
    for h in range(GLA_HEADS):
        sl = slice(h * GLA_QK_DIM, (h + 1) * GLA_QK_DIM)
        vs = slice(h * GLA_V_DIM, (h + 1) * GLA_V_DIM)
        qd = q_dec[:, sl]
        qd_b = qd.astype(BF16)
        att = lax.dot_general(qd_b, k_inv[:, sl], (((1,), (1,)), ((), ())), preferred_element_type=F32)
        att = jnp.where(blk_causal, att, 0.0).astype(BF16)
        vh = v_ref[:, vs]
        y = jnp.dot(att, vh, preferred_element_type=F32)
        s0 = s_ref[h]
        kd_t = k_dec[:, sl].T.astype(BF16)
        kv0 = jnp.dot(kd_t, jnp.where(first_vrows, vh, jnp.zeros_like(vh)), preferred_element_type=F32)
        kv1 = jnp.dot(kd_t, jnp.where(first_vrows, jnp.zeros_like(vh), vh), preferred_element_type=F32)
        b_t = b[:, sl].T
        s1 = s0 * jnp.exp(b_t[:, LC - 1:LC]) + kv0
        s_ref[h] = s1 * jnp.exp(b_t[:, LB - 1:LB]) + kv1
        y = y + jnp.dot(jnp.where(first_rows, qd, 0.0).astype(BF16), s0.astype(BF16), preferred_element_type=F32)
        y = y + jnp.dot(jnp.where(first_rows, 0.0, qd).astype(BF16), s1.astype(BF16), preferred_element_type=F32)
        ms = jnp.mean(y * y, -1, keepdims=True)
        yn = y * lax.rsqrt(ms + RMS_EPS) * nw_ref[:, vs]
        outs.append(yn * _silu(r_ref[:, vs].astype(F32)))
    o_ref[...] = jnp.concatenate(outs, axis=1).astype(o_ref.dtype)


def _gla_mixer(proj, small, bsz, seq, w_alpha, b_alpha, norm_w):
    LB = GLA_BLOCK
    nb = seq // LB
    wa = jnp.zeros((LANES, GLA_QK), F32).at[SSD_HEADS:SSD_HEADS + GLA_GATE_RANK].set(w_alpha.astype(F32))
    idx = np.arange(LB)
    tblk = ((idx[:, None] >= idx[None, :]) & ((idx[:, None] // GLA_CHUNK) == (idx[None, :] // GLA_CHUNK)))
    rows = lambda b, c: b * nb + c
    full = lambda shape: pl.BlockSpec(shape, lambda b, c: (0,) * len(shape))
    return pl.pallas_call(
        _gla_kernel,
        grid=(bsz, nb),
        in_specs=[
            pl.BlockSpec((LB, 512), lambda b, c: (rows(b, c), _CB_GQ)),
            pl.BlockSpec((LB, 512), lambda b, c: (rows(b, c), _CB_GK)),
            pl.BlockSpec((LB, 1024), lambda b, c: (rows(b, c), _CB_GV)),
            pl.BlockSpec((LB, 1024), lambda b, c: (rows(b, c), _CB_GR)),
            pl.BlockSpec((LB, LANES), lambda b, c: (rows(b, c), 0)),
            full((LANES, GLA_QK)), full((1, GLA_QK)), full((LB, LB)), full((1, GLA_WIDTH)),
        ],
        out_specs=pl.BlockSpec((LB, GLA_WIDTH), lambda b, c: (rows(b, c), 0)),
        out_shape=jax.ShapeDtypeStruct((bsz * seq, GLA_WIDTH), BF16),
        scratch_shapes=[pltpu.VMEM((GLA_HEADS, GLA_QK_DIM, GLA_V_DIM), F32)],
        compiler_params=_cparams(("parallel", "arbitrary")),
        name="gla_mixer",
    )(proj, proj, proj, proj, small, wa, b_alpha.astype(F32)[None, :], jnp.asarray(tblk.astype(np.float32)),
      norm_w.astype(F32)[None, :])


def _merge_kernel(ya_ref, yb_ref, yc_ref, yd_ref, g0_ref, g1_ref, g2_ref, g3_ref, wb_ref, o_ref):
    acc = None
    for n, (y_ref, g_ref) in enumerate(((ya_ref, g0_ref), (yb_ref, g1_ref), (yc_ref, g2_ref), (yd_ref, g3_ref))):
        br = jnp.dot(y_ref[...], wb_ref[n], preferred_element_type=F32)
        term = jax.nn.sigmoid(g_ref[...].astype(F32)) * br
        acc = term if acc is None else acc + term
    o_ref[...] = acc.astype(o_ref.dtype)


def _merge(ys, proj, w_branch, tm=512, tn=512):
    t = ys[0].shape[0]
    tm = min(tm, t)
    nj = D_MODEL // tn
    yspec = pl.BlockSpec((tm, BRANCH_WIDTH), lambda i, j: (i, 0))
    gspec = lambda n: pl.BlockSpec((tm, tn), lambda i, j: (i, n * nj + j))
    return pl.pallas_call(
        _merge_kernel,
        grid=(t // tm, nj),
        in_specs=[yspec] * 4 + [gspec(n) for n in range(N_BRANCH)]
                 + [pl.BlockSpec((N_BRANCH, BRANCH_WIDTH, tn), lambda i, j: (0, 0, j))],
        out_specs=pl.BlockSpec((tm, tn), lambda i, j: (i, j)),
        out_shape=jax.ShapeDtypeStruct((t, D_MODEL), BF16),
        compiler_params=_cparams(("parallel", "parallel")),
        name="branch_merge",
    )(*ys, proj, proj, proj, proj, w_branch.astype(BF16))


def _proj_ln_kernel(m_ref, w_ref, h_ref, lw_ref, lb_ref, o_ref, ob_ref):
    mix = jnp.dot(m_ref[...], w_ref[...], preferred_element_type=F32)
    out = _layer_norm(DEEPNORM_ALPHA * h_ref[...] + mix, lw_ref[...], lb_ref[...])
    o_ref[...] = out
    ob_ref[...] = out.astype(BF16)


def _proj_ln(merged, w_out, h, ln_w, ln_b, tm=512):
    t = h.shape[0]
    tm = min(tm, t)
    row = pl.BlockSpec((tm, D_MODEL), lambda i: (i, 0))
    vec = pl.BlockSpec((1, D_MODEL), lambda i: (0, 0))
    return pl.pallas_call(
        _proj_ln_kernel,
        grid=(t // tm,),
        in_specs=[row, pl.BlockSpec((D_MODEL, D_MODEL), lambda i: (0, 0)), row, vec, vec],
        out_specs=[row, row],
        out_shape=[jax.ShapeDtypeStruct((t, D_MODEL), F32), jax.ShapeDtypeStruct((t, D_MODEL), BF16)],
        compiler_params=_cparams(("parallel",)),
        name="out_proj_ln",
    )(merged, w_out.astype(BF16), h, ln_w.astype(F32)[None, :], ln_b.astype(F32)[None, :])


def _router_kernel(h_ref, w_ref, b_ref, tri_ref, idx_ref, gate_ref, rank_ref, cnt_ref, run_ref):
    @pl.when(pl.program_id(0) == 0)
    def _():
        run_ref[...] = jnp.zeros_like(run_ref)

    logits = lax.dot_general(w_ref[...], h_ref[...], (((1,), (1,)), ((), ())),
                             preferred_element_type=F32, precision=HIGHEST) + b_ref[...]
    eid = lax.broadcasted_iota(jnp.int32, logits.shape, 0)
    vals = logits
    run = run_ref[...]
    top_v, top_i, ranks = [], [], []
    for _ in range(TOP_K):
        m = jnp.max(vals, axis=0, keepdims=True)
        sel = jnp.min(jnp.where(vals == m, eid, N_EXPERTS), axis=0, keepdims=True)
        hit = eid == sel
        top_v.append(m)
        top_i.append(sel)
        vals = jnp.where(hit, -jnp.inf, vals)
        onehot = jnp.where(hit, 1.0, 0.0)
        before = jnp.dot(onehot.astype(BF16), tri_ref[...], preferred_element_type=F32)
        ranks.append(jnp.sum(onehot * (run + before), axis=0, keepdims=True))
        run = run + jnp.sum(onehot, axis=1, keepdims=True)
    run_ref[...] = run
    ex = [jnp.exp(v - top_v[0]) for v in top_v]
    den = ex[0] + ex[1] + ex[2] + ex[3]
    zi = jnp.zeros_like(top_i[0])
    zf = jnp.zeros_like(den)
    idx_ref[...] = jnp.concatenate(top_i + [zi] * (8 - TOP_K), axis=0)
    gate_ref[...] = jnp.concatenate([e / den for e in ex] + [zf] * (8 - TOP_K), axis=0)
    rank_ref[...] = jnp.concatenate(ranks + [zf] * (8 - TOP_K), axis=0).astype(jnp.int32)
    cnt_ref[...] = jnp.broadcast_to(run, cnt_ref.shape).astype(jnp.int32)


def _router(h, router_w, router_b, tm=512):
    t = h.shape[0]
    tm = min(tm, t)
    tri = np.triu(np.ones((tm, tm), np.float32), 1)
    tok = pl.BlockSpec((8, tm), lambda i: (0, i))
    return pl.pallas_call(
        _router_kernel,
        grid=(t // tm,),
        in_specs=[pl.BlockSpec((tm, D_MODEL), lambda i: (i, 0)),
                  pl.BlockSpec((N_EXPERTS, D_MODEL), lambda i: (0, 0)),
                  pl.BlockSpec((N_EXPERTS, 1), lambda i: (0, 0)),
                  pl.BlockSpec((tm, tm), lambda i: (0, 0))],
        out_specs=[tok, tok, tok, pl.BlockSpec((N_EXPERTS, LANES), lambda i: (0, 0))],
        out_shape=[jax.ShapeDtypeStruct((8, t), jnp.int32), jax.ShapeDtypeStruct((8, t), F32),
                   jax.ShapeDtypeStruct((8, t), jnp.int32), jax.ShapeDtypeStruct((N_EXPERTS, LANES), jnp.int32)],
        scratch_shapes=[pltpu.VMEM((N_EXPERTS, 1), F32)],
        compiler_params=_cparams(("arbitrary",)),
        name="router_topk",
    )(h, router_w.astype(F32).T, router_b.astype(F32)[:, None], jnp.asarray(tri, BF16))


def _dispatch_kernel(idx_hbm, h_ref, xb_ref, idx_ref, zero_ref, isem, rsem, *, tm, n_pad):
    i = pl.program_id(0)
    cp = pltpu.make_async_copy(idx_hbm.at[i], idx_ref, isem)
    cp.start()
    zero_ref[...] = jnp.zeros_like(zero_ref)
    cp.wait()

    def issue(r, carry):
        for k in range(TOP_K):
            pltpu.make_async_copy(h_ref.at[pl.ds(r, 1)], xb_ref.at[pl.ds(idx_ref[k * tm + r], 1)], rsem).start()
        return carry

    lax.fori_loop(0, tm, issue, 0, unroll=8)

    def issue_pad(r, carry):
        pltpu.make_async_copy(zero_ref.at[pl.ds(0, 1)], xb_ref.at[pl.ds(idx_ref[TOP_K * tm + r], 1)], rsem).start()
        return carry

    lax.fori_loop(0, n_pad, issue_pad, 0, unroll=8)
    for _ in range(TOP_K):
        pltpu.make_async_copy(h_ref, xb_ref.at[pl.ds(0, tm)], rsem).wait()
    pltpu.make_async_copy(xb_ref.at[pl.ds(0, n_pad)], xb_ref.at[pl.ds(0, n_pad)], rsem).wait()


def _dispatch(h, dest, pad_rows, n_rows, tm=512):
    t = h.shape[0]
    tm = min(tm, t)
    nt = t // tm
    n_pad = pad_rows.shape[0] // nt
    idx = jnp.concatenate([dest.reshape(TOP_K, nt, tm).transpose(1, 0, 2).reshape(nt, TOP_K * tm),
                           pad_rows.reshape(nt, n_pad)], axis=1)
    return pl.pallas_call(
        functools.partial(_dispatch_kernel, tm=tm, n_pad=n_pad),
        grid=(nt,),
        in_specs=[pl.BlockSpec(memory_space=pl.ANY), pl.BlockSpec((tm, D_MODEL), lambda i: (i, 0))],
        out_specs=pl.BlockSpec(memory_space=pl.ANY),
        out_shape=jax.ShapeDtypeStruct((n_rows, D_MODEL), F32),
        scratch_shapes=[pltpu.SMEM((TOP_K * tm + n_pad,), jnp.int32), pltpu.VMEM((8, D_MODEL), F32),
                        pltpu.SemaphoreType.DMA(()), pltpu.SemaphoreType.DMA(())],
        compiler_params=_cparams(("arbitrary",)),
        name="moe_dispatch",
    )(idx, h)


def _expert_kernel(be_ref, nb_ref, x_ref, wgu_ref, bgu_ref, wd_ref, bd_ref, o_ref):
    i = pl.program_id(0)
    n_used = nb_ref[0]

    @pl.when(i < n_used)
    def _():
        xb = x_ref[...].astype(BF16)
        hgu = jnp.dot(xb, wgu_ref[0], preferred_element_type=F32) + bgu_ref[0]
        g = jnp.minimum(hgu[:, :D_EXPERT], SWIGLU_LIMIT)
        u = jnp.clip(hgu[:, D_EXPERT:], -SWIGLU_LIMIT, SWIGLU_LIMIT)
        act = (u + 1.0) * (g * jax.nn.sigmoid(SWIGLU_ALPHA * g))
        o_ref[...] = jnp.dot(act.astype(BF16), wd_ref[0], preferred_element_type=F32) + bd_ref[0]

    @pl.when(i >= n_used)
    def _():
        o_ref[...] = jnp.zeros_like(o_ref)


def _experts(xb, block_expert, n_used, w_gate_up, b_gate_up, w_down, b_down):
    blk = MOE_BLOCK
    n_blocks = xb.shape[0] // blk
    grid_spec = pltpu.PrefetchScalarGridSpec(
        num_scalar_prefetch=2,
        grid=(n_blocks,),
        in_specs=[
            pl.BlockSpec((blk, D_MODEL), lambda i, be, nb: (i, 0)),
            pl.BlockSpec((1, D_MODEL, 2 * D_EXPERT), lambda i, be, nb: (be[i], 0, 0)),
            pl.BlockSpec((1, 1, 2 * D_EXPERT), lambda i, be, nb: (be[i], 0, 0)),
            pl.BlockSpec((1, D_EXPERT, D_MODEL), lambda i, be, nb: (be[i], 0, 0)),
            pl.BlockSpec((1, 1, D_MODEL), lambda i, be, nb: (be[i], 0, 0)),
        ],
        out_specs=pl.BlockSpec((blk, D_MODEL), lambda i, be, nb: (i, 0)),
    )
    return pl.pallas_call(
        _expert_kernel,
        grid_spec=grid_spec,
        out_shape=jax.ShapeDtypeStruct((n_blocks * blk, D_MODEL), F32),
        compiler_params=_cparams(("arbitrary",)),
        name="expert_ffn",
    )(block_expert, n_used, xb, w_gate_up.astype(BF16), b_gate_up.astype(F32)[:, None, :],
      w_down.astype(BF16), b_down.astype(F32)[:, None, :])


def _combine_kernel(dest_ref, yb_ref, gate_ref, h_ref, lw_ref, lb_ref, o_ref, ob_ref, idx_ref, buf_ref, isem, rsem,
                    *, tm):
    i = pl.program_id(0)
    n = pl.num_programs(0)

    def idx_copy(tile, slot):
        return pltpu.make_async_copy(dest_ref.at[tile], idx_ref.at[slot], isem.at[slot])

    def issue_rows(slot):
        def issue(r, carry):
            for k in range(TOP_K):
                pltpu.make_async_copy(yb_ref.at[pl.ds(idx_ref[slot, k * tm + r], 1)],
                                      buf_ref.at[slot, k, pl.ds(r, 1)], rsem.at[slot]).start()
            return carry

        lax.fori_loop(0, tm, issue, 0, unroll=8)

    slot = i % 2

    @pl.when(i == 0)
    def _():
        idx_copy(0, 0).start()
        idx_copy(0, 0).wait()
        issue_rows(0)

        @pl.when(n > 1)
        def _():
            idx_copy(1, 1).start()

    @pl.when(i + 1 < n)
    def _():
        idx_copy(i + 1, 1 - slot).wait()
        issue_rows(1 - slot)

    @pl.when(i + 2 < n)
    def _():
        idx_copy(i + 2, slot).start()

    for k in range(TOP_K):
        pltpu.make_async_copy(yb_ref.at[pl.ds(0, tm)], buf_ref.at[slot, k], rsem.at[slot]).wait()

    gates = gate_ref[...]
    ffn = gates[:, 0:1] * buf_ref[slot, 0]
    for k in range(1, TOP_K):
        ffn = ffn + gates[:, k:k + 1] * buf_ref[slot, k]
    out = _layer_norm(DEEPNORM_ALPHA * h_ref[...] + ffn, lw_ref[...], lb_ref[...])
    o_ref[...] = out
    ob_ref[...] = out.astype(BF16)


def _combine(dest, yb, gates, h, ln_w, ln_b, tm=256):
    t = h.shape[0]
    tm = min(tm, t)
    row = pl.BlockSpec((tm, D_MODEL), lambda i: (i, 0))
    vec = pl.BlockSpec((1, D_MODEL), lambda i: (0, 0))
    dest_tiles = dest.reshape(TOP_K, t // tm, tm).transpose(1, 0, 2).reshape(t // tm, TOP_K * tm)
    return pl.pallas_call(
        functools.partial(_combine_kernel, tm=tm),
        grid=(t // tm,),
        in_specs=[pl.BlockSpec(memory_space=pl.ANY), pl.BlockSpec(memory_space=pl.ANY),
                  pl.BlockSpec((tm, 8), lambda i: (i, 0)), row, vec, vec],
        out_specs=[row, row],
        out_shape=[jax.ShapeDtypeStruct((t, D_MODEL), F32), jax.ShapeDtypeStruct((t, D_MODEL), BF16)],
        scratch_shapes=[pltpu.SMEM((2, TOP_K * tm), jnp.int32), pltpu.VMEM((2, TOP_K, tm, D_MODEL), F32),
                        pltpu.SemaphoreType.DMA((2,)), pltpu.SemaphoreType.DMA((2,))],
        compiler_params=_cparams(("arbitrary",)),
        name="moe_combine_ln",
    )(dest_tiles, yb, gates, h, ln_w.astype(F32)[None, :], ln_b.astype(F32)[None, :])


def _moe(h, router_w, router_b, w_gate_up, b_gate_up, w_down, b_down, ln_w, ln_b):
    t = h.shape[0]
    blk = MOE_BLOCK
    top_idx, top_gate, rank, counts = _router(h, router_w, router_b)
    top_idx, rank, counts = top_idx[:TOP_K], rank[:TOP_K], counts[:, 0]
    n_assign = TOP_K * t
    n_blocks = n_assign // blk + N_EXPERTS
    n_rows = n_blocks * blk
    padded = (counts + blk - 1) // blk * blk
    pad_end = jnp.cumsum(padded)
    pad_start = pad_end - padded
    experts = jnp.arange(N_EXPERTS, dtype=jnp.int32)
    dest = rank + jnp.sum(jnp.where(top_idx[..., None] == experts, pad_start, 0), axis=-1)
    blocks = jnp.arange(n_blocks, dtype=jnp.int32)
    block_expert = jnp.minimum(jnp.searchsorted(pad_end // blk, blocks, side='right'), N_EXPERTS - 1).astype(jnp.int32)
    n_used = (pad_end[-1] // blk).astype(jnp.int32).reshape(1)
    gap_start = jnp.concatenate([pad_start + counts, pad_end[-1:]])
    gap_len = jnp.concatenate([padded - counts, n_rows - pad_end[-1:]])
    gap_end = jnp.cumsum(gap_len)
    j = jnp.arange(n_rows - n_assign, dtype=jnp.int32)
    seg = jnp.searchsorted(gap_end, j, side='right')
    pad_rows = (gap_start[seg] + j - (gap_end[seg] - gap_len[seg])).astype(jnp.int32)
    xb = _dispatch(h, dest.astype(jnp.int32), pad_rows, n_rows)
    yb = _experts(xb, block_expert, n_used, w_gate_up, b_gate_up, w_down, b_down)
    return _combine(dest.astype(jnp.int32), yb, top_gate.T, h, ln_w, ln_b)


def _ple_kernel(hb_ref, h_ref, p_ref, wg_ref, wp_ref, lw_ref, lb_ref, o_ref, ob_ref):
    gate = jax.nn.sigmoid(jnp.dot(hb_ref[...], wg_ref[...], preferred_element_type=F32))
    emb = jnp.dot(p_ref[...].astype(BF16), wp_ref[...], preferred_element_type=F32)
    out = _layer_norm(DEEPNORM_ALPHA * h_ref[...] + gate * emb, lw_ref[...], lb_ref[...])
    o_ref[...] = out
    ob_ref[...] = out.astype(BF16)


def _ple(hb, h, p_i, w_gate, w_proj, ln_w, ln_b, tm=512):
    t = h.shape[0]
    tm = min(tm, t)
    row = pl.BlockSpec((tm, D_MODEL), lambda i: (i, 0))
    vec = pl.BlockSpec((1, D_MODEL), lambda i: (0, 0))
    return pl.pallas_call(
        _ple_kernel,
        grid=(t // tm,),
        in_specs=[row, row, pl.BlockSpec((tm, PLE_DIM), lambda i: (i, 0)),
                  pl.BlockSpec((D_MODEL, D_MODEL), lambda i: (0, 0)),
                  pl.BlockSpec((PLE_DIM, D_MODEL), lambda i: (0, 0)), vec, vec],
        out_specs=[row, row],
        out_shape=[jax.ShapeDtypeStruct((t, D_MODEL), F32), jax.ShapeDtypeStruct((t, D_MODEL), BF16)],
        compiler_params=_cparams(("parallel",)),
        name="ple_ln",
    )(hb, h, p_i, w_gate.astype(BF16), w_proj.astype(BF16), ln_w.astype(F32)[None, :], ln_b.astype(F32)[None, :])


def _token_mixer(h, hb, bsz, seq, positions, w_in, ssd_conv_w, ssd_conv_b, ssd_dt_bias, ssd_a_log, ssd_d,
                 ssd_norm_w, s5_lambda_re, s5_lambda_im, s5_log_dt, s5_b_re, s5_b_im, s5_c_re, s5_c_im, s5_d,
                 s5_w_glu, s5_b_glu, ret_norm_w, gla_w_alpha, gla_b_alpha, gla_norm_w, w_branch, s5_rows):
    w_main = _main_weight(w_in)
    w_u = w_in[:, _OFF_U:_OFF_U + S5_WIDTH].astype(BF16)
    w_small = jnp.concatenate([w_in[:, _OFF_DT:_OFF_DT + SSD_HEADS], w_in[:, _OFF_GC:_OFF_GC + GLA_GATE_RANK],
                               jnp.zeros((D_MODEL, LANES - SSD_HEADS - GLA_GATE_RANK), w_in.dtype)], axis=1)
    proj = _matmul(hb, w_main, BF16, 1024, 1536, name="in_proj_main")
    u = _matmul(hb, w_u, F32, 1024, 1024, name="in_proj_s5")
    small = _matmul(h, w_small.astype(F32), F32, 512, LANES, precision=HIGHEST, name="in_proj_small")
    y_a = _ssd_mixer(proj, small, bsz, seq, ssd_conv_w, ssd_conv_b, ssd_dt_bias, ssd_a_log, ssd_d, ssd_norm_w)
    tables = _s5_tables(s5_lambda_re, s5_lambda_im, s5_log_dt, s5_b_re, s5_b_im, s5_c_re, s5_c_im)
    y_b = _s5_post(_s5_scan(u, bsz, seq, tables, s5_rows), u, s5_d, s5_w_glu, s5_b_glu)
    y_c = _ret_mixer(proj, bsz, seq, positions, ret_norm_w)
    y_d = _gla_mixer(proj, small, bsz, seq, gla_w_alpha, gla_b_alpha, gla_norm_w)
    return _merge((y_a, y_b, y_c, y_d), proj, w_branch)


def kernel(x, p, positions, w_in, ssd_conv_w, ssd_conv_b, ssd_dt_bias, ssd_a_log, ssd_d, ssd_norm_w, s5_lambda_re, s5_lambda_im, s5_log_dt, s5_b_re, s5_b_im, s5_c_re, s5_c_im, s5_d, s5_w_glu, s5_b_glu, ret_norm_w, gla_w_alpha, gla_b_alpha, gla_norm_w, w_branch, w_out, ln1_w, ln1_b, router_w, router_b, moe_w_gate_up, moe_b_gate_up, moe_w_down, moe_b_down, ln2_w, ln2_b, ple_w_gate, ple_w_proj, ln3_w, ln3_b):
    s5_rows = 256
    bsz, seq, d = x.shape
    t = bsz * seq
    h = x.reshape(t, d).astype(F32)
    hb = h.astype(BF16)
    for i in range(DEPTH):
        merged = _token_mixer(h, hb, bsz, seq, positions, w_in[i], ssd_conv_w[i], ssd_conv_b[i], ssd_dt_bias[i],
                              ssd_a_log[i], ssd_d[i], ssd_norm_w[i], s5_lambda_re[i], s5_lambda_im[i],
                              s5_log_dt[i], s5_b_re[i], s5_b_im[i], s5_c_re[i], s5_c_im[i], s5_d[i],
                              s5_w_glu[i], s5_b_glu[i], ret_norm_w[i], gla_w_alpha[i], gla_b_alpha[i],
                              gla_norm_w[i], w_branch[i], s5_rows)
        h, hb = _proj_ln(merged, w_out[i], h, ln1_w[i], ln1_b[i])
        h, hb = _moe(h, router_w[i], router_b[i], moe_w_gate_up[i], moe_b_gate_up[i], moe_w_down[i],
                     moe_b_down[i], ln2_w[i], ln2_b[i])
        h, hb = _ple(hb, h, p[i].reshape(t, PLE_DIM), ple_w_gate[i], ple_w_proj[i], ln3_w[i], ln3_b[i])
    return h.reshape(bsz, seq, d).astype(x.dtype)
```

```python
import functools
import math

import numpy as np
import jax
import jax.numpy as jnp
from jax import lax
from jax.experimental import pallas as pl
from jax.experimental.pallas import tpu as pltpu

F32 = jnp.float32
BF16 = jnp.bfloat16
HIGHEST = lax.Precision.HIGHEST

D_MODEL = 2048
DEPTH = 2
PLE_DIM = 256
N_BRANCH = 4
BRANCH_WIDTH = 1024

SSD_HEADS = 16
SSD_HEAD_DIM = 64
SSD_INNER = SSD_HEADS * SSD_HEAD_DIM
SSD_GROUPS = 2
SSD_STATE = 128
SSD_CONV = 4
SSD_CHUNK = 128

S5_WIDTH = 1024
S5_GROUP = 16
S5_GROUPS = S5_WIDTH // S5_GROUP
S5_STATE = 64
S5_CHUNK = 16
S5_TILE_GROUPS = 8

RET_HEADS = 8
RET_QK_DIM = 64
RET_V_DIM = 128
RET_QK = RET_HEADS * RET_QK_DIM
RET_WIDTH = RET_HEADS * RET_V_DIM
RET_CHUNK = 128
ROPE_BASE = 10000.0

GLA_HEADS = 4
GLA_QK_DIM = 128
GLA_V_DIM = 256
GLA_QK = GLA_HEADS * GLA_QK_DIM
GLA_WIDTH = GLA_HEADS * GLA_V_DIM
GLA_GATE_RANK = 16
GLA_TAU = 16.0
GLA_CHUNK = 64
GLA_BLOCK = 128

N_EXPERTS = 32
TOP_K = 4
D_EXPERT = 1024
SWIGLU_ALPHA = 1.702
SWIGLU_LIMIT = 7.0
MOE_BLOCK = 512

DEEPNORM_ALPHA = (2 * DEPTH) ** 0.25
LN_EPS = 1e-5
RMS_EPS = 1e-6

LANES = 128
VMEM_LIMIT = 56 * 1024 * 1024

_OFF_Z = 0
_OFF_XS = 1024
_OFF_BC = 2048
_OFF_DT = 2560
_OFF_U = 2576
_OFF_RQ = 3600
_OFF_RK = 4112
_OFF_RV = 4624
_OFF_RG = 5648
_OFF_GQ = 6672
_OFF_GK = 7184
_OFF_GV = 7696
_OFF_GR = 8720
_OFF_GC = 9744
_OFF_GATE = 9760

_CB_Z, _CB_XS, _CB_RV, _CB_RG, _CB_GV, _CB_GR = 8, 9, 10, 11, 12, 13
_CB_BC, _CB_RQ, _CB_RK, _CB_GQ, _CB_GK = 28, 29, 30, 31, 32
N_MAIN = 16896


def _main_weight(w_in):
    def cols(off, n):
        return w_in[:, off:off + n].astype(BF16)

    def rotary(off):
        half = RET_QK_DIM // 2
        w = cols(off, RET_QK).reshape(D_MODEL, RET_HEADS // 2, 2, 2, half)
        return w.transpose(0, 1, 3, 2, 4).reshape(D_MODEL, RET_QK)

    return jnp.concatenate([
        cols(_OFF_GATE, N_BRANCH * D_MODEL), cols(_OFF_Z, 1024), cols(_OFF_XS, 1024), cols(_OFF_RV, 1024),
        cols(_OFF_RG, 1024), cols(_OFF_GV, 1024), cols(_OFF_GR, 1024), cols(_OFF_BC, 512),
        rotary(_OFF_RQ), rotary(_OFF_RK), cols(_OFF_GQ, 512), cols(_OFF_GK, 512)], axis=1)


def _cparams(sem):
    return pltpu.CompilerParams(dimension_semantics=sem, vmem_limit_bytes=VMEM_LIMIT)


def _silu(x):
    return x * jax.nn.sigmoid(x)


def _softplus(x):
    return jnp.maximum(x, 0.0) + jnp.log1p(jnp.exp(-jnp.abs(x)))


def _layer_norm(x, w, b):
    mu = jnp.mean(x, -1, keepdims=True)
    xc = x - mu
    var = jnp.mean(xc * xc, -1, keepdims=True)
    return xc * lax.rsqrt(var + LN_EPS) * w + b


def _mm_kernel(a_ref, b_ref, o_ref):
    o_ref[...] = jnp.dot(a_ref[...], b_ref[...], preferred_element_type=F32).astype(o_ref.dtype)


def _matmul(a, b, out_dtype, tm, tn, name="matmul"):
    m, k = a.shape
    n = b.shape[1]
    tm, tn = min(tm, m), min(tn, n)
    return pl.pallas_call(
        _mm_kernel,
        grid=(m // tm, n // tn),
        in_specs=[pl.BlockSpec((tm, k), lambda i, j: (i, 0)),
                  pl.BlockSpec((k, tn), lambda i, j: (0, j))],
        out_specs=pl.BlockSpec((tm, tn), lambda i, j: (i, j)),
        out_shape=jax.ShapeDtypeStruct((m, n), out_dtype),
        compiler_params=_cparams(("parallel", "parallel")),
        name=name,
    )(a, b)


def _split_bf16(x):
    hi = x.astype(BF16)
    return hi, (x - hi.astype(F32)).astype(BF16)


def _dot3(a, b, dims):
    a_hi, a_lo = _split_bf16(a)
    b_hi, b_lo = _split_bf16(b)
    dg = functools.partial(lax.dot_general, dimension_numbers=(dims, ((), ())), preferred_element_type=F32)
    return dg(a_hi, b_hi) + (dg(a_hi, b_lo) + dg(a_lo, b_hi))


def _mm3_kernel(a_ref, b_ref, o_ref):
    o_ref[...] = _dot3(a_ref[...], b_ref[...], ((1,), (0,)))


def _matmul_f32(a, b, tm, name):
    m, k = a.shape
    n = b.shape[1]
    tm = min(tm, m)
    return pl.pallas_call(
        _mm3_kernel,
        grid=(m // tm,),
        in_specs=[pl.BlockSpec((tm, k), lambda i: (i, 0)), pl.BlockSpec((k, n), lambda i: (0, 0))],
        out_specs=pl.BlockSpec((tm, n), lambda i: (i, 0)),
        out_shape=jax.ShapeDtypeStruct((m, n), F32),
        compiler_params=_cparams(("parallel",)),
        name=name,
    )(a, b)


def _causal_conv(x, tail, w, b):
    n_tap = SSD_CONV
    acc = x * w[n_tap - 1:n_tap] + b
    x8 = x[0:8]
    acc8 = x8 * w[n_tap - 1:n_tap] + b
    row8 = lax.broadcasted_iota(jnp.int32, x8.shape, 0)
    for s in range(1, n_tap):
        wk = w[n_tap - 1 - s:n_tap - s]
        acc = acc + pltpu.roll(x, s, 0) * wk
        v8 = jnp.where(row8 < s, pltpu.roll(tail, s, 0), pltpu.roll(x8, s, 0))
        acc8 = acc8 + v8 * wk
    return jnp.concatenate([acc8, acc[8:]], axis=0)


def _ssd_kernel(z_ref, xs_ref, bc_ref, sm_ref, cwx_ref, cwbc_ref, cbx_ref, cbbc_ref, dtb_ref,
                nega_ref, dsk_ref, nw_ref, exp_ref, tril_ref, o_ref, tailx_ref, tailbc_ref, ht_ref):
    L = SSD_CHUNK

    @pl.when(pl.program_id(1) == 0)
    def _():
        tailx_ref[...] = jnp.zeros_like(tailx_ref)
        tailbc_ref[...] = jnp.zeros_like(tailbc_ref)
        ht_ref[...] = jnp.zeros_like(ht_ref)

    x_raw = xs_ref[...].astype(F32)
    bc_raw = bc_ref[...].astype(F32)
    xs = _silu(_causal_conv(x_raw, tailx_ref[...], cwx_ref[...], cbx_ref[...]))
    bc = _silu(_causal_conv(bc_raw, tailbc_ref[...], cwbc_ref[...], cbbc_ref[...]))
    tailx_ref[...] = x_raw[L - 8:L]
    tailbc_ref[...] = bc_raw[L - 8:L]

    dt = _softplus(sm_ref[...] + dtb_ref[...])
    loga = dt * nega_ref[...]
    acs = jnp.dot(tril_ref[...], loga, preferred_element_type=F32, precision=HIGHEST)
    acs_t = acs.T
    expand = exp_ref[...]
    dt_e = jnp.dot(dt, expand, preferred_element_type=F32, precision=HIGHEST)
    acs_e = jnp.dot(acs, expand, preferred_element_type=F32, precision=HIGHEST)
    last_e = acs_e[L - 1:L]
    xdt = xs * dt_e
    xdt_b = xdt.astype(BF16)
    xw_b = (xdt * jnp.exp(last_e - acs_e)).astype(BF16)
    dec_in = jnp.exp(acs_e)
    dec_chunk = jnp.exp(last_e)

    row = lax.broadcasted_iota(jnp.int32, (L, L), 0)
    col = lax.broadcasted_iota(jnp.int32, (L, L), 1)
    causal = row >= col
    lo_half = col < SSD_HEAD_DIM
    heads_per_group = SSD_HEADS // SSD_GROUPS
    n_bc = SSD_GROUPS * SSD_STATE

    y_tiles = []
    for g in range(SSD_GROUPS):
        bm = bc[:, g * SSD_STATE:(g + 1) * SSD_STATE]
        cm_b = bc[:, n_bc + g * SSD_STATE:n_bc + (g + 1) * SSD_STATE].astype(BF16)
        bm_t_b = bm.T.astype(BF16)
        cb = jnp.dot(cm_b, bm_t_b, preferred_element_type=F32)
        for j in range(g * heads_per_group // 2, (g + 1) * heads_per_group // 2):
            sl = slice(j * LANES, (j + 1) * LANES)
            xt = xdt_b[:, sl]
            y_pair = None
            for sub in range(2):
                h = 2 * j + sub
                seg = acs[:, h:h + 1] - acs_t[h:h + 1, :]
                m = (cb * jnp.exp(jnp.where(causal, seg, -jnp.inf))).astype(BF16)
                xh = jnp.where(lo_half if sub == 0 else ~lo_half, xt, jnp.zeros_like(xt))
                yd = jnp.dot(m, xh, preferred_element_type=F32)
                y_pair = yd if y_pair is None else y_pair + yd
            h_t = ht_ref[:, sl]
            y_off = jnp.dot(cm_b, h_t.astype(BF16), preferred_element_type=F32) * dec_in[:, sl]
            st = jnp.dot(bm_t_b, xw_b[:, sl], preferred_element_type=F32)
            ht_ref[:, sl] = h_t * dec_chunk[:, sl] + st
            y_tiles.append(y_pair + y_off)
    y = jnp.concatenate(y_tiles, axis=1) + dsk_ref[...] * xs
    y = y * _silu(z_ref[...].astype(F32))
    gw = SSD_INNER // SSD_GROUPS
    outs = []
    for g in range(SSD_GROUPS):
        yg = y[:, g * gw:(g + 1) * gw]
        ms = jnp.mean(yg * yg, -1, keepdims=True)
        outs.append(yg * lax.rsqrt(ms + RMS_EPS) * nw_ref[:, g * gw:(g + 1) * gw])
    o_ref[...] = jnp.concatenate(outs, axis=1).astype(o_ref.dtype)


def _ssd_mixer(proj, small, bsz, seq, conv_w, conv_b, dt_bias, a_log, d_skip, norm_w):
    L = SSD_CHUNK
    nc = seq // L
    pad_lanes = LANES - SSD_HEADS
    cw = jnp.pad(conv_w.astype(F32), ((0, 8 - SSD_CONV), (0, 0)))
    cwx, cwbc = cw[:, :SSD_INNER], cw[:, SSD_INNER:]
    cb = conv_b.astype(F32)[None, :]
    cbx, cbbc = cb[:, :SSD_INNER], cb[:, SSD_INNER:]
    dtb = jnp.pad(dt_bias.astype(F32), (0, pad_lanes))[None, :]
    nega = jnp.pad(-jnp.exp(a_log.astype(F32)), (0, pad_lanes))[None, :]
    dsk = jnp.repeat(d_skip.astype(F32), SSD_HEAD_DIM)[None, :]
    nw = norm_w.astype(F32)[None, :]
    expand = (np.arange(LANES)[:, None] == (np.arange(SSD_INNER)[None, :] // SSD_HEAD_DIM)).astype(np.float32)
    tril = np.tril(np.ones((L, L), np.float32))
    rows = lambda b, c: b * nc + c
    full = lambda shape: pl.BlockSpec(shape, lambda b, c: (0,) * len(shape))
    return pl.pallas_call(
        _ssd_kernel,
        grid=(bsz, nc),
        in_specs=[
            pl.BlockSpec((L, 1024), lambda b, c: (rows(b, c), _CB_Z)),
            pl.BlockSpec((L, 1024), lambda b, c: (rows(b, c), _CB_XS)),
            pl.BlockSpec((L, 512), lambda b, c: (rows(b, c), _CB_BC)),
            pl.BlockSpec((L, LANES), lambda b, c: (rows(b, c), 0)),
            full((8, SSD_INNER)), full((8, 512)), full((1, SSD_INNER)), full((1, 512)),
            full((1, LANES)), full((1, LANES)), full((1, SSD_INNER)), full((1, SSD_INNER)),
            full((LANES, SSD_INNER)), full((L, L)),
        ],
        out_specs=pl.BlockSpec((L, SSD_INNER), lambda b, c: (rows(b, c), 0)),
        out_shape=jax.ShapeDtypeStruct((bsz * seq, SSD_INNER), BF16),
        scratch_shapes=[pltpu.VMEM((8, SSD_INNER), F32), pltpu.VMEM((8, 512), F32),
                        pltpu.VMEM((SSD_STATE, SSD_INNER), F32)],
        compiler_params=_cparams(("parallel", "arbitrary")),
        name="ssd_mixer",
    )(proj, proj, proj, small, cwx, cwbc, cbx, cbbc, dtb, nega, dsk, nw, jnp.asarray(expand), jnp.asarray(tril))


def _s5_scan_kernel(u_ref, bd_ref, w_ref, v_ref, a1_ref, a2_ref, y_ref, toep_ref, sc_ref, hin_ref, hc_ref, *, rows):
    R = rows

    @pl.when((pl.program_id(1) == 0) & (pl.program_id(2) == 0))
    def _():
        toep_ref[...] = jnp.zeros_like(toep_ref)
        for s in range(S5_CHUNK):
            for l in range(s, S5_CHUNK):
                toep_ref[s * LANES:(s + 1) * LANES, l * LANES:(l + 1) * LANES] = bd_ref[0, l - s]

    @pl.when(pl.program_id(2) == 0)
    def _():
        hc_ref[...] = jnp.zeros_like(hc_ref)

    parts = [u_ref[pl.ds(s, R, stride=S5_CHUNK), :].astype(BF16) for s in range(S5_CHUNK)]
    uf = jnp.concatenate(parts, axis=1)
    y_intra = jnp.dot(uf, toep_ref[...], preferred_element_type=F32)
    sc_ref[...] = jnp.dot(uf, w_ref[0], preferred_element_type=F32)
    a1 = a1_ref[0]
    a2 = a2_ref[0]
    half = S5_TILE_GROUPS * S5_STATE

    def step(r, h):
        hin_ref[pl.ds(r, 1), :] = h
        return a1 * h + a2 * pltpu.roll(h, half, 1) + sc_ref[pl.ds(r, 1), :]

    hc_ref[...] = lax.fori_loop(0, R, step, hc_ref[...])
    y = y_intra + jnp.dot(hin_ref[...].astype(BF16), v_ref[0], preferred_element_type=F32)
    for l in range(S5_CHUNK):
        y_ref[pl.ds(l, R, stride=S5_CHUNK), :] = y[:, l * LANES:(l + 1) * LANES]


def _s5_tables(lam_re, lam_im, log_dt, b_re, b_im, c_re, c_im):
    f32 = F32
    lc = S5_CHUNK
    tg = S5_TILE_GROUPS
    nt = S5_GROUPS // tg
    lr = jnp.minimum(lam_re.astype(f32), -1e-4)[None, :]
    li = lam_im.astype(f32)[None, :]
    dt = jnp.exp(log_dt.astype(f32))[:, None]
    mag = jnp.exp(lr * dt)
    ab_re, ab_im = mag * jnp.cos(li * dt), mag * jnp.sin(li * dt)
    den = lr * lr + li * li
    nr, ni = ab_re - 1.0, ab_im
    f_re, f_im = (nr * lr + ni * li) / den, (ni * lr - nr * li) / den
    b_re, b_im = b_re.astype(f32), b_im.astype(f32)
    bb_re = f_re[..., None] * b_re - f_im[..., None] * b_im
    bb_im = f_re[..., None] * b_im + f_im[..., None] * b_re
    tau = jnp.arange(lc + 1, dtype=f32)[:, None, None]
    pw_mag = jnp.exp(tau * (lr * dt)[None])
    pw_re = pw_mag * jnp.cos(tau * (li * dt)[None])
    pw_im = pw_mag * jnp.sin(tau * (li * dt)[None])
    c_re, c_im = c_re.astype(f32), c_im.astype(f32)
    cl_re = c_re[None] * pw_re[:, :, None, :] - c_im[None] * pw_im[:, :, None, :]
    cl_im = c_re[None] * pw_im[:, :, None, :] + c_im[None] * pw_re[:, :, None, :]
    hp = lax.Precision.HIGHEST
    kern = (jnp.einsum('tgop,gpi->tgoi', cl_re[:lc], bb_re, precision=hp)
            - jnp.einsum('tgop,gpi->tgoi', cl_im[:lc], bb_im, precision=hp))
    eye = jnp.eye(tg, dtype=f32)
    toep = jnp.einsum('tTgoi,gh->Ttgiho', kern.reshape(lc, nt, tg, S5_GROUP, S5_GROUP), eye)
    toep = toep.reshape(nt, lc, LANES, LANES)
    rev = pw_re[lc - 1 - np.arange(lc)], pw_im[lc - 1 - np.arange(lc)]
    wr = rev[0][..., None] * bb_re[None] - rev[1][..., None] * bb_im[None]
    wi = rev[0][..., None] * bb_im[None] + rev[1][..., None] * bb_re[None]
    wst = jnp.stack([wr, wi], 0).reshape(2, lc, nt, tg, S5_STATE, S5_GROUP)
    w_in = jnp.einsum('rsTgpi,gh->Tsgirhp', wst, eye).reshape(nt, lc * LANES, 2 * tg * S5_STATE)
    vst = jnp.stack([cl_re[1:], -cl_im[1:]], 0).reshape(2, lc, nt, tg, S5_GROUP, S5_STATE)
    v_out = jnp.einsum('rlTgop,gh->Trgplho', vst, eye).reshape(nt, 2 * tg * S5_STATE, lc * LANES)
    a_re = pw_re[lc].reshape(nt, 1, tg * S5_STATE)
    a_im = pw_im[lc].reshape(nt, 1, tg * S5_STATE)
    a1 = jnp.concatenate([a_re, a_re], -1)
    a2 = jnp.concatenate([-a_im, a_im], -1)
    return toep.astype(BF16), w_in.astype(BF16), v_out.astype(BF16), a1, a2


def _s5_scan(u, bsz, seq, tables, rows):
    toep, w_in, v_out, a1, a2 = tables
    nt = S5_GROUPS // S5_TILE_GROUPS
    lc = S5_CHUNK
    rows = min(rows, seq // lc)
    nblk = seq // (lc * rows)
    tok = rows * lc
    ns = 2 * S5_TILE_GROUPS * S5_STATE
    return pl.pallas_call(
        functools.partial(_s5_scan_kernel, rows=rows),
        grid=(nt, bsz, nblk),
        in_specs=[
            pl.BlockSpec((tok, LANES), lambda t, b, r: (b * nblk + r, t)),
            pl.BlockSpec((1, lc, LANES, LANES), lambda t, b, r: (t, 0, 0, 0)),
            pl.BlockSpec((1, lc * LANES, ns), lambda t, b, r: (t, 0, 0)),
            pl.BlockSpec((1, ns, lc * LANES), lambda t, b, r: (t, 0, 0)),
            pl.BlockSpec((1, 1, ns), lambda t, b, r: (t, 0, 0)),
            pl.BlockSpec((1, 1, ns), lambda t, b, r: (t, 0, 0)),
        ],
        out_specs=pl.BlockSpec((tok, LANES), lambda t, b, r: (b * nblk + r, t)),
        out_shape=jax.ShapeDtypeStruct((bsz * seq, S5_WIDTH), F32),
        scratch_shapes=[pltpu.VMEM((lc * LANES, lc * LANES), BF16), pltpu.VMEM((rows, ns), F32),
                        pltpu.VMEM((rows, ns), F32), pltpu.VMEM((1, ns), F32)],
        compiler_params=_cparams(("arbitrary", "arbitrary", "arbitrary")),
        name="s5_scan",
    )(u, toep, w_in, v_out, a1, a2)


def _s5_post_kernel(y_ref, u_ref, d_ref, w_ref, b_ref, o_ref):
    g = jax.nn.gelu(y_ref[...] + d_ref[...] * u_ref[...], approximate=True)
    gate = jnp.dot(g.astype(BF16), w_ref[...], preferred_element_type=F32) + b_ref[...]
    o_ref[...] = (g * jax.nn.sigmoid(gate)).astype(o_ref.dtype)


def _s5_post(y, u, d_skip, w_glu, b_glu, tm=512):
    t = y.shape[0]
    tm = min(tm, t)
    row = pl.BlockSpec((tm, S5_WIDTH), lambda i: (i, 0))
    vec = pl.BlockSpec((1, S5_WIDTH), lambda i: (0, 0))
    return pl.pallas_call(
        _s5_post_kernel,
        grid=(t // tm,),
        in_specs=[row, row, vec, pl.BlockSpec((S5_WIDTH, S5_WIDTH), lambda i: (0, 0)), vec],
        out_specs=row,
        out_shape=jax.ShapeDtypeStruct((t, S5_WIDTH), BF16),
        compiler_params=_cparams(("parallel",)),
        name="s5_post",
    )(y, u, d_skip.astype(F32)[None, :], w_glu.astype(BF16), b_glu.astype(F32)[None, :])


def _ret_kernel(q_ref, k_ref, v_ref, g_ref, cos_ref, sin_ref, intra_ref, dend_ref, osc_ref, nw_ref,
                o_ref, r_ref, *, chunk_decay):
    L = RET_CHUNK

    @pl.when(pl.program_id(1) == 0)
    def _():
        r_ref[...] = jnp.zeros_like(r_ref)

    cos = cos_ref[...]
    sin = sin_ref[...]
    lane = lax.broadcasted_iota(jnp.int32, (L, LANES), 1)
    rowi = lax.broadcasted_iota(jnp.int32, (LANES, L), 0)
    half = RET_QK_DIM // 2
    outs = []
    for t in range(RET_HEADS // 2):
        sl = slice(t * LANES, (t + 1) * LANES)
        qt = q_ref[:, sl].astype(F32)
        kt = k_ref[:, sl].astype(F32)
        qr = qt * cos + pltpu.roll(qt, 64, 1) * sin
        kr = (kt * cos + pltpu.roll(kt, 64, 1) * sin) * (RET_QK_DIM ** -0.5)
        kr_b = kr.astype(BF16)
        kr_t = kr.T
        for sub in range(2):
            h = 2 * t + sub
            vs = slice(h * RET_V_DIM, (h + 1) * RET_V_DIM)
            qm = jnp.where(((lane // half) % 2) == sub, qr, 0.0).astype(BF16)
            s = lax.dot_general(qm, kr_b, (((1,), (1,)), ((), ())), preferred_element_type=F32)
            p = (s * intra_ref[h]).astype(BF16)
            vh = v_ref[:, vs]
            state = r_ref[h]
            y_in = jnp.dot(p, vh, preferred_element_type=F32)
            y_off = jnp.dot(qm, state.astype(BF16), preferred_element_type=F32) * osc_ref[:, vs]
            ktm = jnp.where(((rowi // half) % 2) == sub, kr_t * dend_ref[h:h + 1, :], 0.0).astype(BF16)
            r_ref[h] = state * chunk_decay[h] + jnp.dot(ktm, vh, preferred_element_type=F32)
            y = y_in + y_off
            mu = jnp.mean(y, -1, keepdims=True)
            yc = y - mu
            var = jnp.mean(yc * yc, -1, keepdims=True)
            yn = yc * lax.rsqrt(var + LN_EPS) * nw_ref[:, vs]
            outs.append(_silu(g_ref[:, vs].astype(F32)) * yn)
    o_ref[...] = jnp.concatenate(outs, axis=1).astype(o_ref.dtype)


def _ret_mixer(proj, bsz, seq, positions, norm_w):
    L = RET_CHUNK
    nc = seq // L
    half = RET_QK_DIM // 2
    inv_freq = 1.0 / (ROPE_BASE ** (jnp.arange(half, dtype=F32) / half))
    ang = positions.astype(F32).reshape(bsz * seq, 1) * inv_freq[None, :]
    cos32, sin32 = jnp.cos(ang), jnp.sin(ang)
    cos_t = jnp.concatenate([cos32] * 4, axis=1)
    sin_t = jnp.concatenate([-sin32, -sin32, sin32, sin32], axis=1)
    log_gamma = np.log1p(-np.exp2(-5.0 - np.arange(RET_HEADS, dtype=np.float64)))
    pos = np.arange(L, dtype=np.float64)
    rel = pos[:, None] - pos[None, :]
    intra = np.where(rel[None] >= 0, np.exp(rel[None] * log_gamma[:, None, None]), 0.0).astype(np.float32)
    dend = np.exp((L - 1 - pos)[None, :] * log_gamma[:, None]).astype(np.float32)
    osc = np.repeat(np.exp((pos + 1)[:, None] * log_gamma[None, :]), RET_V_DIM, axis=1).astype(np.float32)
    chunk_decay = tuple(float(v) for v in np.exp(L * log_gamma))
    rows = lambda b, c: b * nc + c
    full = lambda shape: pl.BlockSpec(shape, lambda b, c: (0,) * len(shape))
    return pl.pallas_call(
        functools.partial(_ret_kernel, chunk_decay=chunk_decay),
        grid=(bsz, nc),
        in_specs=[
            pl.BlockSpec((L, 512), lambda b, c: (rows(b, c), _CB_RQ)),
            pl.BlockSpec((L, 512), lambda b, c: (rows(b, c), _CB_RK)),
            pl.BlockSpec((L, 1024), lambda b, c: (rows(b, c), _CB_RV)),
            pl.BlockSpec((L, 1024), lambda b, c: (rows(b, c), _CB_RG)),
            pl.BlockSpec((L, LANES), lambda b, c: (rows(b, c), 0)),
            pl.BlockSpec((L, LANES), lambda b, c: (rows(b, c), 0)),
            full((RET_HEADS, L, L)), full((RET_HEADS, L)), full((L, RET_WIDTH)), full((1, RET_WIDTH)),
        ],
        out_specs=pl.BlockSpec((L, RET_WIDTH), lambda b, c: (rows(b, c), 0)),
        out_shape=jax.ShapeDtypeStruct((bsz * seq, RET_WIDTH), BF16),
        scratch_shapes=[pltpu.VMEM((RET_HEADS, LANES, RET_V_DIM), F32)],
        compiler_params=_cparams(("parallel", "arbitrary")),
        name="retention_mixer",
    )(proj, proj, proj, proj, cos_t, sin_t, jnp.asarray(intra), jnp.asarray(dend), jnp.asarray(osc),
      norm_w.astype(F32)[None, :])


def _gla_kernel(q_ref, k_ref, v_ref, r_ref, sm_ref, wa_ref, ba_ref, tblk_ref, nw_ref, o_ref, s_ref):
    LB = GLA_BLOCK
    LC = GLA_CHUNK

    @pl.when(pl.program_id(1) == 0)
    def _():
        s_ref[...] = jnp.zeros_like(s_ref)

    logit = jnp.dot(sm_ref[...], wa_ref[...], preferred_element_type=F32, precision=HIGHEST) + ba_ref[...]
    la = (jnp.minimum(logit, 0.0) - jnp.log1p(jnp.exp(-jnp.abs(logit)))) * (1.0 / GLA_TAU)
    b = jnp.dot(tblk_ref[...], la, preferred_element_type=F32, precision=HIGHEST)
    rowf = lax.broadcasted_iota(jnp.int32, (LB, GLA_QK), 0)
    b_end = jnp.where(rowf < LC, b[LC - 1:LC], b[LB - 1:LB])
    q = q_ref[...].astype(F32) * (GLA_QK_DIM ** -0.5)
    k = k_ref[...].astype(F32)
    q_dec = q * jnp.exp(b)
    k_inv = (k * jnp.exp(-b)).astype(BF16)
    k_dec = k * jnp.exp(b_end - b)
    row = lax.broadcasted_iota(jnp.int32, (LB, LB), 0)
    col = lax.broadcasted_iota(jnp.int32, (LB, LB), 1)
    blk_causal = (row >= col) & ((row // LC) == (col // LC))
    first_rows = row < LC
    first_vrows = lax.broadcasted_iota(jnp.int32, (LB, GLA_V_DIM), 0) < LC
    outs = []
    for h in range(GLA_HEADS):
        sl = slice(h * GLA_QK_DIM, (h + 1) * GLA_QK_DIM)
        vs = slice(h * GLA_V_DIM, (h + 1) * GLA_V_DIM)
        qd = q_dec[:, sl]
        qd_b = qd.astype(BF16)
        att = lax.dot_general(qd_b, k_inv[:, sl], (((1,), (1,)), ((), ())), preferred_element_type=F32)
        att = jnp.where(blk_causal, att, 0.0).astype(BF16)
        vh = v_ref[:, vs]
        y = jnp.dot(att, vh, preferred_element_type=F32)
        s0 = s_ref[h]
        kd_t = k_dec[:, sl].T.astype(BF16)
        kv0 = jnp.dot(kd_t, jnp.where(first_vrows, vh, jnp.zeros_like(vh)), preferred_element_type=F32)
        kv1 = jnp.dot(kd_t, jnp.where(first_vrows, jnp.zeros_like(vh), vh), preferred_element_type=F32)
        b_t = b[:, sl].T
        s1 = s0 * jnp.exp(b_t[:, LC - 1:LC]) + kv0
        s_ref[h] = s1 * jnp.exp(b_t[:, LB - 1:LB]) + kv1
        y = y + jnp.dot(jnp.where(first_rows, qd, 0.0).astype(BF16), s0.astype(BF16), preferred_element_type=F32)
        y = y + jnp.dot(jnp.where(first_rows, 0.0, qd).astype(BF16), s1.astype(BF16), preferred_element_type=F32)
        ms = jnp.mean(y * y, -1, keepdims=True)
        yn = y * lax.rsqrt(ms + RMS_EPS) * nw_ref[:, vs]
        outs.append(yn * _silu(r_ref[:, vs].astype(F32)))
    o_ref[...] = jnp.concatenate(outs, axis=1).astype(o_ref.dtype)


def _gla_mixer(proj, small, bsz, seq, w_alpha, b_alpha, norm_w):
    LB = GLA_BLOCK
    nb = seq // LB
    wa = jnp.zeros((LANES, GLA_QK), F32).at[SSD_HEADS:SSD_HEADS + GLA_GATE_RANK].set(w_alpha.astype(F32))
    idx = np.arange(LB)
    tblk = ((idx[:, None] >= idx[None, :]) & ((idx[:, None] // GLA_CHUNK) == (idx[None, :] // GLA_CHUNK)))
    rows = lambda b, c: b * nb + c
    full = lambda shape: pl.BlockSpec(shape, lambda b, c: (0,) * len(shape))
    return pl.pallas_call(
        _gla_kernel,
        grid=(bsz, nb),
        in_specs=[
            pl.BlockSpec((LB, 512), lambda b, c: (rows(b, c), _CB_GQ)),
            pl.BlockSpec((LB, 512), lambda b, c: (rows(b, c), _CB_GK)),
            pl.BlockSpec((LB, 1024), lambda b, c: (rows(b, c), _CB_GV)),
            pl.BlockSpec((LB, 1024), lambda b, c: (rows(b, c), _CB_GR)),
            pl.BlockSpec((LB, LANES), lambda b, c: (rows(b, c), 0)),
            full((LANES, GLA_QK)), full((1, GLA_QK)), full((LB, LB)), full((1, GLA_WIDTH)),
        ],
        out_specs=pl.BlockSpec((LB, GLA_WIDTH), lambda b, c: (rows(b, c), 0)),
        out_shape=jax.ShapeDtypeStruct((bsz * seq, GLA_WIDTH), BF16),
        scratch_shapes=[pltpu.VMEM((GLA_HEADS, GLA_QK_DIM, GLA_V_DIM), F32)],
        compiler_params=_cparams(("parallel", "arbitrary")),
        name="gla_mixer",
    )(proj, proj, proj, proj, small, wa, b_alpha.astype(F32)[None, :], jnp.asarray(tblk.astype(np.float32)),
      norm_w.astype(F32)[None, :])


def _merge_kernel(ya_ref, yb_ref, yc_ref, yd_ref, g0_ref, g1_ref, g2_ref, g3_ref, wb_ref, o_ref):
    acc = None
    for n, (y_ref, g_ref) in enumerate(((ya_ref, g0_ref), (yb_ref, g1_ref), (yc_ref, g2_ref), (yd_ref, g3_ref))):
        br = jnp.dot(y_ref[...], wb_ref[n], preferred_element_type=F32)
        term = jax.nn.sigmoid(g_ref[...].astype(F32)) * br
        acc = term if acc is None else acc + term
    o_ref[...] = acc.astype(o_ref.dtype)


def _merge(ys, proj, w_branch, tm=1024, tn=512):
    t = ys[0].shape[0]
    tm = min(tm, t)
    nj = D_MODEL // tn
    yspec = pl.BlockSpec((tm, BRANCH_WIDTH), lambda i, j: (i, 0))
    gspec = lambda n: pl.BlockSpec((tm, tn), lambda i, j: (i, n * nj + j))
    return pl.pallas_call(
        _merge_kernel,
        grid=(t // tm, nj),
        in_specs=[yspec] * 4 + [gspec(n) for n in range(N_BRANCH)]
                 + [pl.BlockSpec((N_BRANCH, BRANCH_WIDTH, tn), lambda i, j: (0, 0, j))],
        out_specs=pl.BlockSpec((tm, tn), lambda i, j: (i, j)),
        out_shape=jax.ShapeDtypeStruct((t, D_MODEL), BF16),
        compiler_params=_cparams(("parallel", "parallel")),
        name="branch_merge",
    )(*ys, proj, proj, proj, proj, w_branch.astype(BF16))


def _proj_ln_kernel(m_ref, w_ref, h_ref, lw_ref, lb_ref, o_ref, ob_ref):
    mix = jnp.dot(m_ref[...], w_ref[...], preferred_element_type=F32)
    out = _layer_norm(DEEPNORM_ALPHA * h_ref[...] + mix, lw_ref[...], lb_ref[...])
    o_ref[...] = out
    ob_ref[...] = out.astype(BF16)


def _proj_ln(merged, w_out, h, ln_w, ln_b, tm=512):
    t = h.shape[0]
    tm = min(tm, t)
    row = pl.BlockSpec((tm, D_MODEL), lambda i: (i, 0))
    vec = pl.BlockSpec((1, D_MODEL), lambda i: (0, 0))
    return pl.pallas_call(
        _proj_ln_kernel,
        grid=(t // tm,),
        in_specs=[row, pl.BlockSpec((D_MODEL, D_MODEL), lambda i: (0, 0)), row, vec, vec],
        out_specs=[row, row],
        out_shape=[jax.ShapeDtypeStruct((t, D_MODEL), F32), jax.ShapeDtypeStruct((t, D_MODEL), BF16)],
        compiler_params=_cparams(("parallel",)),
        name="out_proj_ln",
    )(merged, w_out.astype(BF16), h, ln_w.astype(F32)[None, :], ln_b.astype(F32)[None, :])


def _router_kernel(h_ref, w_ref, b_ref, tri_ref, idx_ref, gate_ref, rank_ref, cnt_ref, run_ref):
    @pl.when(pl.program_id(0) == 0)
    def _():
        run_ref[...] = jnp.zeros_like(run_ref)

    logits = _dot3(w_ref[...], h_ref[...], ((1,), (1,))) + b_ref[...]
    eid = lax.broadcasted_iota(jnp.int32, logits.shape, 0)
    vals = logits
    run = run_ref[...]
    top_v, top_i, ranks = [], [], []
    for _ in range(TOP_K):
        m = jnp.max(vals, axis=0, keepdims=True)
        sel = jnp.min(jnp.where(vals == m, eid, N_EXPERTS), axis=0, keepdims=True)
        hit = eid == sel
        top_v.append(m)
        top_i.append(sel)
        vals = jnp.where(hit, -jnp.inf, vals)
        onehot = jnp.where(hit, 1.0, 0.0)
        before = jnp.dot(onehot.astype(BF16), tri_ref[...], preferred_element_type=F32)
        ranks.append(jnp.sum(onehot * (run + before), axis=0, keepdims=True))
        run = run + jnp.sum(onehot, axis=1, keepdims=True)
    run_ref[...] = run
    ex = [jnp.exp(v - top_v[0]) for v in top_v]
    den = ex[0] + ex[1] + ex[2] + ex[3]
    zi = jnp.zeros_like(top_i[0])
    zf = jnp.zeros_like(den)
    idx_ref[...] = jnp.concatenate(top_i + [zi] * (8 - TOP_K), axis=0)
    gate_ref[...] = jnp.concatenate([e / den for e in ex] + [zf] * (8 - TOP_K), axis=0)
    rank_ref[...] = jnp.concatenate(ranks + [zf] * (8 - TOP_K), axis=0).astype(jnp.int32)
    cnt_ref[...] = jnp.broadcast_to(run, cnt_ref.shape).astype(jnp.int32)


def _router(h, router_w, router_b, tm=512):
    t = h.shape[0]
    tm = min(tm, t)
    tri = np.triu(np.ones((tm, tm), np.float32), 1)
    tok = pl.BlockSpec((8, tm), lambda i: (0, i))
    return pl.pallas_call(
        _router_kernel,
        grid=(t // tm,),
        in_specs=[pl.BlockSpec((tm, D_MODEL), lambda i: (i, 0)),
                  pl.BlockSpec((N_EXPERTS, D_MODEL), lambda i: (0, 0)),
                  pl.BlockSpec((N_EXPERTS, 1), lambda i: (0, 0)),
                  pl.BlockSpec((tm, tm), lambda i: (0, 0))],
        out_specs=[tok, tok, tok, pl.BlockSpec((N_EXPERTS, LANES), lambda i: (0, 0))],
        out_shape=[jax.ShapeDtypeStruct((8, t), jnp.int32), jax.ShapeDtypeStruct((8, t), F32),
                   jax.ShapeDtypeStruct((8, t), jnp.int32), jax.ShapeDtypeStruct((N_EXPERTS, LANES), jnp.int32)],
        scratch_shapes=[pltpu.VMEM((N_EXPERTS, 1), F32)],
        compiler_params=_cparams(("arbitrary",)),
        name="router_topk",
    )(h, router_w.astype(F32).T, router_b.astype(F32)[:, None], jnp.asarray(tri, BF16))


def _dispatch_kernel(idx_hbm, h_ref, xb_ref, idx_ref, zero_ref, isem, rsem, *, tm, n_pad):
    i = pl.program_id(0)
    cp = pltpu.make_async_copy(idx_hbm.at[i], idx_ref, isem)
    cp.start()
    zero_ref[...] = jnp.zeros_like(zero_ref)
    cp.wait()

    def issue(r, carry):
        for k in range(TOP_K):
            pltpu.make_async_copy(h_ref.at[pl.ds(r, 1)], xb_ref.at[pl.ds(idx_ref[k * tm + r], 1)], rsem).start()
        return carry

    lax.fori_loop(0, tm, issue, 0, unroll=8)

    def issue_pad(r, carry):
        pltpu.make_async_copy(zero_ref.at[pl.ds(0, 1)], xb_ref.at[pl.ds(idx_ref[TOP_K * tm + r], 1)], rsem).start()
        return carry

    lax.fori_loop(0, n_pad, issue_pad, 0, unroll=8)
    for _ in range(TOP_K):
        pltpu.make_async_copy(h_ref, xb_ref.at[pl.ds(0, tm)], rsem).wait()
    pltpu.make_async_copy(xb_ref.at[pl.ds(0, n_pad)], xb_ref.at[pl.ds(0, n_pad)], rsem).wait()


def _dispatch(h, dest, pad_rows, n_rows, tm=512):
    t = h.shape[0]
    tm = min(tm, t)
    nt = t // tm
    n_pad = pad_rows.shape[0] // nt
    idx = jnp.concatenate([dest.reshape(TOP_K, nt, tm).transpose(1, 0, 2).reshape(nt, TOP_K * tm),
                           pad_rows.reshape(nt, n_pad)], axis=1)
    return pl.pallas_call(
        functools.partial(_dispatch_kernel, tm=tm, n_pad=n_pad),
        grid=(nt,),
        in_specs=[pl.BlockSpec(memory_space=pl.ANY), pl.BlockSpec((tm, D_MODEL), lambda i: (i, 0))],
        out_specs=pl.BlockSpec(memory_space=pl.ANY),
        out_shape=jax.ShapeDtypeStruct((n_rows, D_MODEL), F32),
        scratch_shapes=[pltpu.SMEM((TOP_K * tm + n_pad,), jnp.int32), pltpu.VMEM((8, D_MODEL), F32),
                        pltpu.SemaphoreType.DMA(()), pltpu.SemaphoreType.DMA(())],
        compiler_params=_cparams(("arbitrary",)),
        name="moe_dispatch",
    )(idx, h)


def _expert_kernel(be_ref, nb_ref, x_ref, wgu_ref, bgu_ref, wd_ref, bd_ref, o_ref):
    i = pl.program_id(0)
    n_used = nb_ref[0]

    @pl.when(i < n_used)
    def _():
        xb = x_ref[...].astype(BF16)
        hgu = jnp.dot(xb, wgu_ref[0], preferred_element_type=F32) + bgu_ref[0]
        g = jnp.minimum(hgu[:, :D_EXPERT], SWIGLU_LIMIT)
        u = jnp.clip(hgu[:, D_EXPERT:], -SWIGLU_LIMIT, SWIGLU_LIMIT)
        act = (u + 1.0) * (g * jax.nn.sigmoid(SWIGLU_ALPHA * g))
        o_ref[...] = jnp.dot(act.astype(BF16), wd_ref[0], preferred_element_type=F32) + bd_ref[0]

    @pl.when(i >= n_used)
    def _():
        o_ref[...] = jnp.zeros_like(o_ref)


def _expert_params(w_gate_up, b_gate_up, w_down, b_down):
    n = w_gate_up.shape[0] * N_EXPERTS
    return (w_gate_up.astype(BF16).reshape(n, D_MODEL, 2 * D_EXPERT), b_gate_up.astype(F32).reshape(n, 1, 2 * D_EXPERT),
            w_down.astype(BF16).reshape(n, D_EXPERT, D_MODEL), b_down.astype(F32).reshape(n, 1, D_MODEL))


def _experts(xb, block_expert, n_used, expert_params):
    w_gate_up, b_gate_up, w_down, b_down = expert_params
    blk = MOE_BLOCK
    n_blocks = xb.shape[0] // blk
    grid_spec = pltpu.PrefetchScalarGridSpec(
        num_scalar_prefetch=2,
        grid=(n_blocks,),
        in_specs=[
            pl.BlockSpec((blk, D_MODEL), lambda i, be, nb: (i, 0)),
            pl.BlockSpec((1, D_MODEL, 2 * D_EXPERT), lambda i, be, nb: (be[i], 0, 0)),
            pl.BlockSpec((1, 1, 2 * D_EXPERT), lambda i, be, nb: (be[i], 0, 0)),
            pl.BlockSpec((1, D_EXPERT, D_MODEL), lambda i, be, nb: (be[i], 0, 0)),
            pl.BlockSpec((1, 1, D_MODEL), lambda i, be, nb: (be[i], 0, 0)),
        ],
        out_specs=pl.BlockSpec((blk, D_MODEL), lambda i, be, nb: (i, 0)),
    )
    return pl.pallas_call(
        _expert_kernel,
        grid_spec=grid_spec,
        out_shape=jax.ShapeDtypeStruct((n_blocks * blk, D_MODEL), F32),
        compiler_params=_cparams(("arbitrary",)),
        name="expert_ffn",
    )(block_expert, n_used, xb, w_gate_up, b_gate_up, w_down, b_down)


def _combine_kernel(dest_ref, yb_ref, gate_ref, h_ref, lw_ref, lb_ref, o_ref, ob_ref, idx_ref, buf_ref, isem, rsem,
                    *, tm):
    i = pl.program_id(0)
    n = pl.num_programs(0)

    def idx_copy(tile, slot):
        return pltpu.make_async_copy(dest_ref.at[tile], idx_ref.at[slot], isem.at[slot])

    def issue_rows(slot):
        def issue(r, carry):
            for k in range(TOP_K):
                pltpu.make_async_copy(yb_ref.at[pl.ds(idx_ref[slot, k * tm + r], 1)],
                                      buf_ref.at[slot, k, pl.ds(r, 1)], rsem.at[slot]).start()
            return carry

        lax.fori_loop(0, tm, issue, 0, unroll=8)

    slot = i % 2

    @pl.when(i == 0)
    def _():
        idx_copy(0, 0).start()
        idx_copy(0, 0).wait()
        issue_rows(0)

        @pl.when(n > 1)
        def _():
            idx_copy(1, 1).start()

    for s in range(2):
        @pl.when((i + 1 < n) & (slot == 1 - s))
        def _():
            idx_copy(i + 1, s).wait()
            issue_rows(s)

    @pl.when(i + 2 < n)
    def _():
        idx_copy(i + 2, slot).start()

    for k in range(TOP_K):
        pltpu.make_async_copy(yb_ref.at[pl.ds(0, tm)], buf_ref.at[slot, k], rsem.at[slot]).wait()

    rows = min(tm, 16)
    for r0 in range(0, tm, rows):
        rs = slice(r0, r0 + rows)
        gates = gate_ref[rs, :]
        ffn = gates[:, 0:1] * buf_ref[slot, 0, rs, :]
        for k in range(1, TOP_K):
            ffn = ffn + gates[:, k:k + 1] * buf_ref[slot, k, rs, :]
        out = _layer_norm(DEEPNORM_ALPHA * h_ref[rs, :] + ffn, lw_ref[...], lb_ref[...])
        o_ref[rs, :] = out
        ob_ref[rs, :] = out.astype(BF16)


def _combine(dest, yb, gates, h, ln_w, ln_b, tm=256):
    t = h.shape[0]
    tm = min(tm, t)
    row = pl.BlockSpec((tm, D_MODEL), lambda i: (i, 0))
    vec = pl.BlockSpec((1, D_MODEL), lambda i: (0, 0))
    dest_tiles = dest.reshape(TOP_K, t // tm, tm).transpose(1, 0, 2).reshape(t // tm, TOP_K * tm)
    return pl.pallas_call(
        functools.partial(_combine_kernel, tm=tm),
        grid=(t // tm,),
        in_specs=[pl.BlockSpec(memory_space=pl.ANY), pl.BlockSpec(memory_space=pl.ANY),
                  pl.BlockSpec((tm, 8), lambda i: (i, 0)), row, vec, vec],
        out_specs=[row, row],
        out_shape=[jax.ShapeDtypeStruct((t, D_MODEL), F32), jax.ShapeDtypeStruct((t, D_MODEL), BF16)],
        scratch_shapes=[pltpu.SMEM((2, TOP_K * tm), jnp.int32), pltpu.VMEM((2, TOP_K, tm, D_MODEL), F32),
                        pltpu.SemaphoreType.DMA((2,)), pltpu.SemaphoreType.DMA((2,))],
        compiler_params=_cparams(("arbitrary",)),
        name="moe_combine_ln",
    )(dest_tiles, yb, gates, h, ln_w.astype(F32)[None, :], ln_b.astype(F32)[None, :])


def _moe(h, router_w, router_b, expert_params, layer, ln_w, ln_b):
    t = h.shape[0]
    blk = MOE_BLOCK
    top_idx, top_gate, rank, counts = _router(h, router_w, router_b)
    top_idx, rank, counts = top_idx[:TOP_K], rank[:TOP_K], counts[:, 0]
    n_assign = TOP_K * t
    n_blocks = n_assign // blk + N_EXPERTS
    n_rows = n_blocks * blk
    padded = (counts + blk - 1) // blk * blk
    pad_end = jnp.cumsum(padded)
    pad_start = pad_end - padded
    experts = jnp.arange(N_EXPERTS, dtype=jnp.int32)
    dest = rank + jnp.sum(jnp.where(top_idx[..., None] == experts, pad_start, 0), axis=-1)
    blocks = jnp.arange(n_blocks, dtype=jnp.int32)
    block_expert = jnp.sum(((pad_end // blk)[None, :] <= blocks[:, None]).astype(jnp.int32), axis=1)
    block_expert = jnp.minimum(block_expert, N_EXPERTS - 1)
    n_used = (pad_end[-1] // blk).astype(jnp.int32).reshape(1)
    gap_start = jnp.concatenate([pad_start + counts, pad_end[-1:]])
    gap_len = jnp.concatenate([padded - counts, n_rows - pad_end[-1:]])
    gap_end = jnp.cumsum(gap_len)
    j = jnp.arange(n_rows - n_assign, dtype=jnp.int32)
    in_seg = (j[:, None] >= (gap_end - gap_len)[None, :]) & (j[:, None] < gap_end[None, :])
    pad_rows = jnp.sum(jnp.where(in_seg, (gap_start - (gap_end - gap_len))[None, :] + j[:, None], 0), axis=1)
    pad_rows = pad_rows.astype(jnp.int32)
    xb = _dispatch(h, dest.astype(jnp.int32), pad_rows, n_rows)
    yb = _experts(xb, block_expert + layer * N_EXPERTS, n_used, expert_params)
    return _combine(dest.astype(jnp.int32), yb, top_gate.T, h, ln_w, ln_b)


def _ple_kernel(hb_ref, h_ref, p_ref, wg_ref, wp_ref, lw_ref, lb_ref, o_ref, ob_ref):
    gate = jax.nn.sigmoid(jnp.dot(hb_ref[...], wg_ref[...], preferred_element_type=F32))
    emb = jnp.dot(p_ref[...].astype(BF16), wp_ref[...], preferred_element_type=F32)
    out = _layer_norm(DEEPNORM_ALPHA * h_ref[...] + gate * emb, lw_ref[...], lb_ref[...])
    o_ref[...] = out
    ob_ref[...] = out.astype(BF16)


def _ple(hb, h, p_i, w_gate, w_proj, ln_w, ln_b, tm=512):
    t = h.shape[0]
    tm = min(tm, t)
    row = pl.BlockSpec((tm, D_MODEL), lambda i: (i, 0))
    vec = pl.BlockSpec((1, D_MODEL), lambda i: (0, 0))
    return pl.pallas_call(
        _ple_kernel,
        grid=(t // tm,),
        in_specs=[row, row, pl.BlockSpec((tm, PLE_DIM), lambda i: (i, 0)),
                  pl.BlockSpec((D_MODEL, D_MODEL), lambda i: (0, 0)),
                  pl.BlockSpec((PLE_DIM, D_MODEL), lambda i: (0, 0)), vec, vec],
        out_specs=[row, row],
        out_shape=[jax.ShapeDtypeStruct((t, D_MODEL), F32), jax.ShapeDtypeStruct((t, D_MODEL), BF16)],
        compiler_params=_cparams(("parallel",)),
        name="ple_ln",
    )(hb, h, p_i, w_gate.astype(BF16), w_proj.astype(BF16), ln_w.astype(F32)[None, :], ln_b.astype(F32)[None, :])


def _token_mixer(h, hb, bsz, seq, positions, w_in, ssd_conv_w, ssd_conv_b, ssd_dt_bias, ssd_a_log, ssd_d,
                 ssd_norm_w, s5_lambda_re, s5_lambda_im, s5_log_dt, s5_b_re, s5_b_im, s5_c_re, s5_c_im, s5_d,
                 s5_w_glu, s5_b_glu, ret_norm_w, gla_w_alpha, gla_b_alpha, gla_norm_w, w_branch, s5_rows):
    w_main = _main_weight(w_in)
    w_u = w_in[:, _OFF_U:_OFF_U + S5_WIDTH].astype(BF16)
    w_small = jnp.concatenate([w_in[:, _OFF_DT:_OFF_DT + SSD_HEADS], w_in[:, _OFF_GC:_OFF_GC + GLA_GATE_RANK],
                               jnp.zeros((D_MODEL, LANES - SSD_HEADS - GLA_GATE_RANK), w_in.dtype)], axis=1)
    proj = _matmul(hb, w_main, BF16, 1024, 1536, name="in_proj_main")
    u = _matmul(hb, w_u, F32, 1024, 1024, name="in_proj_s5")
    small = _matmul_f32(h, w_small.astype(F32), 512, name="in_proj_small")
    y_a = _ssd_mixer(proj, small, bsz, seq, ssd_conv_w, ssd_conv_b, ssd_dt_bias, ssd_a_log, ssd_d, ssd_norm_w)
    tables = _s5_tables(s5_lambda_re, s5_lambda_im, s5_log_dt, s5_b_re, s5_b_im, s5_c_re, s5_c_im)
    y_b = _s5_post(_s5_scan(u, bsz, seq, tables, s5_rows), u, s5_d, s5_w_glu, s5_b_glu)
    y_c = _ret_mixer(proj, bsz, seq, positions, ret_norm_w)
    y_d = _gla_mixer(proj, small, bsz, seq, gla_w_alpha, gla_b_alpha, gla_norm_w)
    return _merge((y_a, y_b, y_c, y_d), proj, w_branch)


def kernel(x, p, positions, w_in, ssd_conv_w, ssd_conv_b, ssd_dt_bias, ssd_a_log, ssd_d, ssd_norm_w, s5_lambda_re, s5_lambda_im, s5_log_dt, s5_b_re, s5_b_im, s5_c_re, s5_c_im, s5_d, s5_w_glu, s5_b_glu, ret_norm_w, gla_w_alpha, gla_b_alpha, gla_norm_w, w_branch, w_out, ln1_w, ln1_b, router_w, router_b, moe_w_gate_up, moe_b_gate_up, moe_w_down, moe_b_down, ln2_w, ln2_b, ple_w_gate, ple_w_proj, ln3_w, ln3_b):
    s5_rows = 256
    bsz, seq, d = x.shape
    t = bsz * seq
    h = x.reshape(t, d).astype(F32)
    hb = h.astype(BF16)
    expert_params = _expert_params(moe_w_gate_up, moe_b_gate_up, moe_w_down, moe_b_down)
    for i in range(DEPTH):
        merged = _token_mixer(h, hb, bsz, seq, positions, w_in[i], ssd_conv_w[i], ssd_conv_b[i], ssd_dt_bias[i],
                              ssd_a_log[i], ssd_d[i], ssd_norm_w[i], s5_lambda_re[i], s5_lambda_im[i],
                              s5_log_dt[i], s5_b_re[i], s5_b_im[i], s5_c_re[i], s5_c_im[i], s5_d[i],
                              s5_w_glu[i], s5_b_glu[i], ret_norm_w[i], gla_w_alpha[i], gla_b_alpha[i],
                              gla_norm_w[i], w_branch[i], s5_rows)
        h, hb = _proj_ln(merged, w_out[i], h, ln1_w[i], ln1_b[i])
        h, hb = _moe(h, router_w[i], router_b[i], expert_params, i, ln2_w[i], ln2_b[i])
        h, hb = _ple(hb, h, p[i].reshape(t, PLE_DIM), ple_w_gate[i], ple_w_proj[i], ln3_w[i], ln3_b[i])
    return h.reshape(bsz, seq, d).astype(x.dtype)
```

```python
import functools
import math

import numpy as np
import jax
import jax.numpy as jnp
from jax import lax
from jax.experimental import pallas as pl
from jax.experimental.pallas import tpu as pltpu

F32 = jnp.float32
BF16 = jnp.bfloat16
HIGHEST = lax.Precision.HIGHEST

D_MODEL = 2048
DEPTH = 2
PLE_DIM = 256
N_BRANCH = 4
BRANCH_WIDTH = 1024

SSD_HEADS = 16
SSD_HEAD_DIM = 64
SSD_INNER = SSD_HEADS * SSD_HEAD_DIM
SSD_GROUPS = 2
SSD_STATE = 128
SSD_CONV = 4
SSD_CHUNK = 128

S5_WIDTH = 1024
S5_GROUP = 16
S5_GROUPS = S5_WIDTH // S5_GROUP
S5_STATE = 64
S5_CHUNK = 16
S5_TILE_GROUPS = 8

RET_HEADS = 8
RET_QK_DIM = 64
RET_V_DIM = 128
RET_QK = RET_HEADS * RET_QK_DIM
RET_WIDTH = RET_HEADS * RET_V_DIM
RET_CHUNK = 128
ROPE_BASE = 10000.0

GLA_HEADS = 4
GLA_QK_DIM = 128
GLA_V_DIM = 256
GLA_QK = GLA_HEADS * GLA_QK_DIM
GLA_WIDTH = GLA_HEADS * GLA_V_DIM
GLA_GATE_RANK = 16
GLA_TAU = 16.0
GLA_CHUNK = 64
GLA_BLOCK = 128

N_EXPERTS = 32
TOP_K = 4
D_EXPERT = 1024
SWIGLU_ALPHA = 1.702
SWIGLU_LIMIT = 7.0
MOE_BLOCK = 512

DEEPNORM_ALPHA = (2 * DEPTH) ** 0.25
LN_EPS = 1e-5
RMS_EPS = 1e-6

LANES = 128
VMEM_LIMIT = 56 * 1024 * 1024

_OFF_Z = 0
_OFF_XS = 1024
_OFF_BC = 2048
_OFF_DT = 2560
_OFF_U = 2576
_OFF_RQ = 3600
_OFF_RK = 4112
_OFF_RV = 4624
_OFF_RG = 5648
_OFF_GQ = 6672
_OFF_GK = 7184
_OFF_GV = 7696
_OFF_GR = 8720
_OFF_GC = 9744
_OFF_GATE = 9760

_CB_Z, _CB_XS, _CB_RV, _CB_RG, _CB_GV, _CB_GR = 8, 9, 10, 11, 12, 13
_CB_BC, _CB_RQ, _CB_RK, _CB_GQ, _CB_GK = 28, 29, 30, 31, 32
N_MAIN = 16896


def _main_weight(w_in):
    def cols(off, n):
        return w_in[:, off:off + n].astype(BF16)

    def rotary(off):
        half = RET_QK_DIM // 2
        w = cols(off, RET_QK).reshape(D_MODEL, RET_HEADS // 2, 2, 2, half)
        return w.transpose(0, 1, 3, 2, 4).reshape(D_MODEL, RET_QK)

    return jnp.concatenate([
        cols(_OFF_GATE, N_BRANCH * D_MODEL), cols(_OFF_Z, 1024), cols(_OFF_XS, 1024), cols(_OFF_RV, 1024),
        cols(_OFF_RG, 1024), cols(_OFF_GV, 1024), cols(_OFF_GR, 1024), cols(_OFF_BC, 512),
        rotary(_OFF_RQ), rotary(_OFF_RK), cols(_OFF_GQ, 512), cols(_OFF_GK, 512)], axis=1)


def _cparams(sem):
    return pltpu.CompilerParams(dimension_semantics=sem, vmem_limit_bytes=VMEM_LIMIT)


def _silu(x):
    return x * jax.nn.sigmoid(x)


def _softplus(x):
    return jnp.maximum(x, 0.0) + jnp.log1p(jnp.exp(-jnp.abs(x)))


def _layer_norm(x, w, b):
    mu = jnp.mean(x, -1, keepdims=True)
    xc = x - mu
    var = jnp.mean(xc * xc, -1, keepdims=True)
    return xc * lax.rsqrt(var + LN_EPS) * w + b


def _mm_kernel(a_ref, b_ref, o_ref):
    o_ref[...] = jnp.dot(a_ref[...], b_ref[...], preferred_element_type=F32).astype(o_ref.dtype)


def _matmul(a, b, out_dtype, tm, tn, name="matmul"):
    m, k = a.shape
    n = b.shape[1]
    tm, tn = min(tm, m), min(tn, n)
    return pl.pallas_call(
        _mm_kernel,
        grid=(m // tm, n // tn),
        in_specs=[pl.BlockSpec((tm, k), lambda i, j: (i, 0)),
                  pl.BlockSpec((k, tn), lambda i, j: (0, j))],
        out_specs=pl.BlockSpec((tm, tn), lambda i, j: (i, j)),
        out_shape=jax.ShapeDtypeStruct((m, n), out_dtype),
        compiler_params=_cparams(("parallel", "parallel")),
        name=name,
    )(a, b)


def _split_bf16(x):
    hi = x.astype(BF16)
    return hi, (x - hi.astype(F32)).astype(BF16)


def _dot3(a, b, dims):
    a_hi, a_lo = _split_bf16(a)
    b_hi, b_lo = _split_bf16(b)
    dg = functools.partial(lax.dot_general, dimension_numbers=(dims, ((), ())), preferred_element_type=F32)
    return dg(a_hi, b_hi) + (dg(a_hi, b_lo) + dg(a_lo, b_hi))


def _mm3_kernel(a_ref, b_ref, o_ref):
    o_ref[...] = _dot3(a_ref[...], b_ref[...], ((1,), (0,)))


def _matmul_f32(a, b, tm, name):
    m, k = a.shape
    n = b.shape[1]
    tm = min(tm, m)
    return pl.pallas_call(
        _mm3_kernel,
        grid=(m // tm,),
        in_specs=[pl.BlockSpec((tm, k), lambda i: (i, 0)), pl.BlockSpec((k, n), lambda i: (0, 0))],
        out_specs=pl.BlockSpec((tm, n), lambda i: (i, 0)),
        out_shape=jax.ShapeDtypeStruct((m, n), F32),
        compiler_params=_cparams(("parallel",)),
        name=name,
    )(a, b)


def _causal_conv(x, tail, w, b):
    n_tap = SSD_CONV
    acc = x * w[n_tap - 1:n_tap] + b
    x8 = x[0:8]
    acc8 = x8 * w[n_tap - 1:n_tap] + b
    row8 = lax.broadcasted_iota(jnp.int32, x8.shape, 0)
    for s in range(1, n_tap):
        wk = w[n_tap - 1 - s:n_tap - s]
        acc = acc + pltpu.roll(x, s, 0) * wk
        v8 = jnp.where(row8 < s, pltpu.roll(tail, s, 0), pltpu.roll(x8, s, 0))
        acc8 = acc8 + v8 * wk
    return jnp.concatenate([acc8, acc[8:]], axis=0)


def _ssd_kernel(z_ref, xs_ref, bc_ref, sm_ref, cwx_ref, cwbc_ref, cbx_ref, cbbc_ref, dtb_ref,
                nega_ref, dsk_ref, nw_ref, exp_ref, tril_ref, o_ref, tailx_ref, tailbc_ref, ht_ref):
    L = SSD_CHUNK

    @pl.when(pl.program_id(1) == 0)
    def _():
        tailx_ref[...] = jnp.zeros_like(tailx_ref)
        tailbc_ref[...] = jnp.zeros_like(tailbc_ref)
        ht_ref[...] = jnp.zeros_like(ht_ref)

    x_raw = xs_ref[...].astype(F32)
    bc_raw = bc_ref[...].astype(F32)
    xs = _silu(_causal_conv(x_raw, tailx_ref[...], cwx_ref[...], cbx_ref[...]))
    bc = _silu(_causal_conv(bc_raw, tailbc_ref[...], cwbc_ref[...], cbbc_ref[...]))
    tailx_ref[...] = x_raw[L - 8:L]
    tailbc_ref[...] = bc_raw[L - 8:L]

    dt = _softplus(sm_ref[...] + dtb_ref[...])
    loga = dt * nega_ref[...]
    acs = jnp.dot(tril_ref[...], loga, preferred_element_type=F32, precision=HIGHEST)
    acs_t = acs.T
    expand = exp_ref[...]
    dt_e = jnp.dot(dt, expand, preferred_element_type=F32, precision=HIGHEST)
    acs_e = jnp.dot(acs, expand, preferred_element_type=F32, precision=HIGHEST)
    last_e = acs_e[L - 1:L]
    xdt = xs * dt_e
    xdt_b = xdt.astype(BF16)
    xw_b = (xdt * jnp.exp(last_e - acs_e)).astype(BF16)
    dec_in = jnp.exp(acs_e)
    dec_chunk = jnp.exp(last_e)

    row = lax.broadcasted_iota(jnp.int32, (L, L), 0)
    col = lax.broadcasted_iota(jnp.int32, (L, L), 1)
    causal = row >= col
    lo_half = col < SSD_HEAD_DIM
    heads_per_group = SSD_HEADS // SSD_GROUPS
    n_bc = SSD_GROUPS * SSD_STATE

    y_tiles = []
    for g in range(SSD_GROUPS):
        bm = bc[:, g * SSD_STATE:(g + 1) * SSD_STATE]
        cm_b = bc[:, n_bc + g * SSD_STATE:n_bc + (g + 1) * SSD_STATE].astype(BF16)
        bm_t_b = bm.T.astype(BF16)
        cb = jnp.dot(cm_b, bm_t_b, preferred_element_type=F32)
        for j in range(g * heads_per_group // 2, (g + 1) * heads_per_group // 2):
            sl = slice(j * LANES, (j + 1) * LANES)
            xt = xdt_b[:, sl]
            y_pair = None
            for sub in range(2):
                h = 2 * j + sub
                seg = acs[:, h:h + 1] - acs_t[h:h + 1, :]
                m = (cb * jnp.exp(jnp.where(causal, seg, -jnp.inf))).astype(BF16)
                xh = jnp.where(lo_half if sub == 0 else ~lo_half, xt, jnp.zeros_like(xt))
                yd = jnp.dot(m, xh, preferred_element_type=F32)
                y_pair = yd if y_pair is None else y_pair + yd
            h_t = ht_ref[:, sl]
            y_off = jnp.dot(cm_b, h_t.astype(BF16), preferred_element_type=F32) * dec_in[:, sl]
            st = jnp.dot(bm_t_b, xw_b[:, sl], preferred_element_type=F32)
            ht_ref[:, sl] = h_t * dec_chunk[:, sl] + st
            y_tiles.append(y_pair + y_off)
    y = jnp.concatenate(y_tiles, axis=1) + dsk_ref[...] * xs
    y = y * _silu(z_ref[...].astype(F32))
    gw = SSD_INNER // SSD_GROUPS
    outs = []
    for g in range(SSD_GROUPS):
        yg = y[:, g * gw:(g + 1) * gw]
        ms = jnp.mean(yg * yg, -1, keepdims=True)
        outs.append(yg * lax.rsqrt(ms + RMS_EPS) * nw_ref[:, g * gw:(g + 1) * gw])
    o_ref[...] = jnp.concatenate(outs, axis=1).astype(o_ref.dtype)


def _ssd_mixer(proj, small, bsz, seq, conv_w, conv_b, dt_bias, a_log, d_skip, norm_w):
    L = SSD_CHUNK
    nc = seq // L
    pad_lanes = LANES - SSD_HEADS
    cw = jnp.pad(conv_w.astype(F32), ((0, 8 - SSD_CONV), (0, 0)))
    cwx, cwbc = cw[:, :SSD_INNER], cw[:, SSD_INNER:]
    cb = conv_b.astype(F32)[None, :]
    cbx, cbbc = cb[:, :SSD_INNER], cb[:, SSD_INNER:]
    dtb = jnp.pad(dt_bias.astype(F32), (0, pad_lanes))[None, :]
    nega = jnp.pad(-jnp.exp(a_log.astype(F32)), (0, pad_lanes))[None, :]
    dsk = jnp.repeat(d_skip.astype(F32), SSD_HEAD_DIM)[None, :]
    nw = norm_w.astype(F32)[None, :]
    expand = (np.arange(LANES)[:, None] == (np.arange(SSD_INNER)[None, :] // SSD_HEAD_DIM)).astype(np.float32)
    tril = np.tril(np.ones((L, L), np.float32))
    rows = lambda b, c: b * nc + c
    full = lambda shape: pl.BlockSpec(shape, lambda b, c: (0,) * len(shape))
    return pl.pallas_call(
        _ssd_kernel,
        grid=(bsz, nc),
        in_specs=[
            pl.BlockSpec((L, 1024), lambda b, c: (rows(b, c), _CB_Z)),
            pl.BlockSpec((L, 1024), lambda b, c: (rows(b, c), _CB_XS)),
            pl.BlockSpec((L, 512), lambda b, c: (rows(b, c), _CB_BC)),
            pl.BlockSpec((L, LANES), lambda b, c: (rows(b, c), 0)),
            full((8, SSD_INNER)), full((8, 512)), full((1, SSD_INNER)), full((1, 512)),
            full((1, LANES)), full((1, LANES)), full((1, SSD_INNER)), full((1, SSD_INNER)),
            full((LANES, SSD_INNER)), full((L, L)),
        ],
        out_specs=pl.BlockSpec((L, SSD_INNER), lambda b, c: (rows(b, c), 0)),
        out_shape=jax.ShapeDtypeStruct((bsz * seq, SSD_INNER), BF16),
        scratch_shapes=[pltpu.VMEM((8, SSD_INNER), F32), pltpu.VMEM((8, 512), F32),
                        pltpu.VMEM((SSD_STATE, SSD_INNER), F32)],
        compiler_params=_cparams(("parallel", "arbitrary")),
        name="ssd_mixer",
    )(proj, proj, proj, small, cwx, cwbc, cbx, cbbc, dtb, nega, dsk, nw, jnp.asarray(expand), jnp.asarray(tril))


def _s5_scan_kernel(u_ref, bd_ref, w_ref, v_ref, a1_ref, a2_ref, y_ref, toep_ref, sc_ref, hin_ref, hc_ref, *, rows):
    R = rows

    @pl.when((pl.program_id(1) == 0) & (pl.program_id(2) == 0))
    def _():
        toep_ref[...] = jnp.zeros_like(toep_ref)
        for s in range(S5_CHUNK):
            for l in range(s, S5_CHUNK):
                toep_ref[s * LANES:(s + 1) * LANES, l * LANES:(l + 1) * LANES] = bd_ref[0, l - s]

    @pl.when(pl.program_id(2) == 0)
    def _():
        hc_ref[...] = jnp.zeros_like(hc_ref)

    parts = [u_ref[pl.ds(s, R, stride=S5_CHUNK), :].astype(BF16) for s in range(S5_CHUNK)]
    uf = jnp.concatenate(parts, axis=1)
    y_intra = jnp.dot(uf, toep_ref[...], preferred_element_type=F32)
    sc_ref[...] = jnp.dot(uf, w_ref[0], preferred_element_type=F32)
    a1 = a1_ref[0]
    a2 = a2_ref[0]
    half = S5_TILE_GROUPS * S5_STATE

    def step(r, h):
        hin_ref[pl.ds(r, 1), :] = h
        return a1 * h + a2 * pltpu.roll(h, half, 1) + sc_ref[pl.ds(r, 1), :]

    hc_ref[...] = lax.fori_loop(0, R, step, hc_ref[...])
    y = y_intra + jnp.dot(hin_ref[...].astype(BF16), v_ref[0], preferred_element_type=F32)
    for l in range(S5_CHUNK):
        y_ref[pl.ds(l, R, stride=S5_CHUNK), :] = y[:, l * LANES:(l + 1) * LANES]


def _s5_tables(lam_re, lam_im, log_dt, b_re, b_im, c_re, c_im):
    f32 = F32
    lc = S5_CHUNK
    tg = S5_TILE_GROUPS
    nt = S5_GROUPS // tg
    lr = jnp.minimum(lam_re.astype(f32), -1e-4)[None, :]
    li = lam_im.astype(f32)[None, :]
    dt = jnp.exp(log_dt.astype(f32))[:, None]
    mag = jnp.exp(lr * dt)
    ab_re, ab_im = mag * jnp.cos(li * dt), mag * jnp.sin(li * dt)
    den = lr * lr + li * li
    nr, ni = ab_re - 1.0, ab_im
    f_re, f_im = (nr * lr + ni * li) / den, (ni * lr - nr * li) / den
    b_re, b_im = b_re.astype(f32), b_im.astype(f32)
    bb_re = f_re[..., None] * b_re - f_im[..., None] * b_im
    bb_im = f_re[..., None] * b_im + f_im[..., None] * b_re
    tau = jnp.arange(lc + 1, dtype=f32)[:, None, None]
    pw_mag = jnp.exp(tau * (lr * dt)[None])
    pw_re = pw_mag * jnp.cos(tau * (li * dt)[None])
    pw_im = pw_mag * jnp.sin(tau * (li * dt)[None])
    c_re, c_im = c_re.astype(f32), c_im.astype(f32)
    cl_re = c_re[None] * pw_re[:, :, None, :] - c_im[None] * pw_im[:, :, None, :]
    cl_im = c_re[None] * pw_im[:, :, None, :] + c_im[None] * pw_re[:, :, None, :]
    hp = lax.Precision.HIGHEST
    kern = (jnp.einsum('tgop,gpi->tgoi', cl_re[:lc], bb_re, precision=hp)
            - jnp.einsum('tgop,gpi->tgoi', cl_im[:lc], bb_im, precision=hp))
    eye = jnp.eye(tg, dtype=f32)
    toep = jnp.einsum('tTgoi,gh->Ttgiho', kern.reshape(lc, nt, tg, S5_GROUP, S5_GROUP), eye)
    toep = toep.reshape(nt, lc, LANES, LANES)
    rev = pw_re[lc - 1 - np.arange(lc)], pw_im[lc - 1 - np.arange(lc)]
    wr = rev[0][..., None] * bb_re[None] - rev[1][..., None] * bb_im[None]
    wi = rev[0][..., None] * bb_im[None] + rev[1][..., None] * bb_re[None]
    wst = jnp.stack([wr, wi], 0).reshape(2, lc, nt, tg, S5_STATE, S5_GROUP)
    w_in = jnp.einsum('rsTgpi,gh->Tsgirhp', wst, eye).reshape(nt, lc * LANES, 2 * tg * S5_STATE)
    vst = jnp.stack([cl_re[1:], -cl_im[1:]], 0).reshape(2, lc, nt, tg, S5_GROUP, S5_STATE)
    v_out = jnp.einsum('rlTgop,gh->Trgplho', vst, eye).reshape(nt, 2 * tg * S5_STATE, lc * LANES)
    a_re = pw_re[lc].reshape(nt, 1, tg * S5_STATE)
    a_im = pw_im[lc].reshape(nt, 1, tg * S5_STATE)
    a1 = jnp.concatenate([a_re, a_re], -1)
    a2 = jnp.concatenate([-a_im, a_im], -1)
    return toep.astype(BF16), w_in.astype(BF16), v_out.astype(BF16), a1, a2


def _s5_scan(u, bsz, seq, tables, rows):
    toep, w_in, v_out, a1, a2 = tables
    nt = S5_GROUPS // S5_TILE_GROUPS
    lc = S5_CHUNK
    rows = min(rows, seq // lc)
    nblk = seq // (lc * rows)
    tok = rows * lc
    ns = 2 * S5_TILE_GROUPS * S5_STATE
    return pl.pallas_call(
        functools.partial(_s5_scan_kernel, rows=rows),
        grid=(nt, bsz, nblk),
        in_specs=[
            pl.BlockSpec((tok, LANES), lambda t, b, r: (b * nblk + r, t)),
            pl.BlockSpec((1, lc, LANES, LANES), lambda t, b, r: (t, 0, 0, 0)),
            pl.BlockSpec((1, lc * LANES, ns), lambda t, b, r: (t, 0, 0)),
            pl.BlockSpec((1, ns, lc * LANES), lambda t, b, r: (t, 0, 0)),
            pl.BlockSpec((1, 1, ns), lambda t, b, r: (t, 0, 0)),
            pl.BlockSpec((1, 1, ns), lambda t, b, r: (t, 0, 0)),
        ],
        out_specs=pl.BlockSpec((tok, LANES), lambda t, b, r: (b * nblk + r, t)),
        out_shape=jax.ShapeDtypeStruct((bsz * seq, S5_WIDTH), F32),
        scratch_shapes=[pltpu.VMEM((lc * LANES, lc * LANES), BF16), pltpu.VMEM((rows, ns), F32),
                        pltpu.VMEM((rows, ns), F32), pltpu.VMEM((1, ns), F32)],
        compiler_params=_cparams(("arbitrary", "arbitrary", "arbitrary")),
        name="s5_scan",
    )(u, toep, w_in, v_out, a1, a2)


def _s5_post_kernel(y_ref, u_ref, d_ref, w_ref, b_ref, o_ref):
    g = jax.nn.gelu(y_ref[...] + d_ref[...] * u_ref[...], approximate=True)
    gate = jnp.dot(g.astype(BF16), w_ref[...], preferred_element_type=F32) + b_ref[...]
    o_ref[...] = (g * jax.nn.sigmoid(gate)).astype(o_ref.dtype)


def _s5_post(y, u, d_skip, w_glu, b_glu, tm=512):
    t = y.shape[0]
    tm = min(tm, t)
    row = pl.BlockSpec((tm, S5_WIDTH), lambda i: (i, 0))
    vec = pl.BlockSpec((1, S5_WIDTH), lambda i: (0, 0))
    return pl.pallas_call(
        _s5_post_kernel,
        grid=(t // tm,),
        in_specs=[row, row, vec, pl.BlockSpec((S5_WIDTH, S5_WIDTH), lambda i: (0, 0)), vec],
        out_specs=row,
        out_shape=jax.ShapeDtypeStruct((t, S5_WIDTH), BF16),
        compiler_params=_cparams(("parallel",)),
        name="s5_post",
    )(y, u, d_skip.astype(F32)[None, :], w_glu.astype(BF16), b_glu.astype(F32)[None, :])


def _ret_kernel(q_ref, k_ref, v_ref, g_ref, cos_ref, sin_ref, intra_ref, dend_ref, osc_ref, nw_ref,
                o_ref, r_ref, *, chunk_decay):
    L = RET_CHUNK

    @pl.when(pl.program_id(1) == 0)
    def _():
        r_ref[...] = jnp.zeros_like(r_ref)

    cos = cos_ref[...]
    sin = sin_ref[...]
    lane = lax.broadcasted_iota(jnp.int32, (L, LANES), 1)
    rowi = lax.broadcasted_iota(jnp.int32, (LANES, L), 0)
    half = RET_QK_DIM // 2
    outs = []
    for t in range(RET_HEADS // 2):
        sl = slice(t * LANES, (t + 1) * LANES)
        qt = q_ref[:, sl].astype(F32)
        kt = k_ref[:, sl].astype(F32)
        qr = qt * cos + pltpu.roll(qt, 64, 1) * sin
        kr = (kt * cos + pltpu.roll(kt, 64, 1) * sin) * (RET_QK_DIM ** -0.5)
        kr_b = kr.astype(BF16)
        kr_t = kr.T
        for sub in range(2):
            h = 2 * t + sub
            vs = slice(h * RET_V_DIM, (h + 1) * RET_V_DIM)
            qm = jnp.where(((lane // half) % 2) == sub, qr, 0.0).astype(BF16)
            s = lax.dot_general(qm, kr_b, (((1,), (1,)), ((), ())), preferred_element_type=F32)
            p = (s * intra_ref[h]).astype(BF16)
            vh = v_ref[:, vs]
            state = r_ref[h]
            y_in = jnp.dot(p, vh, preferred_element_type=F32)
            y_off = jnp.dot(qm, state.astype(BF16), preferred_element_type=F32) * osc_ref[:, vs]
            ktm = jnp.where(((rowi // half) % 2) == sub, kr_t * dend_ref[h:h + 1, :], 0.0).astype(BF16)
            r_ref[h] = state * chunk_decay[h] + jnp.dot(ktm, vh, preferred_element_type=F32)
            y = y_in + y_off
            mu = jnp.mean(y, -1, keepdims=True)
            yc = y - mu
            var = jnp.mean(yc * yc, -1, keepdims=True)
            yn = yc * lax.rsqrt(var + LN_EPS) * nw_ref[:, vs]
            outs.append(_silu(g_ref[:, vs].astype(F32)) * yn)
    o_ref[...] = jnp.concatenate(outs, axis=1).astype(o_ref.dtype)


def _ret_mixer(proj, bsz, seq, positions, norm_w):
    L = RET_CHUNK
    nc = seq // L
    half = RET_QK_DIM // 2
    inv_freq = 1.0 / (ROPE_BASE ** (jnp.arange(half, dtype=F32) / half))
    ang = positions.astype(F32).reshape(bsz * seq, 1) * inv_freq[None, :]
    cos32, sin32 = jnp.cos(ang), jnp.sin(ang)
    cos_t = jnp.concatenate([cos32] * 4, axis=1)
    sin_t = jnp.concatenate([-sin32, -sin32, sin32, sin32], axis=1)
    log_gamma = np.log1p(-np.exp2(-5.0 - np.arange(RET_HEADS, dtype=np.float64)))
    pos = np.arange(L, dtype=np.float64)
    rel = pos[:, None] - pos[None, :]
    intra = np.where(rel[None] >= 0, np.exp(rel[None] * log_gamma[:, None, None]), 0.0).astype(np.float32)
    dend = np.exp((L - 1 - pos)[None, :] * log_gamma[:, None]).astype(np.float32)
    osc = np.repeat(np.exp((pos + 1)[:, None] * log_gamma[None, :]), RET_V_DIM, axis=1).astype(np.float32)
    chunk_decay = tuple(float(v) for v in np.exp(L * log_gamma))
    rows = lambda b, c: b * nc + c
    full = lambda shape: pl.BlockSpec(shape, lambda b, c: (0,) * len(shape))
    return pl.pallas_call(
        functools.partial(_ret_kernel, chunk_decay=chunk_decay),
        grid=(bsz, nc),
        in_specs=[
            pl.BlockSpec((L, 512), lambda b, c: (rows(b, c), _CB_RQ)),
            pl.BlockSpec((L, 512), lambda b, c: (rows(b, c), _CB_RK)),
            pl.BlockSpec((L, 1024), lambda b, c: (rows(b, c), _CB_RV)),
            pl.BlockSpec((L, 1024), lambda b, c: (rows(b, c), _CB_RG)),
            pl.BlockSpec((L, LANES), lambda b, c: (rows(b, c), 0)),
            pl.BlockSpec((L, LANES), lambda b, c: (rows(b, c), 0)),
            full((RET_HEADS, L, L)), full((RET_HEADS, L)), full((L, RET_WIDTH)), full((1, RET_WIDTH)),
        ],
        out_specs=pl.BlockSpec((L, RET_WIDTH), lambda b, c: (rows(b, c), 0)),
        out_shape=jax.ShapeDtypeStruct((bsz * seq, RET_WIDTH), BF16),
        scratch_shapes=[pltpu.VMEM((RET_HEADS, LANES, RET_V_DIM), F32)],
        compiler_params=_cparams(("parallel", "arbitrary")),
        name="retention_mixer",
    )(proj, proj, proj, proj, cos_t, sin_t, jnp.asarray(intra), jnp.asarray(dend), jnp.asarray(osc),
      norm_w.astype(F32)[None, :])


def _gla_kernel(q_ref, k_ref, v_ref, r_ref, sm_ref, wa_ref, ba_ref, tblk_ref, nw_ref, o_ref, s_ref):
    LB = GLA_BLOCK
    LC = GLA_CHUNK

    @pl.when(pl.program_id(1) == 0)
    def _():
        s_ref[...] = jnp.zeros_like(s_ref)

    logit = jnp.dot(sm_ref[...], wa_ref[...], preferred_element_type=F32, precision=HIGHEST) + ba_ref[...]
    la = (jnp.minimum(logit, 0.0) - jnp.log1p(jnp.exp(-jnp.abs(logit)))) * (1.0 / GLA_TAU)
    b = jnp.dot(tblk_ref[...], la, preferred_element_type=F32, precision=HIGHEST)
    rowf = lax.broadcasted_iota(jnp.int32, (LB, GLA_QK), 0)
    b_end = jnp.where(rowf < LC, b[LC - 1:LC], b[LB - 1:LB])
    q = q_ref[...].astype(F32) * (GLA_QK_DIM ** -0.5)
    k = k_ref[...].astype(F32)
    q_dec = q * jnp.exp(b)
    k_inv = (k * jnp.exp(-b)).astype(BF16)
    k_dec = k * jnp.exp(b_end - b)
    row = lax.broadcasted_iota(jnp.int32, (LB, LB), 0)
    col = lax.broadcasted_iota(jnp.int32, (LB, LB), 1)
    blk_causal = (row >= col) & ((row // LC) == (col // LC))
    first_rows = row < LC
    first_vrows = lax.broadcasted_iota(jnp.int32, (LB, GLA_V_DIM), 0) < LC
    outs = []
    for h in range(GLA_HEADS):
        sl = slice(h * GLA_QK_DIM, (h + 1) * GLA_QK_DIM)
        vs = slice(h * GLA_V_DIM, (h + 1) * GLA_V_DIM)
        qd = q_dec[:, sl]
        qd_b = qd.astype(BF16)
        att = lax.dot_general(qd_b, k_inv[:, sl], (((1,), (1,)), ((), ())), preferred_element_type=F32)
        att = jnp.where(blk_causal, att, 0.0).astype(BF16)
        vh = v_ref[:, vs]
        y = jnp.dot(att, vh, preferred_element_type=F32)
        s0 = s_ref[h]
        kd_t = k_dec[:, sl].T.astype(BF16)
        kv0 = jnp.dot(kd_t, jnp.where(first_vrows, vh, jnp.zeros_like(vh)), preferred_element_type=F32)
        kv1 = jnp.dot(kd_t, jnp.where(first_vrows, jnp.zeros_like(vh), vh), preferred_element_type=F32)
        b_t = b[:, sl].T
        s1 = s0 * jnp.exp(b_t[:, LC - 1:LC]) + kv0
        s_ref[h] = s1 * jnp.exp(b_t[:, LB - 1:LB]) + kv1
        y = y + jnp.dot(jnp.where(first_rows, qd, 0.0).astype(BF16), s0.astype(BF16), preferred_element_type=F32)
        y = y + jnp.dot(jnp.where(first_rows, 0.0, qd).astype(BF16), s1.astype(BF16), preferred_element_type=F32)
        ms = jnp.mean(y * y, -1, keepdims=True)
        yn = y * lax.rsqrt(ms + RMS_EPS) * nw_ref[:, vs]
        outs.append(yn * _silu(r_ref[:, vs].astype(F32)))
    o_ref[...] = jnp.concatenate(outs, axis=1).astype(o_ref.dtype)


def _gla_mixer(proj, small, bsz, seq, w_alpha, b_alpha, norm_w):
    LB = GLA_BLOCK
    nb = seq // LB
    wa = jnp.zeros((LANES, GLA_QK), F32).at[SSD_HEADS:SSD_HEADS + GLA_GATE_RANK].set(w_alpha.astype(F32))
    idx = np.arange(LB)
    tblk = ((idx[:, None] >= idx[None, :]) & ((idx[:, None] // GLA_CHUNK) == (idx[None, :] // GLA_CHUNK)))
    rows = lambda b, c: b * nb + c
    full = lambda shape: pl.BlockSpec(shape, lambda b, c: (0,) * len(shape))
    return pl.pallas_call(
        _gla_kernel,
        grid=(bsz, nb),
        in_specs=[
            pl.BlockSpec((LB, 512), lambda b, c: (rows(b, c), _CB_GQ)),
            pl.BlockSpec((LB, 512), lambda b, c: (rows(b, c), _CB_GK)),
            pl.BlockSpec((LB, 1024), lambda b, c: (rows(b, c), _CB_GV)),
            pl.BlockSpec((LB, 1024), lambda b, c: (rows(b, c), _CB_GR)),
            pl.BlockSpec((LB, LANES), lambda b, c: (rows(b, c), 0)),
            full((LANES, GLA_QK)), full((1, GLA_QK)), full((LB, LB)), full((1, GLA_WIDTH)),
        ],
        out_specs=pl.BlockSpec((LB, GLA_WIDTH), lambda b, c: (rows(b, c), 0)),
        out_shape=jax.ShapeDtypeStruct((bsz * seq, GLA_WIDTH), BF16),
        scratch_shapes=[pltpu.VMEM((GLA_HEADS, GLA_QK_DIM, GLA_V_DIM), F32)],
        compiler_params=_cparams(("parallel", "arbitrary")),
        name="gla_mixer",
    )(proj, proj, proj, proj, small, wa, b_alpha.astype(F32)[None, :], jnp.asarray(tblk.astype(np.float32)),
      norm_w.astype(F32)[None, :])


def _merge_kernel(ya_ref, yb_ref, yc_ref, yd_ref, g0_ref, g1_ref, g2_ref, g3_ref, wb_ref, o_ref):
    acc = None
    for n, (y_ref, g_ref) in enumerate(((ya_ref, g0_ref), (yb_ref, g1_ref), (yc_ref, g2_ref), (yd_ref, g3_ref))):
        br = jnp.dot(y_ref[...], wb_ref[n], preferred_element_type=F32)
        term = jax.nn.sigmoid(g_ref[...].astype(F32)) * br
        acc = term if acc is None else acc + term
    o_ref[...] = acc.astype(o_ref.dtype)


def _merge(ys, proj, w_branch, tm=1024, tn=512):
    t = ys[0].shape[0]
    tm = min(tm, t)
    nj = D_MODEL // tn
    yspec = pl.BlockSpec((tm, BRANCH_WIDTH), lambda i, j: (i, 0))
    gspec = lambda n: pl.BlockSpec((tm, tn), lambda i, j: (i, n * nj + j))
    return pl.pallas_call(
        _merge_kernel,
        grid=(t // tm, nj),
        in_specs=[yspec] * 4 + [gspec(n) for n in range(N_BRANCH)]
                 + [pl.BlockSpec((N_BRANCH, BRANCH_WIDTH, tn), lambda i, j: (0, 0, j))],
        out_specs=pl.BlockSpec((tm, tn), lambda i, j: (i, j)),
        out_shape=jax.ShapeDtypeStruct((t, D_MODEL), BF16),
        compiler_params=_cparams(("parallel", "parallel")),
        name="branch_merge",
    )(*ys, proj, proj, proj, proj, w_branch.astype(BF16))


def _proj_ln_kernel(m_ref, w_ref, h_ref, lw_ref, lb_ref, o_ref, op_ref):
    mix = jnp.dot(m_ref[...], w_ref[...], preferred_element_type=F32)
    out = _layer_norm(DEEPNORM_ALPHA * h_ref[...] + mix, lw_ref[...], lb_ref[...])
    o_ref[...] = out
    op_ref[...] = _pack_rows(out.astype(BF16))


def _proj_ln(merged, w_out, h, ln_w, ln_b, tm=512):
    t = h.shape[0]
    tm = min(tm, t)
    row = pl.BlockSpec((tm, D_MODEL), lambda i: (i, 0))
    vec = pl.BlockSpec((1, D_MODEL), lambda i: (0, 0))
    return pl.pallas_call(
        _proj_ln_kernel,
        grid=(t // tm,),
        in_specs=[row, pl.BlockSpec((D_MODEL, D_MODEL), lambda i: (0, 0)), row, vec, vec],
        out_specs=[row, pl.BlockSpec((tm, D_MODEL // 2), lambda i: (i, 0))],
        out_shape=[jax.ShapeDtypeStruct((t, D_MODEL), F32), jax.ShapeDtypeStruct((t, D_MODEL // 2), jnp.uint32)],
        compiler_params=_cparams(("parallel",)),
        name="out_proj_ln",
    )(merged, w_out.astype(BF16), h, ln_w.astype(F32)[None, :], ln_b.astype(F32)[None, :])


def _router_kernel(h_ref, w_ref, b_ref, idx_ref, gate_ref, cnt_ref, run_ref):
    @pl.when(pl.program_id(0) == 0)
    def _():
        run_ref[...] = jnp.zeros_like(run_ref)

    logits = _dot3(w_ref[...], h_ref[...], ((1,), (1,))) + b_ref[...]
    eid = lax.broadcasted_iota(jnp.int32, logits.shape, 0)
    vals = logits
    run = run_ref[...]
    top_v, top_i = [], []
    for _ in range(TOP_K):
        m = jnp.max(vals, axis=0, keepdims=True)
        sel = jnp.min(jnp.where(vals == m, eid, N_EXPERTS), axis=0, keepdims=True)
        hit = eid == sel
        top_v.append(m)
        top_i.append(sel)
        vals = jnp.where(hit, -jnp.inf, vals)
        run = run + jnp.sum(jnp.where(hit, 1.0, 0.0), axis=1, keepdims=True)
    run_ref[...] = run
    ex = [jnp.exp(v - top_v[0]) for v in top_v]
    den = ex[0] + ex[1] + ex[2] + ex[3]
    zi = jnp.zeros_like(top_i[0])
    zf = jnp.zeros_like(den)
    idx_ref[...] = jnp.concatenate(top_i + [zi] * (8 - TOP_K), axis=0)
    gate_ref[...] = jnp.concatenate([e / den for e in ex] + [zf] * (8 - TOP_K), axis=0)
    cnt_ref[...] = jnp.broadcast_to(run, cnt_ref.shape).astype(jnp.int32)


def _router(h, router_w, router_b, tm=512):
    t = h.shape[0]
    tm = min(tm, t)
    tok = pl.BlockSpec((8, tm), lambda i: (0, i))
    return pl.pallas_call(
        _router_kernel,
        grid=(t // tm,),
        in_specs=[pl.BlockSpec((tm, D_MODEL), lambda i: (i, 0)),
                  pl.BlockSpec((N_EXPERTS, D_MODEL), lambda i: (0, 0)),
                  pl.BlockSpec((N_EXPERTS, 1), lambda i: (0, 0))],
        out_specs=[tok, tok, pl.BlockSpec((N_EXPERTS, LANES), lambda i: (0, 0))],
        out_shape=[jax.ShapeDtypeStruct((8, t), jnp.int32), jax.ShapeDtypeStruct((8, t), F32),
                   jax.ShapeDtypeStruct((N_EXPERTS, LANES), jnp.int32)],
        scratch_shapes=[pltpu.VMEM((N_EXPERTS, 1), F32)],
        compiler_params=_cparams(("arbitrary",)),
        name="router_topk",
    )(h, router_w.astype(F32).T, router_b.astype(F32)[:, None])


def _pack_rows(x):
    n = x.shape[1] // 2
    lo = pltpu.bitcast(x[:, :n].astype(F32), jnp.uint32) >> 16
    hi = pltpu.bitcast(x[:, n:].astype(F32), jnp.uint32)
    return lo | hi


def _unpack_rows(w):
    lo = pltpu.bitcast(w << 16, F32)
    hi = pltpu.bitcast(w & jnp.uint32(0xFFFF0000), F32)
    return lo, hi


def _expert_kernel(be_ref, nb_ref, tab_ref, hp_ref, wgu_ref, bgu_ref, wd_ref, bd_ref, y_ref,
                   idx_ref, xbuf_ref, ybuf_ref, isem, gsem, ssem):
    i = pl.program_id(0)
    n_used = nb_ref[0]
    blk = MOE_BLOCK

    def idx_copy(b):
        return pltpu.make_async_copy(tab_ref.at[b], idx_ref.at[b % 4], isem.at[b % 4])

    def gather_start(b, r):
        pltpu.make_async_copy(hp_ref.at[pl.ds(idx_ref[b % 4, r], 1)], xbuf_ref.at[b % 2, pl.ds(r, 1)],
                              gsem.at[b % 2]).start()

    def scatter_start(b, r):
        pltpu.make_async_copy(ybuf_ref.at[b % 2, pl.ds(r, 1)], y_ref.at[pl.ds(idx_ref[b % 4, blk + r], 1)],
                              ssem.at[b % 2]).start()

    def gather_wait(b):
        pltpu.make_async_copy(hp_ref.at[pl.ds(0, blk)], xbuf_ref.at[b % 2], gsem.at[b % 2]).wait()

    def scatter_wait(b):
        pltpu.make_async_copy(ybuf_ref.at[b % 2], y_ref.at[pl.ds(0, blk)], ssem.at[b % 2]).wait()

    def ffn(b):
        lo, hi = _unpack_rows(xbuf_ref[b % 2])
        xb = jnp.concatenate([lo.astype(BF16), hi.astype(BF16)], axis=1)
        hgu = jnp.dot(xb, wgu_ref[0], preferred_element_type=F32) + bgu_ref[0]
        g = jnp.minimum(hgu[:, :D_EXPERT], SWIGLU_LIMIT)
        u = jnp.clip(hgu[:, D_EXPERT:], -SWIGLU_LIMIT, SWIGLU_LIMIT)
        act = (u + 1.0) * (g * jax.nn.sigmoid(SWIGLU_ALPHA * g))
        out = jnp.dot(act.astype(BF16), wd_ref[0], preferred_element_type=F32) + bd_ref[0]
        ybuf_ref[b % 2] = _pack_rows(out.astype(BF16))

    @pl.when(i == 0)
    def _():
        idx_copy(0).start()
        ybuf_ref[1] = jnp.zeros((blk, ybuf_ref.shape[2]), ybuf_ref.dtype)
        spare = pltpu.make_async_copy(ybuf_ref.at[1], y_ref.at[pl.ds(y_ref.shape[0] - blk, blk)], ssem.at[1])
        spare.start()
        idx_copy(0).wait()
        lax.fori_loop(0, blk, lambda r, c: (gather_start(0, r), c)[1], 0, unroll=8)
        idx_copy(1).start()
        spare.wait()

    @pl.when((i >= 2) & (i <= n_used))
    def _():
        scatter_wait(i - 2)

    @pl.when(i <= n_used)
    def _():
        idx_copy(i + 1).wait()
        gather_wait(i)

    @pl.when(i == 0)
    def _():
        for r in range(blk):
            gather_start(1, r)
        ffn(0)

    @pl.when((i > 0) & (i < n_used))
    def _():
        for r in range(blk):
            gather_start(i + 1, r)
            scatter_start(i - 1, r)
        ffn(i)

    @pl.when(i < n_used)
    def _():
        idx_copy(i + 2).start()

    @pl.when(i == n_used)
    def _():
        lax.fori_loop(0, blk, lambda r, c: (scatter_start(i - 1, r), c)[1], 0, unroll=8)
        scatter_wait(i - 1)


def _expert_params(w_gate_up, b_gate_up, w_down, b_down):
    n = w_gate_up.shape[0] * N_EXPERTS
    return (w_gate_up.astype(BF16).reshape(n, D_MODEL, 2 * D_EXPERT), b_gate_up.astype(F32).reshape(n, 1, 2 * D_EXPERT),
            w_down.astype(BF16).reshape(n, D_EXPERT, D_MODEL), b_down.astype(F32).reshape(n, 1, D_MODEL))


def _experts(h_packed, table, block_expert, n_used, expert_params):
    w_gate_up, b_gate_up, w_down, b_down = expert_params
    blk = MOE_BLOCK
    n_blocks = block_expert.shape[0]
    n_out = TOP_K * h_packed.shape[0] + blk
    half = D_MODEL // 2
    grid_spec = pltpu.PrefetchScalarGridSpec(
        num_scalar_prefetch=2,
        grid=(n_blocks,),
        in_specs=[
            pl.BlockSpec(memory_space=pl.ANY),
            pl.BlockSpec(memory_space=pl.ANY),
            pl.BlockSpec((1, D_MODEL, 2 * D_EXPERT), lambda i, be, nb: (be[i], 0, 0)),
            pl.BlockSpec((1, 1, 2 * D_EXPERT), lambda i, be, nb: (be[i], 0, 0)),
            pl.BlockSpec((1, D_EXPERT, D_MODEL), lambda i, be, nb: (be[i], 0, 0)),
            pl.BlockSpec((1, 1, D_MODEL), lambda i, be, nb: (be[i], 0, 0)),
        ],
        out_specs=pl.BlockSpec(memory_space=pl.ANY),
        scratch_shapes=[pltpu.SMEM((4, 2 * blk), jnp.int32), pltpu.VMEM((2, blk, half), jnp.uint32),
                        pltpu.VMEM((2, blk, half), jnp.uint32), pltpu.SemaphoreType.DMA((4,)),
                        pltpu.SemaphoreType.DMA((2,)), pltpu.SemaphoreType.DMA((2,))],
    )
    return pl.pallas_call(
        _expert_kernel,
        grid_spec=grid_spec,
        out_shape=jax.ShapeDtypeStruct((n_out, half), jnp.uint32),
        compiler_params=_cparams(("arbitrary",)),
        name="expert_ffn",
    )(block_expert, n_used, table, h_packed, w_gate_up, b_gate_up, w_down, b_down)


def _combine_kernel(y0_ref, y1_ref, y2_ref, y3_ref, gate_ref, h_ref, lw_ref, lb_ref, o_ref, ob_ref, *, tm):
    rows = min(tm, 16)
    for r0 in range(0, tm, rows):
        rs = slice(r0, r0 + rows)
        gates = gate_ref[rs, :]
        ffn = None
        for k, y_ref in enumerate((y0_ref, y1_ref, y2_ref, y3_ref)):
            lo, hi = _unpack_rows(y_ref[rs, :])
            term = gates[:, k:k + 1] * jnp.concatenate([lo, hi], axis=1)
            ffn = term if ffn is None else ffn + term
        out = _layer_norm(DEEPNORM_ALPHA * h_ref[rs, :] + ffn, lw_ref[...], lb_ref[...])
        o_ref[rs, :] = out
        ob_ref[rs, :] = out.astype(BF16)


def _combine(y, gates, h, ln_w, ln_b, tm=512):
    t = h.shape[0]
    tm = min(tm, t)
    nt = t // tm
    row = pl.BlockSpec((tm, D_MODEL), lambda i: (i, 0))
    vec = pl.BlockSpec((1, D_MODEL), lambda i: (0, 0))
    yspec = lambda k: pl.BlockSpec((tm, D_MODEL // 2), lambda i: (k * nt + i, 0))
    return pl.pallas_call(
        functools.partial(_combine_kernel, tm=tm),
        grid=(nt,),
        in_specs=[yspec(k) for k in range(TOP_K)] + [pl.BlockSpec((tm, 8), lambda i: (i, 0)), row, vec, vec],
        out_specs=[row, row],
        out_shape=[jax.ShapeDtypeStruct((t, D_MODEL), F32), jax.ShapeDtypeStruct((t, D_MODEL), BF16)],
        compiler_params=_cparams(("parallel",)),
        name="moe_combine_ln",
    )(y, y, y, y, gates, h, ln_w.astype(F32)[None, :], ln_b.astype(F32)[None, :])


def _moe(h, h_packed, router_w, router_b, expert_params, layer, ln_w, ln_b):
    t = h.shape[0]
    blk = MOE_BLOCK
    top_idx, top_gate, counts = _router(h, router_w, router_b)
    counts = counts[:, 0]
    n_assign = TOP_K * t
    n_blocks = n_assign // blk + N_EXPERTS
    a_ids = jnp.arange(n_assign, dtype=jnp.int32)
    a_sorted = jnp.sort(top_idx[:TOP_K].reshape(-1) * n_assign + a_ids) % n_assign
    padded = (counts + blk - 1) // blk * blk
    pad_end = jnp.cumsum(padded)
    grp_end = jnp.cumsum(counts)
    blocks = jnp.arange(n_blocks, dtype=jnp.int32)
    block_expert = jnp.sum(((pad_end // blk)[None, :] <= blocks[:, None]).astype(jnp.int32), axis=1)
    block_expert = jnp.minimum(block_expert, N_EXPERTS - 1)
    n_used = (pad_end[-1] // blk).astype(jnp.int32).reshape(1)
    is_e = block_expert[:, None] == jnp.arange(N_EXPERTS, dtype=jnp.int32)[None, :]
    pick = lambda v: jnp.sum(jnp.where(is_e, v[None, :], 0), axis=1)
    offset = blocks * blk - pick(pad_end - padded)
    first = pick(grp_end - counts) + offset
    n_valid = jnp.clip(pick(counts) - offset, 0, blk)
    j = jnp.arange(blk, dtype=jnp.int32)[None, :]
    valid = j < n_valid[:, None]
    a_rows = a_sorted[jnp.clip(first[:, None] + j, 0, n_assign - 1)]
    tok_rows = jnp.where(valid, a_rows % t, 0)
    out_rows = jnp.where(valid, a_rows, n_assign + j)
    table = jnp.concatenate([tok_rows, out_rows], axis=1).astype(jnp.int32)
    table = jnp.concatenate([table, jnp.zeros((2, 2 * blk), jnp.int32)], axis=0)
    y = _experts(h_packed, table, block_expert + layer * N_EXPERTS, n_used, expert_params)
    return _combine(y, top_gate.T, h, ln_w, ln_b)


def _ple_kernel(hb_ref, h_ref, p_ref, wg_ref, wp_ref, lw_ref, lb_ref, o_ref, ob_ref):
    gate = jax.nn.sigmoid(jnp.dot(hb_ref[...], wg_ref[...], preferred_element_type=F32))
    emb = jnp.dot(p_ref[...].astype(BF16), wp_ref[...], preferred_element_type=F32)
    out = _layer_norm(DEEPNORM_ALPHA * h_ref[...] + gate * emb, lw_ref[...], lb_ref[...])
    o_ref[...] = out
    ob_ref[...] = out.astype(BF16)


def _ple(hb, h, p_i, w_gate, w_proj, ln_w, ln_b, tm=512):
    t = h.shape[0]
    tm = min(tm, t)
    row = pl.BlockSpec((tm, D_MODEL), lambda i: (i, 0))
    vec = pl.BlockSpec((1, D_MODEL), lambda i: (0, 0))
    return pl.pallas_call(
        _ple_kernel,
        grid=(t // tm,),
        in_specs=[row, row, pl.BlockSpec((tm, PLE_DIM), lambda i: (i, 0)),
                  pl.BlockSpec((D_MODEL, D_MODEL), lambda i: (0, 0)),
                  pl.BlockSpec((PLE_DIM, D_MODEL), lambda i: (0, 0)), vec, vec],
        out_specs=[row, row],
        out_shape=[jax.ShapeDtypeStruct((t, D_MODEL), F32), jax.ShapeDtypeStruct((t, D_MODEL), BF16)],
        compiler_params=_cparams(("parallel",)),
        name="ple_ln",
    )(hb, h, p_i, w_gate.astype(BF16), w_proj.astype(BF16), ln_w.astype(F32)[None, :], ln_b.astype(F32)[None, :])


def _token_mixer(h, hb, bsz, seq, positions, w_in, ssd_conv_w, ssd_conv_b, ssd_dt_bias, ssd_a_log, ssd_d,
                 ssd_norm_w, s5_lambda_re, s5_lambda_im, s5_log_dt, s5_b_re, s5_b_im, s5_c_re, s5_c_im, s5_d,
                 s5_w_glu, s5_b_glu, ret_norm_w, gla_w_alpha, gla_b_alpha, gla_norm_w, w_branch, s5_rows):
    w_main = _main_weight(w_in)
    w_u = w_in[:, _OFF_U:_OFF_U + S5_WIDTH].astype(BF16)
    w_small = jnp.concatenate([w_in[:, _OFF_DT:_OFF_DT + SSD_HEADS], w_in[:, _OFF_GC:_OFF_GC + GLA_GATE_RANK],
                               jnp.zeros((D_MODEL, LANES - SSD_HEADS - GLA_GATE_RANK), w_in.dtype)], axis=1)
    proj = _matmul(hb, w_main, BF16, 1024, 1536, name="in_proj_main")
    u = _matmul(hb, w_u, F32, 1024, 1024, name="in_proj_s5")
    small = _matmul_f32(h, w_small.astype(F32), 512, name="in_proj_small")
    y_a = _ssd_mixer(proj, small, bsz, seq, ssd_conv_w, ssd_conv_b, ssd_dt_bias, ssd_a_log, ssd_d, ssd_norm_w)
    tables = _s5_tables(s5_lambda_re, s5_lambda_im, s5_log_dt, s5_b_re, s5_b_im, s5_c_re, s5_c_im)
    y_b = _s5_post(_s5_scan(u, bsz, seq, tables, s5_rows), u, s5_d, s5_w_glu, s5_b_glu)
    y_c = _ret_mixer(proj, bsz, seq, positions, ret_norm_w)
    y_d = _gla_mixer(proj, small, bsz, seq, gla_w_alpha, gla_b_alpha, gla_norm_w)
    return _merge((y_a, y_b, y_c, y_d), proj, w_branch)


def kernel(x, p, positions, w_in, ssd_conv_w, ssd_conv_b, ssd_dt_bias, ssd_a_log, ssd_d, ssd_norm_w, s5_lambda_re, s5_lambda_im, s5_log_dt, s5_b_re, s5_b_im, s5_c_re, s5_c_im, s5_d, s5_w_glu, s5_b_glu, ret_norm_w, gla_w_alpha, gla_b_alpha, gla_norm_w, w_branch, w_out, ln1_w, ln1_b, router_w, router_b, moe_w_gate_up, moe_b_gate_up, moe_w_down, moe_b_down, ln2_w, ln2_b, ple_w_gate, ple_w_proj, ln3_w, ln3_b):
    s5_rows = 256
    bsz, seq, d = x.shape
    t = bsz * seq
    h = x.reshape(t, d).astype(F32)
    hb = h.astype(BF16)
    expert_params = _expert_params(moe_w_gate_up, moe_b_gate_up, moe_w_down, moe_b_down)
    for i in range(DEPTH):
        merged = _token_mixer(h, hb, bsz, seq, positions, w_in[i], ssd_conv_w[i], ssd_conv_b[i], ssd_dt_bias[i],
                              ssd_a_log[i], ssd_d[i], ssd_norm_w[i], s5_lambda_re[i], s5_lambda_im[i],
                              s5_log_dt[i], s5_b_re[i], s5_b_im[i], s5_c_re[i], s5_c_im[i], s5_d[i],
                              s5_w_glu[i], s5_b_glu[i], ret_norm_w[i], gla_w_alpha[i], gla_b_alpha[i],
                              gla_norm_w[i], w_branch[i], s5_rows)
        h, h_packed = _proj_ln(merged, w_out[i], h, ln1_w[i], ln1_b[i])
        h, hb = _moe(h, h_packed, router_w[i], router_b[i], expert_params, i, ln2_w[i], ln2_b[i])
        h, hb = _ple(hb, h, p[i].reshape(t, PLE_DIM), ple_w_gate[i], ple_w_proj[i], ln3_w[i], ln3_b[i])
    return h.reshape(bsz, seq, d).astype(x.dtype)
```

```python
import functools
import math

import numpy as np
import jax
import jax.numpy as jnp
from jax import lax
from jax.experimental import pallas as pl
from jax.experimental.pallas import tpu as pltpu

F32 = jnp.float32
BF16 = jnp.bfloat16
HIGHEST = lax.Precision.HIGHEST

D_MODEL = 2048
DEPTH = 2
PLE_DIM = 256
N_BRANCH = 4
BRANCH_WIDTH = 1024

SSD_HEADS = 16
SSD_HEAD_DIM = 64
SSD_INNER = SSD_HEADS * SSD_HEAD_DIM
SSD_GROUPS = 2
SSD_STATE = 128
SSD_CONV = 4
SSD_CHUNK = 128

S5_WIDTH = 1024
S5_GROUP = 16
S5_GROUPS = S5_WIDTH // S5_GROUP
S5_STATE = 64
S5_CHUNK = 16
S5_TILE_GROUPS = 8

RET_HEADS = 8
RET_QK_DIM = 64
RET_V_DIM = 128
RET_QK = RET_HEADS * RET_QK_DIM
RET_WIDTH = RET_HEADS * RET_V_DIM
RET_CHUNK = 128
ROPE_BASE = 10000.0

GLA_HEADS = 4
GLA_QK_DIM = 128
GLA_V_DIM = 256
GLA_QK = GLA_HEADS * GLA_QK_DIM
GLA_WIDTH = GLA_HEADS * GLA_V_DIM
GLA_GATE_RANK = 16
GLA_TAU = 16.0
GLA_CHUNK = 64
GLA_BLOCK = 128

N_EXPERTS = 32
TOP_K = 4
D_EXPERT = 1024
SWIGLU_ALPHA = 1.702
SWIGLU_LIMIT = 7.0
MOE_BLOCK = 512

DEEPNORM_ALPHA = (2 * DEPTH) ** 0.25
LN_EPS = 1e-5
RMS_EPS = 1e-6

LANES = 128
VMEM_LIMIT = 56 * 1024 * 1024

_OFF_Z = 0
_OFF_XS = 1024
_OFF_BC = 2048
_OFF_DT = 2560
_OFF_U = 2576
_OFF_RQ = 3600
_OFF_RK = 4112
_OFF_RV = 4624
_OFF_RG = 5648
_OFF_GQ = 6672
_OFF_GK = 7184
_OFF_GV = 7696
_OFF_GR = 8720
_OFF_GC = 9744
_OFF_GATE = 9760

_CB_Z, _CB_XS, _CB_RV, _CB_RG, _CB_GV, _CB_GR = 8, 9, 10, 11, 12, 13
_CB_BC, _CB_RQ, _CB_RK, _CB_GQ, _CB_GK = 28, 29, 30, 31, 32
N_MAIN = 16896


def _main_weight(w_in):
    def cols(off, n):
        return w_in[:, off:off + n].astype(BF16)

    def rotary(off):
        half = RET_QK_DIM // 2
        w = cols(off, RET_QK).reshape(D_MODEL, RET_HEADS // 2, 2, 2, half)
        return w.transpose(0, 1, 3, 2, 4).reshape(D_MODEL, RET_QK)

    return jnp.concatenate([
        cols(_OFF_GATE, N_BRANCH * D_MODEL), cols(_OFF_Z, 1024), cols(_OFF_XS, 1024), cols(_OFF_RV, 1024),
        cols(_OFF_RG, 1024), cols(_OFF_GV, 1024), cols(_OFF_GR, 1024), cols(_OFF_BC, 512),
        rotary(_OFF_RQ), rotary(_OFF_RK), cols(_OFF_GQ, 512), cols(_OFF_GK, 512)], axis=1)


def _cparams(sem):
    return pltpu.CompilerParams(dimension_semantics=sem, vmem_limit_bytes=VMEM_LIMIT)


def _silu(x):
    return x * jax.nn.sigmoid(x)


def _softplus(x):
    return jnp.maximum(x, 0.0) + jnp.log1p(jnp.exp(-jnp.abs(x)))


def _layer_norm(x, w, b):
    mu = jnp.mean(x, -1, keepdims=True)
    xc = x - mu
    var = jnp.mean(xc * xc, -1, keepdims=True)
    return xc * lax.rsqrt(var + LN_EPS) * w + b


def _mm_kernel(a_ref, b_ref, o_ref):
    o_ref[...] = jnp.dot(a_ref[...], b_ref[...], preferred_element_type=F32).astype(o_ref.dtype)


def _matmul(a, b, out_dtype, tm, tn, name="matmul"):
    m, k = a.shape
    n = b.shape[1]
    tm, tn = min(tm, m), min(tn, n)
    return pl.pallas_call(
        _mm_kernel,
        grid=(m // tm, n // tn),
        in_specs=[pl.BlockSpec((tm, k), lambda i, j: (i, 0)),
                  pl.BlockSpec((k, tn), lambda i, j: (0, j))],
        out_specs=pl.BlockSpec((tm, tn), lambda i, j: (i, j)),
        out_shape=jax.ShapeDtypeStruct((m, n), out_dtype),
        compiler_params=_cparams(("parallel", "parallel")),
        name=name,
    )(a, b)


def _split_bf16(x):
    hi = x.astype(BF16)
    return hi, (x - hi.astype(F32)).astype(BF16)


def _dot3(a, b, dims):
    a_hi, a_lo = _split_bf16(a)
    b_hi, b_lo = _split_bf16(b)
    dg = functools.partial(lax.dot_general, dimension_numbers=(dims, ((), ())), preferred_element_type=F32)
    return dg(a_hi, b_hi) + (dg(a_hi, b_lo) + dg(a_lo, b_hi))


def _mm3_kernel(a_ref, b_ref, o_ref):
    o_ref[...] = _dot3(a_ref[...], b_ref[...], ((1,), (0,)))


def _matmul_f32(a, b, tm, name):
    m, k = a.shape
    n = b.shape[1]
    tm = min(tm, m)
    return pl.pallas_call(
        _mm3_kernel,
        grid=(m // tm,),
        in_specs=[pl.BlockSpec((tm, k), lambda i: (i, 0)), pl.BlockSpec((k, n), lambda i: (0, 0))],
        out_specs=pl.BlockSpec((tm, n), lambda i: (i, 0)),
        out_shape=jax.ShapeDtypeStruct((m, n), F32),
        compiler_params=_cparams(("parallel",)),
        name=name,
    )(a, b)


def _causal_conv(x, tail, w, b):
    n_tap = SSD_CONV
    acc = x * w[n_tap - 1:n_tap] + b
    x8 = x[0:8]
    acc8 = x8 * w[n_tap - 1:n_tap] + b
    row8 = lax.broadcasted_iota(jnp.int32, x8.shape, 0)
    for s in range(1, n_tap):
        wk = w[n_tap - 1 - s:n_tap - s]
        acc = acc + pltpu.roll(x, s, 0) * wk
        v8 = jnp.where(row8 < s, pltpu.roll(tail, s, 0), pltpu.roll(x8, s, 0))
        acc8 = acc8 + v8 * wk
    return jnp.concatenate([acc8, acc[8:]], axis=0)


def _ssd_kernel(z_ref, xs_ref, bc_ref, sm_ref, cwx_ref, cwbc_ref, cbx_ref, cbbc_ref, dtb_ref,
                nega_ref, dsk_ref, nw_ref, exp_ref, tril_ref, o_ref, tailx_ref, tailbc_ref, ht_ref):
    L = SSD_CHUNK

    @pl.when(pl.program_id(1) == 0)
    def _():
        tailx_ref[...] = jnp.zeros_like(tailx_ref)
        tailbc_ref[...] = jnp.zeros_like(tailbc_ref)
        ht_ref[...] = jnp.zeros_like(ht_ref)

    x_raw = xs_ref[...].astype(F32)
    bc_raw = bc_ref[...].astype(F32)
    xs = _silu(_causal_conv(x_raw, tailx_ref[...], cwx_ref[...], cbx_ref[...]))
    bc = _silu(_causal_conv(bc_raw, tailbc_ref[...], cwbc_ref[...], cbbc_ref[...]))
    tailx_ref[...] = x_raw[L - 8:L]
    tailbc_ref[...] = bc_raw[L - 8:L]

    dt = _softplus(sm_ref[...] + dtb_ref[...])
    loga = dt * nega_ref[...]
    acs = jnp.dot(tril_ref[...], loga, preferred_element_type=F32, precision=HIGHEST)
    acs_t = acs.T
    expand = exp_ref[...]
    dt_e = jnp.dot(dt, expand, preferred_element_type=F32, precision=HIGHEST)
    acs_e = jnp.dot(acs, expand, preferred_element_type=F32, precision=HIGHEST)
    last_e = acs_e[L - 1:L]
    xdt = xs * dt_e
    xdt_b = xdt.astype(BF16)
    xw_b = (xdt * jnp.exp(last_e - acs_e)).astype(BF16)
    dec_in = jnp.exp(acs_e)
    dec_chunk = jnp.exp(last_e)

    row = lax.broadcasted_iota(jnp.int32, (L, L), 0)
    col = lax.broadcasted_iota(jnp.int32, (L, L), 1)
    causal = row >= col
    lo_half = col < SSD_HEAD_DIM
    heads_per_group = SSD_HEADS // SSD_GROUPS
    n_bc = SSD_GROUPS * SSD_STATE

    y_tiles = []
    for g in range(SSD_GROUPS):
        bm = bc[:, g * SSD_STATE:(g + 1) * SSD_STATE]
        cm_b = bc[:, n_bc + g * SSD_STATE:n_bc + (g + 1) * SSD_STATE].astype(BF16)
        bm_t_b = bm.T.astype(BF16)
        cb = jnp.dot(cm_b, bm_t_b, preferred_element_type=F32)
        for j in range(g * heads_per_group // 2, (g + 1) * heads_per_group // 2):
            sl = slice(j * LANES, (j + 1) * LANES)
            xt = xdt_b[:, sl]
            y_pair = None
            for sub in range(2):
                h = 2 * j + sub
                seg = acs[:, h:h + 1] - acs_t[h:h + 1, :]
                m = (cb * jnp.exp(jnp.where(causal, seg, -jnp.inf))).astype(BF16)
                xh = jnp.where(lo_half if sub == 0 else ~lo_half, xt, jnp.zeros_like(xt))
                yd = jnp.dot(m, xh, preferred_element_type=F32)
                y_pair = yd if y_pair is None else y_pair + yd
            h_t = ht_ref[:, sl]
            y_off = jnp.dot(cm_b, h_t.astype(BF16), preferred_element_type=F32) * dec_in[:, sl]
            st = jnp.dot(bm_t_b, xw_b[:, sl], preferred_element_type=F32)
            ht_ref[:, sl] = h_t * dec_chunk[:, sl] + st
            y_tiles.append(y_pair + y_off)
    y = jnp.concatenate(y_tiles, axis=1) + dsk_ref[...] * xs
    y = y * _silu(z_ref[...].astype(F32))
    gw = SSD_INNER // SSD_GROUPS
    outs = []
    for g in range(SSD_GROUPS):
        yg = y[:, g * gw:(g + 1) * gw]
        ms = jnp.mean(yg * yg, -1, keepdims=True)
        outs.append(yg * lax.rsqrt(ms + RMS_EPS) * nw_ref[:, g * gw:(g + 1) * gw])
    o_ref[...] = jnp.concatenate(outs, axis=1).astype(o_ref.dtype)


def _ssd_mixer(proj, small, bsz, seq, conv_w, conv_b, dt_bias, a_log, d_skip, norm_w):
    L = SSD_CHUNK
    nc = seq // L
    pad_lanes = LANES - SSD_HEADS
    cw = jnp.pad(conv_w.astype(F32), ((0, 8 - SSD_CONV), (0, 0)))
    cwx, cwbc = cw[:, :SSD_INNER], cw[:, SSD_INNER:]
    cb = conv_b.astype(F32)[None, :]
    cbx, cbbc = cb[:, :SSD_INNER], cb[:, SSD_INNER:]
    dtb = jnp.pad(dt_bias.astype(F32), (0, pad_lanes))[None, :]
    nega = jnp.pad(-jnp.exp(a_log.astype(F32)), (0, pad_lanes))[None, :]
    dsk = jnp.repeat(d_skip.astype(F32), SSD_HEAD_DIM)[None, :]
    nw = norm_w.astype(F32)[None, :]
    expand = (np.arange(LANES)[:, None] == (np.arange(SSD_INNER)[None, :] // SSD_HEAD_DIM)).astype(np.float32)
    tril = np.tril(np.ones((L, L), np.float32))
    rows = lambda b, c: b * nc + c
    full = lambda shape: pl.BlockSpec(shape, lambda b, c: (0,) * len(shape))
    return pl.pallas_call(
        _ssd_kernel,
        grid=(bsz, nc),
        in_specs=[
            pl.BlockSpec((L, 1024), lambda b, c: (rows(b, c), _CB_Z)),
            pl.BlockSpec((L, 1024), lambda b, c: (rows(b, c), _CB_XS)),
            pl.BlockSpec((L, 512), lambda b, c: (rows(b, c), _CB_BC)),
            pl.BlockSpec((L, LANES), lambda b, c: (rows(b, c), 0)),
            full((8, SSD_INNER)), full((8, 512)), full((1, SSD_INNER)), full((1, 512)),
            full((1, LANES)), full((1, LANES)), full((1, SSD_INNER)), full((1, SSD_INNER)),
            full((LANES, SSD_INNER)), full((L, L)),
        ],
        out_specs=pl.BlockSpec((L, SSD_INNER), lambda b, c: (rows(b, c), 0)),
        out_shape=jax.ShapeDtypeStruct((bsz * seq, SSD_INNER), BF16),
        scratch_shapes=[pltpu.VMEM((8, SSD_INNER), F32), pltpu.VMEM((8, 512), F32),
                        pltpu.VMEM((SSD_STATE, SSD_INNER), F32)],
        compiler_params=_cparams(("parallel", "arbitrary")),
        name="ssd_mixer",
    )(proj, proj, proj, small, cwx, cwbc, cbx, cbbc, dtb, nega, dsk, nw, jnp.asarray(expand), jnp.asarray(tril))


def _s5_scan_kernel(u_ref, bd_ref, w_ref, v_ref, a1_ref, a2_ref, y_ref, toep_ref, sc_ref, hin_ref, hc_ref, *, rows):
    R = rows

    @pl.when((pl.program_id(1) == 0) & (pl.program_id(2) == 0))
    def _():
        toep_ref[...] = jnp.zeros_like(toep_ref)
        for s in range(S5_CHUNK):
            for l in range(s, S5_CHUNK):
                toep_ref[s * LANES:(s + 1) * LANES, l * LANES:(l + 1) * LANES] = bd_ref[0, l - s]

    @pl.when(pl.program_id(2) == 0)
    def _():
        hc_ref[...] = jnp.zeros_like(hc_ref)

    parts = [u_ref[pl.ds(s, R, stride=S5_CHUNK), :].astype(BF16) for s in range(S5_CHUNK)]
    uf = jnp.concatenate(parts, axis=1)
    y_intra = jnp.dot(uf, toep_ref[...], preferred_element_type=F32)
    sc_ref[...] = jnp.dot(uf, w_ref[0], preferred_element_type=F32)
    a1 = a1_ref[0]
    a2 = a2_ref[0]
    half = S5_TILE_GROUPS * S5_STATE

    def step(r, h):
        hin_ref[pl.ds(r, 1), :] = h
        return a1 * h + a2 * pltpu.roll(h, half, 1) + sc_ref[pl.ds(r, 1), :]

    hc_ref[...] = lax.fori_loop(0, R, step, hc_ref[...])
    y = y_intra + jnp.dot(hin_ref[...].astype(BF16), v_ref[0], preferred_element_type=F32)
    for l in range(S5_CHUNK):
        y_ref[pl.ds(l, R, stride=S5_CHUNK), :] = y[:, l * LANES:(l + 1) * LANES]


def _s5_tables(lam_re, lam_im, log_dt, b_re, b_im, c_re, c_im):
    f32 = F32
    lc = S5_CHUNK
    tg = S5_TILE_GROUPS
    nt = S5_GROUPS // tg
    lr = jnp.minimum(lam_re.astype(f32), -1e-4)[None, :]
    li = lam_im.astype(f32)[None, :]
    dt = jnp.exp(log_dt.astype(f32))[:, None]
    mag = jnp.exp(lr * dt)
    ab_re, ab_im = mag * jnp.cos(li * dt), mag * jnp.sin(li * dt)
    den = lr * lr + li * li
    nr, ni = ab_re - 1.0, ab_im
    f_re, f_im = (nr * lr + ni * li) / den, (ni * lr - nr * li) / den
    b_re, b_im = b_re.astype(f32), b_im.astype(f32)
    bb_re = f_re[..., None] * b_re - f_im[..., None] * b_im
    bb_im = f_re[..., None] * b_im + f_im[..., None] * b_re
    tau = jnp.arange(lc + 1, dtype=f32)[:, None, None]
    pw_mag = jnp.exp(tau * (lr * dt)[None])
    pw_re = pw_mag * jnp.cos(tau * (li * dt)[None])
    pw_im = pw_mag * jnp.sin(tau * (li * dt)[None])
    c_re, c_im = c_re.astype(f32), c_im.astype(f32)
    cl_re = c_re[None] * pw_re[:, :, None, :] - c_im[None] * pw_im[:, :, None, :]
    cl_im = c_re[None] * pw_im[:, :, None, :] + c_im[None] * pw_re[:, :, None, :]
    hp = lax.Precision.HIGHEST
    kern = (jnp.einsum('tgop,gpi->tgoi', cl_re[:lc], bb_re, precision=hp)
            - jnp.einsum('tgop,gpi->tgoi', cl_im[:lc], bb_im, precision=hp))
    eye = jnp.eye(tg, dtype=f32)
    toep = jnp.einsum('tTgoi,gh->Ttgiho', kern.reshape(lc, nt, tg, S5_GROUP, S5_GROUP), eye)
    toep = toep.reshape(nt, lc, LANES, LANES)
    rev = pw_re[lc - 1 - np.arange(lc)], pw_im[lc - 1 - np.arange(lc)]
    wr = rev[0][..., None] * bb_re[None] - rev[1][..., None] * bb_im[None]
    wi = rev[0][..., None] * bb_im[None] + rev[1][..., None] * bb_re[None]
    wst = jnp.stack([wr, wi], 0).reshape(2, lc, nt, tg, S5_STATE, S5_GROUP)
    w_in = jnp.einsum('rsTgpi,gh->Tsgirhp', wst, eye).reshape(nt, lc * LANES, 2 * tg * S5_STATE)
    vst = jnp.stack([cl_re[1:], -cl_im[1:]], 0).reshape(2, lc, nt, tg, S5_GROUP, S5_STATE)
    v_out = jnp.einsum('rlTgop,gh->Trgplho', vst, eye).reshape(nt, 2 * tg * S5_STATE, lc * LANES)
    a_re = pw_re[lc].reshape(nt, 1, tg * S5_STATE)
    a_im = pw_im[lc].reshape(nt, 1, tg * S5_STATE)
    a1 = jnp.concatenate([a_re, a_re], -1)
    a2 = jnp.concatenate([-a_im, a_im], -1)
    return toep.astype(BF16), w_in.astype(BF16), v_out.astype(BF16), a1, a2


def _s5_scan(u, bsz, seq, tables, rows):
    toep, w_in, v_out, a1, a2 = tables
    nt = S5_GROUPS // S5_TILE_GROUPS
    lc = S5_CHUNK
    rows = min(rows, seq // lc)
    nblk = seq // (lc * rows)
    tok = rows * lc
    ns = 2 * S5_TILE_GROUPS * S5_STATE
    return pl.pallas_call(
        functools.partial(_s5_scan_kernel, rows=rows),
        grid=(nt, bsz, nblk),
        in_specs=[
            pl.BlockSpec((tok, LANES), lambda t, b, r: (b * nblk + r, t)),
            pl.BlockSpec((1, lc, LANES, LANES), lambda t, b, r: (t, 0, 0, 0)),
            pl.BlockSpec((1, lc * LANES, ns), lambda t, b, r: (t, 0, 0)),
            pl.BlockSpec((1, ns, lc * LANES), lambda t, b, r: (t, 0, 0)),
            pl.BlockSpec((1, 1, ns), lambda t, b, r: (t, 0, 0)),
            pl.BlockSpec((1, 1, ns), lambda t, b, r: (t, 0, 0)),
        ],
        out_specs=pl.BlockSpec((tok, LANES), lambda t, b, r: (b * nblk + r, t)),
        out_shape=jax.ShapeDtypeStruct((bsz * seq, S5_WIDTH), F32),
        scratch_shapes=[pltpu.VMEM((lc * LANES, lc * LANES), BF16), pltpu.VMEM((rows, ns), F32),
                        pltpu.VMEM((rows, ns), F32), pltpu.VMEM((1, ns), F32)],
        compiler_params=_cparams(("arbitrary", "arbitrary", "arbitrary")),
        name="s5_scan",
    )(u, toep, w_in, v_out, a1, a2)


def _s5_post_kernel(y_ref, u_ref, d_ref, w_ref, b_ref, o_ref):
    g = jax.nn.gelu(y_ref[...] + d_ref[...] * u_ref[...], approximate=True)
    gate = jnp.dot(g.astype(BF16), w_ref[...], preferred_element_type=F32) + b_ref[...]
    o_ref[...] = (g * jax.nn.sigmoid(gate)).astype(o_ref.dtype)


def _s5_post(y, u, d_skip, w_glu, b_glu, tm=512):
    t = y.shape[0]
    tm = min(tm, t)
    row = pl.BlockSpec((tm, S5_WIDTH), lambda i: (i, 0))
    vec = pl.BlockSpec((1, S5_WIDTH), lambda i: (0, 0))
    return pl.pallas_call(
        _s5_post_kernel,
        grid=(t // tm,),
        in_specs=[row, row, vec, pl.BlockSpec((S5_WIDTH, S5_WIDTH), lambda i: (0, 0)), vec],
        out_specs=row,
        out_shape=jax.ShapeDtypeStruct((t, S5_WIDTH), BF16),
        compiler_params=_cparams(("parallel",)),
        name="s5_post",
    )(y, u, d_skip.astype(F32)[None, :], w_glu.astype(BF16), b_glu.astype(F32)[None, :])


def _ret_kernel(q_ref, k_ref, v_ref, g_ref, cos_ref, sin_ref, intra_ref, dend_ref, osc_ref, nw_ref,
                o_ref, r_ref, *, chunk_decay):
    L = RET_CHUNK

    @pl.when(pl.program_id(1) == 0)
    def _():
        r_ref[...] = jnp.zeros_like(r_ref)

    cos = cos_ref[...]
    sin = sin_ref[...]
    lane = lax.broadcasted_iota(jnp.int32, (L, LANES), 1)
    rowi = lax.broadcasted_iota(jnp.int32, (LANES, L), 0)
    half = RET_QK_DIM // 2
    outs = []
    for t in range(RET_HEADS // 2):
        sl = slice(t * LANES, (t + 1) * LANES)
        qt = q_ref[:, sl].astype(F32)
        kt = k_ref[:, sl].astype(F32)
        qr = qt * cos + pltpu.roll(qt, 64, 1) * sin
        kr = (kt * cos + pltpu.roll(kt, 64, 1) * sin) * (RET_QK_DIM ** -0.5)
        kr_b = kr.astype(BF16)
        kr_t = kr.T
        for sub in range(2):
            h = 2 * t + sub
            vs = slice(h * RET_V_DIM, (h + 1) * RET_V_DIM)
            qm = jnp.where(((lane // half) % 2) == sub, qr, 0.0).astype(BF16)
            s = lax.dot_general(qm, kr_b, (((1,), (1,)), ((), ())), preferred_element_type=F32)
            p = (s * intra_ref[h]).astype(BF16)
            vh = v_ref[:, vs]
            state = r_ref[h]
            y_in = jnp.dot(p, vh, preferred_element_type=F32)
            y_off = jnp.dot(qm, state.astype(BF16), preferred_element_type=F32) * osc_ref[:, vs]
            ktm = jnp.where(((rowi // half) % 2) == sub, kr_t * dend_ref[h:h + 1, :], 0.0).astype(BF16)
            r_ref[h] = state * chunk_decay[h] + jnp.dot(ktm, vh, preferred_element_type=F32)
            y = y_in + y_off
            mu = jnp.mean(y, -1, keepdims=True)
            yc = y - mu
            var = jnp.mean(yc * yc, -1, keepdims=True)
            yn = yc * lax.rsqrt(var + LN_EPS) * nw_ref[:, vs]
            outs.append(_silu(g_ref[:, vs].astype(F32)) * yn)
    o_ref[...] = jnp.concatenate(outs, axis=1).astype(o_ref.dtype)


def _ret_mixer(proj, bsz, seq, positions, norm_w):
    L = RET_CHUNK
    nc = seq // L
    half = RET_QK_DIM // 2
    inv_freq = 1.0 / (ROPE_BASE ** (jnp.arange(half, dtype=F32) / half))
    ang = positions.astype(F32).reshape(bsz * seq, 1) * inv_freq[None, :]
    cos32, sin32 = jnp.cos(ang), jnp.sin(ang)
    cos_t = jnp.concatenate([cos32] * 4, axis=1)
    sin_t = jnp.concatenate([-sin32, -sin32, sin32, sin32], axis=1)
    log_gamma = np.log1p(-np.exp2(-5.0 - np.arange(RET_HEADS, dtype=np.float64)))
    pos = np.arange(L, dtype=np.float64)
    rel = pos[:, None] - pos[None, :]
    intra = np.where(rel[None] >= 0, np.exp(rel[None] * log_gamma[:, None, None]), 0.0).astype(np.float32)
    dend = np.exp((L - 1 - pos)[None, :] * log_gamma[:, None]).astype(np.float32)
    osc = np.repeat(np.exp((pos + 1)[:, None] * log_gamma[None, :]), RET_V_DIM, axis=1).astype(np.float32)
    chunk_decay = tuple(float(v) for v in np.exp(L * log_gamma))
    rows = lambda b, c: b * nc + c
    full = lambda shape: pl.BlockSpec(shape, lambda b, c: (0,) * len(shape))
    return pl.pallas_call(
        functools.partial(_ret_kernel, chunk_decay=chunk_decay),
        grid=(bsz, nc),
        in_specs=[
            pl.BlockSpec((L, 512), lambda b, c: (rows(b, c), _CB_RQ)),
            pl.BlockSpec((L, 512), lambda b, c: (rows(b, c), _CB_RK)),
            pl.BlockSpec((L, 1024), lambda b, c: (rows(b, c), _CB_RV)),
            pl.BlockSpec((L, 1024), lambda b, c: (rows(b, c), _CB_RG)),
            pl.BlockSpec((L, LANES), lambda b, c: (rows(b, c), 0)),
            pl.BlockSpec((L, LANES), lambda b, c: (rows(b, c), 0)),
            full((RET_HEADS, L, L)), full((RET_HEADS, L)), full((L, RET_WIDTH)), full((1, RET_WIDTH)),
        ],
        out_specs=pl.BlockSpec((L, RET_WIDTH), lambda b, c: (rows(b, c), 0)),
        out_shape=jax.ShapeDtypeStruct((bsz * seq, RET_WIDTH), BF16),
        scratch_shapes=[pltpu.VMEM((RET_HEADS, LANES, RET_V_DIM), F32)],
        compiler_params=_cparams(("parallel", "arbitrary")),
        name="retention_mixer",
    )(proj, proj, proj, proj, cos_t, sin_t, jnp.asarray(intra), jnp.asarray(dend), jnp.asarray(osc),
      norm_w.astype(F32)[None, :])


def _gla_kernel(q_ref, k_ref, v_ref, r_ref, sm_ref, wa_ref, ba_ref, tblk_ref, nw_ref, o_ref, s_ref):
    LB = GLA_BLOCK
    LC = GLA_CHUNK

    @pl.when(pl.program_id(1) == 0)
    def _():
        s_ref[...] = jnp.zeros_like(s_ref)

    logit = jnp.dot(sm_ref[...], wa_ref[...], preferred_element_type=F32, precision=HIGHEST) + ba_ref[...]
    la = (jnp.minimum(logit, 0.0) - jnp.log1p(jnp.exp(-jnp.abs(logit)))) * (1.0 / GLA_TAU)
    b = jnp.dot(tblk_ref[...], la, preferred_element_type=F32, precision=HIGHEST)
    rowf = lax.broadcasted_iota(jnp.int32, (LB, GLA_QK), 0)
    b_end = jnp.where(rowf < LC, b[LC - 1:LC], b[LB - 1:LB])
    q = q_ref[...].astype(F32) * (GLA_QK_DIM ** -0.5)
    k = k_ref[...].astype(F32)
    q_dec = q * jnp.exp(b)
    k_inv = (k * jnp.exp(-b)).astype(BF16)
    k_dec = k * jnp.exp(b_end - b)
    row = lax.broadcasted_iota(jnp.int32, (LB, LB), 0)
    col = lax.broadcasted_iota(jnp.int32, (LB, LB), 1)
    blk_causal = (row >= col) & ((row // LC) == (col // LC))
    first_rows = row < LC
    first_vrows = lax.broadcasted_iota(jnp.int32, (LB, GLA_V_DIM), 0) < LC
    outs = []
    for h in range(GLA_HEADS):
        sl = slice(h * GLA_QK_DIM, (h + 1) * GLA_QK_DIM)
        vs = slice(h * GLA_V_DIM, (h + 1) * GLA_V_DIM)
        qd = q_dec[:, sl]
        qd_b = qd.astype(BF16)
        att = lax.dot_general(qd_b, k_inv[:, sl], (((1,), (1,)), ((), ())), preferred_element_type=F32)
        att = jnp.where(blk_causal, att, 0.0).astype(BF16)
        vh = v_ref[:, vs]
        y = jnp.dot(att, vh, preferred_element_type=F32)
        s0 = s_ref[h]
        kd_t = k_dec[:, sl].T.astype(BF16)
        kv0 = jnp.dot(kd_t, jnp.where(first_vrows, vh, jnp.zeros_like(vh)), preferred_element_type=F32)
        kv1 = jnp.dot(kd_t, jnp.where(first_vrows, jnp.zeros_like(vh), vh), preferred_element_type=F32)
        b_t = b[:, sl].T
        s1 = s0 * jnp.exp(b_t[:, LC - 1:LC]) + kv0
        s_ref[h] = s1 * jnp.exp(b_t[:, LB - 1:LB]) + kv1
        y = y + jnp.dot(jnp.where(first_rows, qd, 0.0).astype(BF16), s0.astype(BF16), preferred_element_type=F32)
        y = y + jnp.dot(jnp.where(first_rows, 0.0, qd).astype(BF16), s1.astype(BF16), preferred_element_type=F32)
        ms = jnp.mean(y * y, -1, keepdims=True)
        yn = y * lax.rsqrt(ms + RMS_EPS) * nw_ref[:, vs]
        outs.append(yn * _silu(r_ref[:, vs].astype(F32)))
    o_ref[...] = jnp.concatenate(outs, axis=1).astype(o_ref.dtype)


def _gla_mixer(proj, small, bsz, seq, w_alpha, b_alpha, norm_w):
    LB = GLA_BLOCK
    nb = seq // LB
    wa = jnp.zeros((LANES, GLA_QK), F32).at[SSD_HEADS:SSD_HEADS + GLA_GATE_RANK].set(w_alpha.astype(F32))
    idx = np.arange(LB)
    tblk = ((idx[:, None] >= idx[None, :]) & ((idx[:, None] // GLA_CHUNK) == (idx[None, :] // GLA_CHUNK)))
    rows = lambda b, c: b * nb + c
    full = lambda shape: pl.BlockSpec(shape, lambda b, c: (0,) * len(shape))
    return pl.pallas_call(
        _gla_kernel,
        grid=(bsz, nb),
        in_specs=[
            pl.BlockSpec((LB, 512), lambda b, c: (rows(b, c), _CB_GQ)),
            pl.BlockSpec((LB, 512), lambda b, c: (rows(b, c), _CB_GK)),
            pl.BlockSpec((LB, 1024), lambda b, c: (rows(b, c), _CB_GV)),
            pl.BlockSpec((LB, 1024), lambda b, c: (rows(b, c), _CB_GR)),
            pl.BlockSpec((LB, LANES), lambda b, c: (rows(b, c), 0)),
            full((LANES, GLA_QK)), full((1, GLA_QK)), full((LB, LB)), full((1, GLA_WIDTH)),
        ],
        out_specs=pl.BlockSpec((LB, GLA_WIDTH), lambda b, c: (rows(b, c), 0)),
        out_shape=jax.ShapeDtypeStruct((bsz * seq, GLA_WIDTH), BF16),
        scratch_shapes=[pltpu.VMEM((GLA_HEADS, GLA_QK_DIM, GLA_V_DIM), F32)],
        compiler_params=_cparams(("parallel", "arbitrary")),
        name="gla_mixer",
    )(proj, proj, proj, proj, small, wa, b_alpha.astype(F32)[None, :], jnp.asarray(tblk.astype(np.float32)),
      norm_w.astype(F32)[None, :])


def _merge_kernel(ya_ref, yb_ref, yc_ref, yd_ref, g0_ref, g1_ref, g2_ref, g3_ref, wb_ref, o_ref):
    acc = None
    for n, (y_ref, g_ref) in enumerate(((ya_ref, g0_ref), (yb_ref, g1_ref), (yc_ref, g2_ref), (yd_ref, g3_ref))):
        br = jnp.dot(y_ref[...], wb_ref[n], preferred_element_type=F32)
        term = jax.nn.sigmoid(g_ref[...].astype(F32)) * br
        acc = term if acc is None else acc + term
    o_ref[...] = acc.astype(o_ref.dtype)


def _merge(ys, proj, w_branch, tm=1024, tn=512):
    t = ys[0].shape[0]
    tm = min(tm, t)
    nj = D_MODEL // tn
    yspec = pl.BlockSpec((tm, BRANCH_WIDTH), lambda i, j: (i, 0))
    gspec = lambda n: pl.BlockSpec((tm, tn), lambda i, j: (i, n * nj + j))
    return pl.pallas_call(
        _merge_kernel,
        grid=(t // tm, nj),
        in_specs=[yspec] * 4 + [gspec(n) for n in range(N_BRANCH)]
                 + [pl.BlockSpec((N_BRANCH, BRANCH_WIDTH, tn), lambda i, j: (0, 0, j))],
        out_specs=pl.BlockSpec((tm, tn), lambda i, j: (i, j)),
        out_shape=jax.ShapeDtypeStruct((t, D_MODEL), BF16),
        compiler_params=_cparams(("parallel", "parallel")),
        name="branch_merge",
    )(*ys, proj, proj, proj, proj, w_branch.astype(BF16))


def _proj_ln_kernel(m_ref, w_ref, h_ref, lw_ref, lb_ref, o_ref, op_ref):
    mix = jnp.dot(m_ref[...], w_ref[...], preferred_element_type=F32)
    out = _layer_norm(DEEPNORM_ALPHA * h_ref[...] + mix, lw_ref[...], lb_ref[...])
    o_ref[...] = out
    _store_token_tiles(op_ref, 0, _pack_rows(out.astype(BF16)))


def _proj_ln(merged, w_out, h, ln_w, ln_b, tm=512):
    t = h.shape[0]
    tm = min(tm, t)
    row = pl.BlockSpec((tm, D_MODEL), lambda i: (i, 0))
    vec = pl.BlockSpec((1, D_MODEL), lambda i: (0, 0))
    return pl.pallas_call(
        _proj_ln_kernel,
        grid=(t // tm,),
        in_specs=[row, pl.BlockSpec((D_MODEL, D_MODEL), lambda i: (0, 0)), row, vec, vec],
        out_specs=[row, pl.BlockSpec((tm * TILE_ROWS, LANES), lambda i: (i, 0))],
        out_shape=[jax.ShapeDtypeStruct((t, D_MODEL), F32), jax.ShapeDtypeStruct((t * TILE_ROWS, LANES), jnp.uint32)],
        compiler_params=_cparams(("parallel",)),
        name="out_proj_ln",
    )(merged, w_out.astype(BF16), h, ln_w.astype(F32)[None, :], ln_b.astype(F32)[None, :])


def _router_kernel(h_ref, w_ref, b_ref, idx_ref, gate_ref, cnt_ref, run_ref):
    @pl.when(pl.program_id(0) == 0)
    def _():
        run_ref[...] = jnp.zeros_like(run_ref)

    logits = _dot3(w_ref[...], h_ref[...], ((1,), (1,))) + b_ref[...]
    eid = lax.broadcasted_iota(jnp.int32, logits.shape, 0)
    vals = logits
    run = run_ref[...]
    top_v, top_i = [], []
    for _ in range(TOP_K):
        m = jnp.max(vals, axis=0, keepdims=True)
        sel = jnp.min(jnp.where(vals == m, eid, N_EXPERTS), axis=0, keepdims=True)
        hit = eid == sel
        top_v.append(m)
        top_i.append(sel)
        vals = jnp.where(hit, -jnp.inf, vals)
        run = run + jnp.sum(jnp.where(hit, 1.0, 0.0), axis=1, keepdims=True)
    run_ref[...] = run
    ex = [jnp.exp(v - top_v[0]) for v in top_v]
    den = ex[0] + ex[1] + ex[2] + ex[3]
    zi = jnp.zeros_like(top_i[0])
    zf = jnp.zeros_like(den)
    idx_ref[...] = jnp.concatenate(top_i + [zi] * (8 - TOP_K), axis=0)
    gate_ref[...] = jnp.concatenate([e / den for e in ex] + [zf] * (8 - TOP_K), axis=0)
    cnt_ref[...] = jnp.broadcast_to(run, cnt_ref.shape).astype(jnp.int32)


def _router(h, router_w, router_b, tm=512):
    t = h.shape[0]
    tm = min(tm, t)
    tok = pl.BlockSpec((8, tm), lambda i: (0, i))
    return pl.pallas_call(
        _router_kernel,
        grid=(t // tm,),
        in_specs=[pl.BlockSpec((tm, D_MODEL), lambda i: (i, 0)),
                  pl.BlockSpec((N_EXPERTS, D_MODEL), lambda i: (0, 0)),
                  pl.BlockSpec((N_EXPERTS, 1), lambda i: (0, 0))],
        out_specs=[tok, tok, pl.BlockSpec((N_EXPERTS, LANES), lambda i: (0, 0))],
        out_shape=[jax.ShapeDtypeStruct((8, t), jnp.int32), jax.ShapeDtypeStruct((8, t), F32),
                   jax.ShapeDtypeStruct((N_EXPERTS, LANES), jnp.int32)],
        scratch_shapes=[pltpu.VMEM((N_EXPERTS, 1), F32)],
        compiler_params=_cparams(("arbitrary",)),
        name="router_topk",
    )(h, router_w.astype(F32).T, router_b.astype(F32)[:, None])


def _pack_rows(x):
    n = x.shape[1] // 2
    lo = pltpu.bitcast(x[:, :n].astype(F32), jnp.uint32) >> 16
    hi = pltpu.bitcast(x[:, n:].astype(F32), jnp.uint32)
    return lo | hi


def _unpack_rows(w):
    lo = pltpu.bitcast(w << 16, F32)
    hi = pltpu.bitcast(w & jnp.uint32(0xFFFF0000), F32)
    return lo, hi


TILE_ROWS = (D_MODEL // 2) // LANES


def _store_token_tiles(ref, first, words):
    m = words.shape[0]
    for s in range(TILE_ROWS):
        ref[pl.ds(first * TILE_ROWS + s, m, stride=TILE_ROWS), :] = words[:, s * LANES:(s + 1) * LANES]


def _load_token_tiles(ref, first, m):
    return jnp.concatenate([ref[pl.ds(first * TILE_ROWS + s, m, stride=TILE_ROWS), :] for s in range(TILE_ROWS)],
                           axis=1)


def _expert_kernel(be_ref, nb_ref, tab_ref, hp_ref, wgu_ref, bgu_ref, wd_ref, bd_ref, y_ref,
                   idx_ref, xbuf_ref, ybuf_ref, isem, gsem, ssem):
    i = pl.program_id(0)
    n_used = nb_ref[0]
    blk = MOE_BLOCK
    tr = TILE_ROWS

    def idx_copy(b):
        return pltpu.make_async_copy(tab_ref.at[b], idx_ref.at[b % 4], isem.at[b % 4])

    def tile(r):
        return pl.ds(r * tr if isinstance(r, int) else pl.multiple_of(r * tr, tr), tr)

    def gather_start(b, r):
        src = pl.multiple_of(idx_ref[b % 4, r], tr)
        pltpu.make_async_copy(hp_ref.at[pl.ds(src, tr)], xbuf_ref.at[b % 2, tile(r)], gsem.at[b % 2]).start()

    def scatter_start(b, r):
        dst = pl.multiple_of(idx_ref[b % 4, blk + r], tr)
        pltpu.make_async_copy(ybuf_ref.at[b % 2, tile(r)], y_ref.at[pl.ds(dst, tr)], ssem.at[b % 2]).start(priority=1)

    def gather_wait(b):
        pltpu.make_async_copy(hp_ref.at[pl.ds(0, blk * tr)], xbuf_ref.at[b % 2], gsem.at[b % 2]).wait()

    def scatter_wait(b):
        pltpu.make_async_copy(ybuf_ref.at[b % 2], y_ref.at[pl.ds(0, blk * tr)], ssem.at[b % 2]).wait()

    def ffn(b):
        lo, hi = _unpack_rows(_load_token_tiles(xbuf_ref.at[b % 2], 0, blk))
        xb = jnp.concatenate([lo.astype(BF16), hi.astype(BF16)], axis=1)
        hgu = jnp.dot(xb, wgu_ref[0], preferred_element_type=F32) + bgu_ref[0]
        g = jnp.minimum(hgu[:, :D_EXPERT], SWIGLU_LIMIT)
        u = jnp.clip(hgu[:, D_EXPERT:], -SWIGLU_LIMIT, SWIGLU_LIMIT)
        act = (u + 1.0) * (g * jax.nn.sigmoid(SWIGLU_ALPHA * g))
        out = jnp.dot(act.astype(BF16), wd_ref[0], preferred_element_type=F32) + bd_ref[0]
        _store_token_tiles(ybuf_ref.at[b % 2], 0, _pack_rows(out.astype(BF16)))

    @pl.when(i == 0)
    def _():
        idx_copy(0).start()
        ybuf_ref[1] = jnp.zeros(ybuf_ref.shape[1:], ybuf_ref.dtype)
        spare = pltpu.make_async_copy(ybuf_ref.at[1], y_ref.at[pl.ds(y_ref.shape[0] - blk * tr, blk * tr)],
                                      ssem.at[1])
        spare.start()
        idx_copy(0).wait()
        lax.fori_loop(0, blk, lambda r, c: (gather_start(0, r), c)[1], 0, unroll=8)
        idx_copy(1).start()
        spare.wait()

    @pl.when((i >= 2) & (i <= n_used))
    def _():
        scatter_wait(i - 2)

    @pl.when(i <= n_used)
    def _():
        idx_copy(i + 1).wait()
        gather_wait(i)

    @pl.when(i == 0)
    def _():
        for r in range(blk):
            gather_start(1, r)
        ffn(0)

    @pl.when((i > 0) & (i < n_used))
    def _():
        for r in range(blk):
            gather_start(i + 1, r)
            scatter_start(i - 1, r)
        ffn(i)

    @pl.when(i < n_used)
    def _():
        idx_copy(i + 2).start()

    @pl.when(i == n_used)
    def _():
        lax.fori_loop(0, blk, lambda r, c: (scatter_start(i - 1, r), c)[1], 0, unroll=8)
        scatter_wait(i - 1)


def _expert_params(w_gate_up, b_gate_up, w_down, b_down):
    n = w_gate_up.shape[0] * N_EXPERTS
    return (w_gate_up.astype(BF16).reshape(n, D_MODEL, 2 * D_EXPERT), b_gate_up.astype(F32).reshape(n, 1, 2 * D_EXPERT),
            w_down.astype(BF16).reshape(n, D_EXPERT, D_MODEL), b_down.astype(F32).reshape(n, 1, D_MODEL))


def _experts(h_packed, table, block_expert, n_used, expert_params):
    w_gate_up, b_gate_up, w_down, b_down = expert_params
    blk = MOE_BLOCK
    n_blocks = block_expert.shape[0]
    n_out = TOP_K * (h_packed.shape[0] // TILE_ROWS) + blk
    grid_spec = pltpu.PrefetchScalarGridSpec(
        num_scalar_prefetch=2,
        grid=(n_blocks,),
        in_specs=[
            pl.BlockSpec(memory_space=pl.ANY),
            pl.BlockSpec(memory_space=pl.ANY),
            pl.BlockSpec((1, D_MODEL, 2 * D_EXPERT), lambda i, be, nb: (be[i], 0, 0)),
            pl.BlockSpec((1, 1, 2 * D_EXPERT), lambda i, be, nb: (be[i], 0, 0)),
            pl.BlockSpec((1, D_EXPERT, D_MODEL), lambda i, be, nb: (be[i], 0, 0)),
            pl.BlockSpec((1, 1, D_MODEL), lambda i, be, nb: (be[i], 0, 0)),
        ],
        out_specs=pl.BlockSpec(memory_space=pl.ANY),
        scratch_shapes=[pltpu.SMEM((4, 2 * blk), jnp.int32), pltpu.VMEM((2, blk * TILE_ROWS, LANES), jnp.uint32),
                        pltpu.VMEM((2, blk * TILE_ROWS, LANES), jnp.uint32), pltpu.SemaphoreType.DMA((4,)),
                        pltpu.SemaphoreType.DMA((2,)), pltpu.SemaphoreType.DMA((2,))],
    )
    return pl.pallas_call(
        _expert_kernel,
        grid_spec=grid_spec,
        out_shape=jax.ShapeDtypeStruct((n_out * TILE_ROWS, LANES), jnp.uint32),
        compiler_params=_cparams(("arbitrary",)),
        name="expert_ffn",
    )(block_expert, n_used, table, h_packed, w_gate_up, b_gate_up, w_down, b_down)


def _combine_kernel(y0_ref, y1_ref, y2_ref, y3_ref, gate_ref, h_ref, lw_ref, lb_ref, o_ref, ob_ref, *, tm):
    rows = min(tm, 16)
    for r0 in range(0, tm, rows):
        rs = slice(r0, r0 + rows)
        gates = gate_ref[rs, :]
        ffn = None
        for k, y_ref in enumerate((y0_ref, y1_ref, y2_ref, y3_ref)):
            lo, hi = _unpack_rows(_load_token_tiles(y_ref, r0, rows))
            term = gates[:, k:k + 1] * jnp.concatenate([lo, hi], axis=1)
            ffn = term if ffn is None else ffn + term
        out = _layer_norm(DEEPNORM_ALPHA * h_ref[rs, :] + ffn, lw_ref[...], lb_ref[...])
        o_ref[rs, :] = out
        ob_ref[rs, :] = out.astype(BF16)


def _combine(y, gates, h, ln_w, ln_b, tm=512):
    t = h.shape[0]
    tm = min(tm, t)
    nt = t // tm
    row = pl.BlockSpec((tm, D_MODEL), lambda i: (i, 0))
    vec = pl.BlockSpec((1, D_MODEL), lambda i: (0, 0))
    yspec = lambda k: pl.BlockSpec((tm * TILE_ROWS, LANES), lambda i: (k * nt + i, 0))
    return pl.pallas_call(
        functools.partial(_combine_kernel, tm=tm),
        grid=(nt,),
        in_specs=[yspec(k) for k in range(TOP_K)] + [pl.BlockSpec((tm, 8), lambda i: (i, 0)), row, vec, vec],
        out_specs=[row, row],
        out_shape=[jax.ShapeDtypeStruct((t, D_MODEL), F32), jax.ShapeDtypeStruct((t, D_MODEL), BF16)],
        compiler_params=_cparams(("parallel",)),
        name="moe_combine_ln",
    )(y, y, y, y, gates, h, ln_w.astype(F32)[None, :], ln_b.astype(F32)[None, :])


def _moe(h, h_packed, router_w, router_b, expert_params, layer, ln_w, ln_b):
    t = h.shape[0]
    blk = MOE_BLOCK
    top_idx, top_gate, counts = _router(h, router_w, router_b)
    counts = counts[:, 0]
    n_assign = TOP_K * t
    n_blocks = n_assign // blk + N_EXPERTS
    a_ids = jnp.arange(n_assign, dtype=jnp.int32)
    a_sorted = jnp.sort(top_idx[:TOP_K].reshape(-1) * n_assign + a_ids) % n_assign
    padded = (counts + blk - 1) // blk * blk
    pad_end = jnp.cumsum(padded)
    grp_end = jnp.cumsum(counts)
    blocks = jnp.arange(n_blocks, dtype=jnp.int32)
    block_expert = jnp.sum(((pad_end // blk)[None, :] <= blocks[:, None]).astype(jnp.int32), axis=1)
    block_expert = jnp.minimum(block_expert, N_EXPERTS - 1)
    n_used = (pad_end[-1] // blk).astype(jnp.int32).reshape(1)
    is_e = block_expert[:, None] == jnp.arange(N_EXPERTS, dtype=jnp.int32)[None, :]
    pick = lambda v: jnp.sum(jnp.where(is_e, v[None, :], 0), axis=1)
    offset = blocks * blk - pick(pad_end - padded)
    first = pick(grp_end - counts) + offset
    n_valid = jnp.clip(pick(counts) - offset, 0, blk)
    j = jnp.arange(blk, dtype=jnp.int32)[None, :]
    valid = j < n_valid[:, None]
    a_rows = a_sorted[jnp.clip(first[:, None] + j, 0, n_assign - 1)]
    tok_rows = jnp.where(valid, a_rows % t, 0)
    out_rows = jnp.where(valid, a_rows, n_assign + j)
    table = (jnp.concatenate([tok_rows, out_rows], axis=1) * TILE_ROWS).astype(jnp.int32)
    table = jnp.concatenate([table, jnp.zeros((2, 2 * blk), jnp.int32)], axis=0)
    y = _experts(h_packed, table, block_expert + layer * N_EXPERTS, n_used, expert_params)
    return _combine(y, top_gate.T, h, ln_w, ln_b)


def _ple_kernel(hb_ref, h_ref, p_ref, wg_ref, wp_ref, lw_ref, lb_ref, o_ref, ob_ref):
    gate = jax.nn.sigmoid(jnp.dot(hb_ref[...], wg_ref[...], preferred_element_type=F32))
    emb = jnp.dot(p_ref[...].astype(BF16), wp_ref[...], preferred_element_type=F32)
    out = _layer_norm(DEEPNORM_ALPHA * h_ref[...] + gate * emb, lw_ref[...], lb_ref[...])
    o_ref[...] = out
    ob_ref[...] = out.astype(BF16)


def _ple(hb, h, p_i, w_gate, w_proj, ln_w, ln_b, tm=512):
    t = h.shape[0]
    tm = min(tm, t)
    row = pl.BlockSpec((tm, D_MODEL), lambda i: (i, 0))
    vec = pl.BlockSpec((1, D_MODEL), lambda i: (0, 0))
    return pl.pallas_call(
        _ple_kernel,
        grid=(t // tm,),
        in_specs=[row, row, pl.BlockSpec((tm, PLE_DIM), lambda i: (i, 0)),
                  pl.BlockSpec((D_MODEL, D_MODEL), lambda i: (0, 0)),
                  pl.BlockSpec((PLE_DIM, D_MODEL), lambda i: (0, 0)), vec, vec],
        out_specs=[row, row],
        out_shape=[jax.ShapeDtypeStruct((t, D_MODEL), F32), jax.ShapeDtypeStruct((t, D_MODEL), BF16)],
        compiler_params=_cparams(("parallel",)),
        name="ple_ln",
    )(hb, h, p_i, w_gate.astype(BF16), w_proj.astype(BF16), ln_w.astype(F32)[None, :], ln_b.astype(F32)[None, :])


def _token_mixer(h, hb, bsz, seq, positions, w_in, ssd_conv_w, ssd_conv_b, ssd_dt_bias, ssd_a_log, ssd_d,
                 ssd_norm_w, s5_lambda_re, s5_lambda_im, s5_log_dt, s5_b_re, s5_b_im, s5_c_re, s5_c_im, s5_d,
                 s5_w_glu, s5_b_glu, ret_norm_w, gla_w_alpha, gla_b_alpha, gla_norm_w, w_branch, s5_rows):
    w_main = _main_weight(w_in)
    w_u = w_in[:, _OFF_U:_OFF_U + S5_WIDTH].astype(BF16)
    w_small = jnp.concatenate([w_in[:, _OFF_DT:_OFF_DT + SSD_HEADS], w_in[:, _OFF_GC:_OFF_GC + GLA_GATE_RANK],
                               jnp.zeros((D_MODEL, LANES - SSD_HEADS - GLA_GATE_RANK), w_in.dtype)], axis=1)
    proj = _matmul(hb, w_main, BF16, 1024, 1536, name="in_proj_main")
    u = _matmul(hb, w_u, F32, 1024, 1024, name="in_proj_s5")
    small = _matmul_f32(h, w_small.astype(F32), 512, name="in_proj_small")
    y_a = _ssd_mixer(proj, small, bsz, seq, ssd_conv_w, ssd_conv_b, ssd_dt_bias, ssd_a_log, ssd_d, ssd_norm_w)
    tables = _s5_tables(s5_lambda_re, s5_lambda_im, s5_log_dt, s5_b_re, s5_b_im, s5_c_re, s5_c_im)
    y_b = _s5_post(_s5_scan(u, bsz, seq, tables, s5_rows), u, s5_d, s5_w_glu, s5_b_glu)
    y_c = _ret_mixer(proj, bsz, seq, positions, ret_norm_w)
    y_d = _gla_mixer(proj, small, bsz, seq, gla_w_alpha, gla_b_alpha, gla_norm_w)
    return _merge((y_a, y_b, y_c, y_d), proj, w_branch)


def kernel(x, p, positions, w_in, ssd_conv_w, ssd_conv_b, ssd_dt_bias, ssd_a_log, ssd_d, ssd_norm_w, s5_lambda_re, s5_lambda_im, s5_log_dt, s5_b_re, s5_b_im, s5_c_re, s5_c_im, s5_d, s5_w_glu, s5_b_glu, ret_norm_w, gla_w_alpha, gla_b_alpha, gla_norm_w, w_branch, w_out, ln1_w, ln1_b, router_w, router_b, moe_w_gate_up, moe_b_gate_up, moe_w_down, moe_b_down, ln2_w, ln2_b, ple_w_gate, ple_w_proj, ln3_w, ln3_b):
    s5_rows = 256
    bsz, seq, d = x.shape
    t = bsz * seq
    h = x.reshape(t, d).astype(F32)
    hb = h.astype(BF16)
    expert_params = _expert_params(moe_w_gate_up, moe_b_gate_up, moe_w_down, moe_b_down)
    for i in range(DEPTH):
        merged = _token_mixer(h, hb, bsz, seq, positions, w_in[i], ssd_conv_w[i], ssd_conv_b[i], ssd_dt_bias[i],
                              ssd_a_log[i], ssd_d[i], ssd_norm_w[i], s5_lambda_re[i], s5_lambda_im[i],
                              s5_log_dt[i], s5_b_re[i], s5_b_im[i], s5_c_re[i], s5_c_im[i], s5_d[i],
                              s5_w_glu[i], s5_b_glu[i], ret_norm_w[i], gla_w_alpha[i], gla_b_alpha[i],
                              gla_norm_w[i], w_branch[i], s5_rows)
        h, h_packed = _proj_ln(merged, w_out[i], h, ln1_w[i], ln1_b[i])
        h, hb = _moe(h, h_packed, router_w[i], router_b[i], expert_params, i, ln2_w[i], ln2_b[i])
        h, hb = _ple(hb, h, p[i].reshape(t, PLE_DIM), ple_w_gate[i], ple_w_proj[i], ln3_w[i], ln3_b[i])
    return h.reshape(bsz, seq, d).astype(x.dtype)
```

```python
import functools
import math

import numpy as np
import jax
import jax.numpy as jnp
from jax import lax
from jax.experimental import pallas as pl
from jax.experimental.pallas import tpu as pltpu

F32 = jnp.float32
BF16 = jnp.bfloat16
HIGHEST = lax.Precision.HIGHEST

D_MODEL = 2048
DEPTH = 2
PLE_DIM = 256
N_BRANCH = 4
BRANCH_WIDTH = 1024

SSD_HEADS = 16
SSD_HEAD_DIM = 64
SSD_INNER = SSD_HEADS * SSD_HEAD_DIM
SSD_GROUPS = 2
SSD_STATE = 128
SSD_CONV = 4
SSD_CHUNK = 128

S5_WIDTH = 1024
S5_GROUP = 16
S5_GROUPS = S5_WIDTH // S5_GROUP
S5_STATE = 64
S5_CHUNK = 16
S5_TILE_GROUPS = 8

RET_HEADS = 8
RET_QK_DIM = 64
RET_V_DIM = 128
RET_QK = RET_HEADS * RET_QK_DIM
RET_WIDTH = RET_HEADS * RET_V_DIM
RET_CHUNK = 128
ROPE_BASE = 10000.0

GLA_HEADS = 4
GLA_QK_DIM = 128
GLA_V_DIM = 256
GLA_QK = GLA_HEADS * GLA_QK_DIM
GLA_WIDTH = GLA_HEADS * GLA_V_DIM
GLA_GATE_RANK = 16
GLA_TAU = 16.0
GLA_CHUNK = 64
GLA_BLOCK = 128

N_EXPERTS = 32
TOP_K = 4
D_EXPERT = 1024
SWIGLU_ALPHA = 1.702
SWIGLU_LIMIT = 7.0
MOE_BLOCK = 512

DEEPNORM_ALPHA = (2 * DEPTH) ** 0.25
LN_EPS = 1e-5
RMS_EPS = 1e-6

LANES = 128
VMEM_LIMIT = 56 * 1024 * 1024

_OFF_Z = 0
_OFF_XS = 1024
_OFF_BC = 2048
_OFF_DT = 2560
_OFF_U = 2576
_OFF_RQ = 3600
_OFF_RK = 4112
_OFF_RV = 4624
_OFF_RG = 5648
_OFF_GQ = 6672
_OFF_GK = 7184
_OFF_GV = 7696
_OFF_GR = 8720
_OFF_GC = 9744
_OFF_GATE = 9760

_CB_Z, _CB_XS, _CB_RV, _CB_RG, _CB_GV, _CB_GR = 8, 9, 10, 11, 12, 13
_CB_BC, _CB_RQ, _CB_RK, _CB_GQ, _CB_GK = 28, 29, 30, 31, 32
N_MAIN = 16896


def _main_weight(w_in):
    def cols(off, n):
        return w_in[:, off:off + n].astype(BF16)

    def rotary(off):
        half = RET_QK_DIM // 2
        w = cols(off, RET_QK).reshape(D_MODEL, RET_HEADS // 2, 2, 2, half)
        return w.transpose(0, 1, 3, 2, 4).reshape(D_MODEL, RET_QK)

    return jnp.concatenate([
        cols(_OFF_GATE, N_BRANCH * D_MODEL), cols(_OFF_Z, 1024), cols(_OFF_XS, 1024), cols(_OFF_RV, 1024),
        cols(_OFF_RG, 1024), cols(_OFF_GV, 1024), cols(_OFF_GR, 1024), cols(_OFF_BC, 512),
        rotary(_OFF_RQ), rotary(_OFF_RK), cols(_OFF_GQ, 512), cols(_OFF_GK, 512)], axis=1)


def _cparams(sem):
    return pltpu.CompilerParams(dimension_semantics=sem, vmem_limit_bytes=VMEM_LIMIT)


def _silu(x):
    return x * jax.nn.sigmoid(x)


def _softplus(x):
    return jnp.maximum(x, 0.0) + jnp.log1p(jnp.exp(-jnp.abs(x)))


def _layer_norm(x, w, b):
    mu = jnp.mean(x, -1, keepdims=True)
    xc = x - mu
    var = jnp.mean(xc * xc, -1, keepdims=True)
    return xc * lax.rsqrt(var + LN_EPS) * w + b


def _mm_kernel(a_ref, b_ref, o_ref):
    o_ref[...] = jnp.dot(a_ref[...], b_ref[...], preferred_element_type=F32).astype(o_ref.dtype)


def _matmul(a, b, out_dtype, tm, tn, name="matmul"):
    m, k = a.shape
    n = b.shape[1]
    tm, tn = min(tm, m), min(tn, n)
    return pl.pallas_call(
        _mm_kernel,
        grid=(m // tm, n // tn),
        in_specs=[pl.BlockSpec((tm, k), lambda i, j: (i, 0)),
                  pl.BlockSpec((k, tn), lambda i, j: (0, j))],
        out_specs=pl.BlockSpec((tm, tn), lambda i, j: (i, j)),
        out_shape=jax.ShapeDtypeStruct((m, n), out_dtype),
        compiler_params=_cparams(("parallel", "parallel")),
        name=name,
    )(a, b)


def _split_bf16(x):
    hi = x.astype(BF16)
    return hi, (x - hi.astype(F32)).astype(BF16)


def _dot3(a, b, dims):
    a_hi, a_lo = _split_bf16(a)
    b_hi, b_lo = _split_bf16(b)
    dg = functools.partial(lax.dot_general, dimension_numbers=(dims, ((), ())), preferred_element_type=F32)
    return dg(a_hi, b_hi) + (dg(a_hi, b_lo) + dg(a_lo, b_hi))


def _mm3_kernel(a_ref, b_ref, o_ref):
    o_ref[...] = _dot3(a_ref[...], b_ref[...], ((1,), (0,)))


def _matmul_f32(a, b, tm, name):
    m, k = a.shape
    n = b.shape[1]
    tm = min(tm, m)
    return pl.pallas_call(
        _mm3_kernel,
        grid=(m // tm,),
        in_specs=[pl.BlockSpec((tm, k), lambda i: (i, 0)), pl.BlockSpec((k, n), lambda i: (0, 0))],
        out_specs=pl.BlockSpec((tm, n), lambda i: (i, 0)),
        out_shape=jax.ShapeDtypeStruct((m, n), F32),
        compiler_params=_cparams(("parallel",)),
        name=name,
    )(a, b)


def _causal_conv(x, tail, w, b):
    n_tap = SSD_CONV
    acc = x * w[n_tap - 1:n_tap] + b
    x8 = x[0:8]
    acc8 = x8 * w[n_tap - 1:n_tap] + b
    row8 = lax.broadcasted_iota(jnp.int32, x8.shape, 0)
    for s in range(1, n_tap):
        wk = w[n_tap - 1 - s:n_tap - s]
        acc = acc + pltpu.roll(x, s, 0) * wk
        v8 = jnp.where(row8 < s, pltpu.roll(tail, s, 0), pltpu.roll(x8, s, 0))
        acc8 = acc8 + v8 * wk
    return jnp.concatenate([acc8, acc[8:]], axis=0)


def _ssd_kernel(z_ref, xs_ref, bc_ref, sm_ref, cwx_ref, cwbc_ref, cbx_ref, cbbc_ref, dtb_ref,
                nega_ref, dsk_ref, nw_ref, exp_ref, tril_ref, o_ref, tailx_ref, tailbc_ref, ht_ref):
    L = SSD_CHUNK

    @pl.when(pl.program_id(1) == 0)
    def _():
        tailx_ref[...] = jnp.zeros_like(tailx_ref)
        tailbc_ref[...] = jnp.zeros_like(tailbc_ref)
        ht_ref[...] = jnp.zeros_like(ht_ref)

    x_raw = xs_ref[...].astype(F32)
    bc_raw = bc_ref[...].astype(F32)
    xs = _silu(_causal_conv(x_raw, tailx_ref[...], cwx_ref[...], cbx_ref[...]))
    bc = _silu(_causal_conv(bc_raw, tailbc_ref[...], cwbc_ref[...], cbbc_ref[...]))
    tailx_ref[...] = x_raw[L - 8:L]
    tailbc_ref[...] = bc_raw[L - 8:L]

    dt = _softplus(sm_ref[...] + dtb_ref[...])
    loga = dt * nega_ref[...]
    acs = jnp.dot(tril_ref[...], loga, preferred_element_type=F32, precision=HIGHEST)
    acs_t = acs.T
    expand = exp_ref[...]
    dt_e = jnp.dot(dt, expand, preferred_element_type=F32, precision=HIGHEST)
    acs_e = jnp.dot(acs, expand, preferred_element_type=F32, precision=HIGHEST)
    last_e = acs_e[L - 1:L]
    xdt = xs * dt_e
    xdt_b = xdt.astype(BF16)
    xw_b = (xdt * jnp.exp(last_e - acs_e)).astype(BF16)
    dec_in = jnp.exp(acs_e)
    dec_chunk = jnp.exp(last_e)

    row = lax.broadcasted_iota(jnp.int32, (L, L), 0)
    col = lax.broadcasted_iota(jnp.int32, (L, L), 1)
    causal = row >= col
    lo_half = col < SSD_HEAD_DIM
    heads_per_group = SSD_HEADS // SSD_GROUPS
    n_bc = SSD_GROUPS * SSD_STATE

    y_tiles = []
    for g in range(SSD_GROUPS):
        bm = bc[:, g * SSD_STATE:(g + 1) * SSD_STATE]
        cm_b = bc[:, n_bc + g * SSD_STATE:n_bc + (g + 1) * SSD_STATE].astype(BF16)
        bm_t_b = bm.T.astype(BF16)
        cb = jnp.dot(cm_b, bm_t_b, preferred_element_type=F32)
        for j in range(g * heads_per_group // 2, (g + 1) * heads_per_group // 2):
            sl = slice(j * LANES, (j + 1) * LANES)
            xt = xdt_b[:, sl]
            y_pair = None
            for sub in range(2):
                h = 2 * j + sub
                seg = acs[:, h:h + 1] - acs_t[h:h + 1, :]
                m = (cb * jnp.exp(jnp.where(causal, seg, -jnp.inf))).astype(BF16)
                xh = jnp.where(lo_half if sub == 0 else ~lo_half, xt, jnp.zeros_like(xt))
                yd = jnp.dot(m, xh, preferred_element_type=F32)
                y_pair = yd if y_pair is None else y_pair + yd
            h_t = ht_ref[:, sl]
            y_off = jnp.dot(cm_b, h_t.astype(BF16), preferred_element_type=F32) * dec_in[:, sl]
            st = jnp.dot(bm_t_b, xw_b[:, sl], preferred_element_type=F32)
            ht_ref[:, sl] = h_t * dec_chunk[:, sl] + st
            y_tiles.append(y_pair + y_off)
    y = jnp.concatenate(y_tiles, axis=1) + dsk_ref[...] * xs
    y = y * _silu(z_ref[...].astype(F32))
    gw = SSD_INNER // SSD_GROUPS
    outs = []
    for g in range(SSD_GROUPS):
        yg = y[:, g * gw:(g + 1) * gw]
        ms = jnp.mean(yg * yg, -1, keepdims=True)
        outs.append(yg * lax.rsqrt(ms + RMS_EPS) * nw_ref[:, g * gw:(g + 1) * gw])
    o_ref[...] = jnp.concatenate(outs, axis=1).astype(o_ref.dtype)


def _ssd_mixer(proj, small, bsz, seq, conv_w, conv_b, dt_bias, a_log, d_skip, norm_w):
    L = SSD_CHUNK
    nc = seq // L
    pad_lanes = LANES - SSD_HEADS
    cw = jnp.pad(conv_w.astype(F32), ((0, 8 - SSD_CONV), (0, 0)))
    cwx, cwbc = cw[:, :SSD_INNER], cw[:, SSD_INNER:]
    cb = conv_b.astype(F32)[None, :]
    cbx, cbbc = cb[:, :SSD_INNER], cb[:, SSD_INNER:]
    dtb = jnp.pad(dt_bias.astype(F32), (0, pad_lanes))[None, :]
    nega = jnp.pad(-jnp.exp(a_log.astype(F32)), (0, pad_lanes))[None, :]
    dsk = jnp.repeat(d_skip.astype(F32), SSD_HEAD_DIM)[None, :]
    nw = norm_w.astype(F32)[None, :]
    expand = (np.arange(LANES)[:, None] == (np.arange(SSD_INNER)[None, :] // SSD_HEAD_DIM)).astype(np.float32)
    tril = np.tril(np.ones((L, L), np.float32))
    rows = lambda b, c: b * nc + c
    full = lambda shape: pl.BlockSpec(shape, lambda b, c: (0,) * len(shape))
    return pl.pallas_call(
        _ssd_kernel,
        grid=(bsz, nc),
        in_specs=[
            pl.BlockSpec((L, 1024), lambda b, c: (rows(b, c), _CB_Z)),
            pl.BlockSpec((L, 1024), lambda b, c: (rows(b, c), _CB_XS)),
            pl.BlockSpec((L, 512), lambda b, c: (rows(b, c), _CB_BC)),
            pl.BlockSpec((L, LANES), lambda b, c: (rows(b, c), 0)),
            full((8, SSD_INNER)), full((8, 512)), full((1, SSD_INNER)), full((1, 512)),
            full((1, LANES)), full((1, LANES)), full((1, SSD_INNER)), full((1, SSD_INNER)),
            full((LANES, SSD_INNER)), full((L, L)),
        ],
        out_specs=pl.BlockSpec((L, SSD_INNER), lambda b, c: (rows(b, c), 0)),
        out_shape=jax.ShapeDtypeStruct((bsz * seq, SSD_INNER), BF16),
        scratch_shapes=[pltpu.VMEM((8, SSD_INNER), F32), pltpu.VMEM((8, 512), F32),
                        pltpu.VMEM((SSD_STATE, SSD_INNER), F32)],
        compiler_params=_cparams(("parallel", "arbitrary")),
        name="ssd_mixer",
    )(proj, proj, proj, small, cwx, cwbc, cbx, cbbc, dtb, nega, dsk, nw, jnp.asarray(expand), jnp.asarray(tril))


def _s5_scan_kernel(u_ref, bd_ref, w_ref, v_ref, a1_ref, a2_ref, y_ref, toep_ref, sc_ref, hin_ref, hc_ref, *, rows):
    R = rows

    @pl.when((pl.program_id(1) == 0) & (pl.program_id(2) == 0))
    def _():
        toep_ref[...] = jnp.zeros_like(toep_ref)
        for s in range(S5_CHUNK):
            for l in range(s, S5_CHUNK):
                toep_ref[s * LANES:(s + 1) * LANES, l * LANES:(l + 1) * LANES] = bd_ref[0, l - s]

    @pl.when(pl.program_id(2) == 0)
    def _():
        hc_ref[...] = jnp.zeros_like(hc_ref)

    parts = [u_ref[pl.ds(s, R, stride=S5_CHUNK), :].astype(BF16) for s in range(S5_CHUNK)]
    uf = jnp.concatenate(parts, axis=1)
    y_intra = jnp.dot(uf, toep_ref[...], preferred_element_type=F32)
    sc_ref[...] = jnp.dot(uf, w_ref[0], preferred_element_type=F32)
    a1 = a1_ref[0]
    a2 = a2_ref[0]
    half = S5_TILE_GROUPS * S5_STATE

    def step(r, h):
        hin_ref[pl.ds(r, 1), :] = h
        return a1 * h + a2 * pltpu.roll(h, half, 1) + sc_ref[pl.ds(r, 1), :]

    hc_ref[...] = lax.fori_loop(0, R, step, hc_ref[...])
    y = y_intra + jnp.dot(hin_ref[...].astype(BF16), v_ref[0], preferred_element_type=F32)
    for l in range(S5_CHUNK):
        y_ref[pl.ds(l, R, stride=S5_CHUNK), :] = y[:, l * LANES:(l + 1) * LANES]


def _s5_tables(lam_re, lam_im, log_dt, b_re, b_im, c_re, c_im):
    f32 = F32
    lc = S5_CHUNK
    tg = S5_TILE_GROUPS
    nt = S5_GROUPS // tg
    lr = jnp.minimum(lam_re.astype(f32), -1e-4)[None, :]
    li = lam_im.astype(f32)[None, :]
    dt = jnp.exp(log_dt.astype(f32))[:, None]
    mag = jnp.exp(lr * dt)
    ab_re, ab_im = mag * jnp.cos(li * dt), mag * jnp.sin(li * dt)
    den = lr * lr + li * li
    nr, ni = ab_re - 1.0, ab_im
    f_re, f_im = (nr * lr + ni * li) / den, (ni * lr - nr * li) / den
    b_re, b_im = b_re.astype(f32), b_im.astype(f32)
    bb_re = f_re[..., None] * b_re - f_im[..., None] * b_im
    bb_im = f_re[..., None] * b_im + f_im[..., None] * b_re
    tau = jnp.arange(lc + 1, dtype=f32)[:, None, None]
    pw_mag = jnp.exp(tau * (lr * dt)[None])
    pw_re = pw_mag * jnp.cos(tau * (li * dt)[None])
    pw_im = pw_mag * jnp.sin(tau * (li * dt)[None])
    c_re, c_im = c_re.astype(f32), c_im.astype(f32)
    cl_re = c_re[None] * pw_re[:, :, None, :] - c_im[None] * pw_im[:, :, None, :]
    cl_im = c_re[None] * pw_im[:, :, None, :] + c_im[None] * pw_re[:, :, None, :]
    hp = lax.Precision.HIGHEST
    kern = (jnp.einsum('tgop,gpi->tgoi', cl_re[:lc], bb_re, precision=hp)
            - jnp.einsum('tgop,gpi->tgoi', cl_im[:lc], bb_im, precision=hp))
    eye = jnp.eye(tg, dtype=f32)
    toep = jnp.einsum('tTgoi,gh->Ttgiho', kern.reshape(lc, nt, tg, S5_GROUP, S5_GROUP), eye)
    toep = toep.reshape(nt, lc, LANES, LANES)
    rev = pw_re[lc - 1 - np.arange(lc)], pw_im[lc - 1 - np.arange(lc)]
    wr = rev[0][..., None] * bb_re[None] - rev[1][..., None] * bb_im[None]
    wi = rev[0][..., None] * bb_im[None] + rev[1][..., None] * bb_re[None]
    wst = jnp.stack([wr, wi], 0).reshape(2, lc, nt, tg, S5_STATE, S5_GROUP)
    w_in = jnp.einsum('rsTgpi,gh->Tsgirhp', wst, eye).reshape(nt, lc * LANES, 2 * tg * S5_STATE)
    vst = jnp.stack([cl_re[1:], -cl_im[1:]], 0).reshape(2, lc, nt, tg, S5_GROUP, S5_STATE)
    v_out = jnp.einsum('rlTgop,gh->Trgplho', vst, eye).reshape(nt, 2 * tg * S5_STATE, lc * LANES)
    a_re = pw_re[lc].reshape(nt, 1, tg * S5_STATE)
    a_im = pw_im[lc].reshape(nt, 1, tg * S5_STATE)
    a1 = jnp.concatenate([a_re, a_re], -1)
    a2 = jnp.concatenate([-a_im, a_im], -1)
    return toep.astype(BF16), w_in.astype(BF16), v_out.astype(BF16), a1, a2


def _s5_scan(u, bsz, seq, tables, rows):
    toep, w_in, v_out, a1, a2 = tables
    nt = S5_GROUPS // S5_TILE_GROUPS
    lc = S5_CHUNK
    rows = min(rows, seq // lc)
    nblk = seq // (lc * rows)
    tok = rows * lc
    ns = 2 * S5_TILE_GROUPS * S5_STATE
    return pl.pallas_call(
        functools.partial(_s5_scan_kernel, rows=rows),
        grid=(nt, bsz, nblk),
        in_specs=[
            pl.BlockSpec((tok, LANES), lambda t, b, r: (b * nblk + r, t)),
            pl.BlockSpec((1, lc, LANES, LANES), lambda t, b, r: (t, 0, 0, 0)),
            pl.BlockSpec((1, lc * LANES, ns), lambda t, b, r: (t, 0, 0)),
            pl.BlockSpec((1, ns, lc * LANES), lambda t, b, r: (t, 0, 0)),
            pl.BlockSpec((1, 1, ns), lambda t, b, r: (t, 0, 0)),
            pl.BlockSpec((1, 1, ns), lambda t, b, r: (t, 0, 0)),
        ],
        out_specs=pl.BlockSpec((tok, LANES), lambda t, b, r: (b * nblk + r, t)),
        out_shape=jax.ShapeDtypeStruct((bsz * seq, S5_WIDTH), F32),
        scratch_shapes=[pltpu.VMEM((lc * LANES, lc * LANES), BF16), pltpu.VMEM((rows, ns), F32),
                        pltpu.VMEM((rows, ns), F32), pltpu.VMEM((1, ns), F32)],
        compiler_params=_cparams(("arbitrary", "arbitrary", "arbitrary")),
        name="s5_scan",
    )(u, toep, w_in, v_out, a1, a2)


def _s5_post_kernel(y_ref, u_ref, d_ref, w_ref, b_ref, o_ref):
    g = jax.nn.gelu(y_ref[...] + d_ref[...] * u_ref[...], approximate=True)
    gate = jnp.dot(g.astype(BF16), w_ref[...], preferred_element_type=F32) + b_ref[...]
    o_ref[...] = (g * jax.nn.sigmoid(gate)).astype(o_ref.dtype)


def _s5_post(y, u, d_skip, w_glu, b_glu, tm=512):
    t = y.shape[0]
    tm = min(tm, t)
    row = pl.BlockSpec((tm, S5_WIDTH), lambda i: (i, 0))
    vec = pl.BlockSpec((1, S5_WIDTH), lambda i: (0, 0))
    return pl.pallas_call(
        _s5_post_kernel,
        grid=(t // tm,),
        in_specs=[row, row, vec, pl.BlockSpec((S5_WIDTH, S5_WIDTH), lambda i: (0, 0)), vec],
        out_specs=row,
        out_shape=jax.ShapeDtypeStruct((t, S5_WIDTH), BF16),
        compiler_params=_cparams(("parallel",)),
        name="s5_post",
    )(y, u, d_skip.astype(F32)[None, :], w_glu.astype(BF16), b_glu.astype(F32)[None, :])


def _ret_kernel(q_ref, k_ref, v_ref, g_ref, cos_ref, sin_ref, intra_ref, dend_ref, osc_ref, nw_ref,
                o_ref, r_ref, *, chunk_decay):
    L = RET_CHUNK

    @pl.when(pl.program_id(1) == 0)
    def _():
        r_ref[...] = jnp.zeros_like(r_ref)

    cos = cos_ref[...]
    sin = sin_ref[...]
    lane = lax.broadcasted_iota(jnp.int32, (L, LANES), 1)
    rowi = lax.broadcasted_iota(jnp.int32, (LANES, L), 0)
    half = RET_QK_DIM // 2
    outs = []
    for t in range(RET_HEADS // 2):
        sl = slice(t * LANES, (t + 1) * LANES)
        qt = q_ref[:, sl].astype(F32)
        kt = k_ref[:, sl].astype(F32)
        qr = qt * cos + pltpu.roll(qt, 64, 1) * sin
        kr = (kt * cos + pltpu.roll(kt, 64, 1) * sin) * (RET_QK_DIM ** -0.5)
        kr_b = kr.astype(BF16)
        kr_t = kr.T
        for sub in range(2):
            h = 2 * t + sub
            vs = slice(h * RET_V_DIM, (h + 1) * RET_V_DIM)
            qm = jnp.where(((lane // half) % 2) == sub, qr, 0.0).astype(BF16)
            s = lax.dot_general(qm, kr_b, (((1,), (1,)), ((), ())), preferred_element_type=F32)
            p = (s * intra_ref[h]).astype(BF16)
            vh = v_ref[:, vs]
            state = r_ref[h]
            y_in = jnp.dot(p, vh, preferred_element_type=F32)
            y_off = jnp.dot(qm, state.astype(BF16), preferred_element_type=F32) * osc_ref[:, vs]
            ktm = jnp.where(((rowi // half) % 2) == sub, kr_t * dend_ref[h:h + 1, :], 0.0).astype(BF16)
            r_ref[h] = state * chunk_decay[h] + jnp.dot(ktm, vh, preferred_element_type=F32)
            y = y_in + y_off
            mu = jnp.mean(y, -1, keepdims=True)
            yc = y - mu
            var = jnp.mean(yc * yc, -1, keepdims=True)
            yn = yc * lax.rsqrt(var + LN_EPS) * nw_ref[:, vs]
            outs.append(_silu(g_ref[:, vs].astype(F32)) * yn)
    o_ref[...] = jnp.concatenate(outs, axis=1).astype(o_ref.dtype)


def _ret_mixer(proj, bsz, seq, positions, norm_w):
    L = RET_CHUNK
    nc = seq // L
    half = RET_QK_DIM // 2
    inv_freq = 1.0 / (ROPE_BASE ** (jnp.arange(half, dtype=F32) / half))
    ang = positions.astype(F32).reshape(bsz * seq, 1) * inv_freq[None, :]
    cos32, sin32 = jnp.cos(ang), jnp.sin(ang)
    cos_t = jnp.concatenate([cos32] * 4, axis=1)
    sin_t = jnp.concatenate([-sin32, -sin32, sin32, sin32], axis=1)
    log_gamma = np.log1p(-np.exp2(-5.0 - np.arange(RET_HEADS, dtype=np.float64)))
    pos = np.arange(L, dtype=np.float64)
    rel = pos[:, None] - pos[None, :]
    intra = np.where(rel[None] >= 0, np.exp(rel[None] * log_gamma[:, None, None]), 0.0).astype(np.float32)
    dend = np.exp((L - 1 - pos)[None, :] * log_gamma[:, None]).astype(np.float32)
    osc = np.repeat(np.exp((pos + 1)[:, None] * log_gamma[None, :]), RET_V_DIM, axis=1).astype(np.float32)
    chunk_decay = tuple(float(v) for v in np.exp(L * log_gamma))
    rows = lambda b, c: b * nc + c
    full = lambda shape: pl.BlockSpec(shape, lambda b, c: (0,) * len(shape))
    return pl.pallas_call(
        functools.partial(_ret_kernel, chunk_decay=chunk_decay),
        grid=(bsz, nc),
        in_specs=[
            pl.BlockSpec((L, 512), lambda b, c: (rows(b, c), _CB_RQ)),
            pl.BlockSpec((L, 512), lambda b, c: (rows(b, c), _CB_RK)),
            pl.BlockSpec((L, 1024), lambda b, c: (rows(b, c), _CB_RV)),
            pl.BlockSpec((L, 1024), lambda b, c: (rows(b, c), _CB_RG)),
            pl.BlockSpec((L, LANES), lambda b, c: (rows(b, c), 0)),
            pl.BlockSpec((L, LANES), lambda b, c: (rows(b, c), 0)),
            full((RET_HEADS, L, L)), full((RET_HEADS, L)), full((L, RET_WIDTH)), full((1, RET_WIDTH)),
        ],
        out_specs=pl.BlockSpec((L, RET_WIDTH), lambda b, c: (rows(b, c), 0)),
        out_shape=jax.ShapeDtypeStruct((bsz * seq, RET_WIDTH), BF16),
        scratch_shapes=[pltpu.VMEM((RET_HEADS, LANES, RET_V_DIM), F32)],
        compiler_params=_cparams(("parallel", "arbitrary")),
        name="retention_mixer",
    )(proj, proj, proj, proj, cos_t, sin_t, jnp.asarray(intra), jnp.asarray(dend), jnp.asarray(osc),
      norm_w.astype(F32)[None, :])


def _gla_kernel(q_ref, k_ref, v_ref, r_ref, sm_ref, wa_ref, ba_ref, tblk_ref, nw_ref, o_ref, s_ref):
    LB = GLA_BLOCK
    LC = GLA_CHUNK

    @pl.when(pl.program_id(1) == 0)
    def _():
        s_ref[...] = jnp.zeros_like(s_ref)

    logit = jnp.dot(sm_ref[...], wa_ref[...], preferred_element_type=F32, precision=HIGHEST) + ba_ref[...]
    la = (jnp.minimum(logit, 0.0) - jnp.log1p(jnp.exp(-jnp.abs(logit)))) * (1.0 / GLA_TAU)
    b = jnp.dot(tblk_ref[...], la, preferred_element_type=F32, precision=HIGHEST)
    rowf = lax.broadcasted_iota(jnp.int32, (LB, GLA_QK), 0)
    b_end = jnp.where(rowf < LC, b[LC - 1:LC], b[LB - 1:LB])
    q = q_ref[...].astype(F32) * (GLA_QK_DIM ** -0.5)
    k = k_ref[...].astype(F32)
    q_dec = q * jnp.exp(b)
    k_inv = (k * jnp.exp(-b)).astype(BF16)
    k_dec = k * jnp.exp(b_end - b)
    row = lax.broadcasted_iota(jnp.int32, (LB, LB), 0)
    col = lax.broadcasted_iota(jnp.int32, (LB, LB), 1)
    blk_causal = (row >= col) & ((row // LC) == (col // LC))
    first_rows = row < LC
    first_vrows = lax.broadcasted_iota(jnp.int32, (LB, GLA_V_DIM), 0) < LC
    outs = []
    for h in range(GLA_HEADS):
        sl = slice(h * GLA_QK_DIM, (h + 1) * GLA_QK_DIM)
        vs = slice(h * GLA_V_DIM, (h + 1) * GLA_V_DIM)
        qd = q_dec[:, sl]
        qd_b = qd.astype(BF16)
        att = lax.dot_general(qd_b, k_inv[:, sl], (((1,), (1,)), ((), ())), preferred_element_type=F32)
        att = jnp.where(blk_causal, att, 0.0).astype(BF16)
        vh = v_ref[:, vs]
        y = jnp.dot(att, vh, preferred_element_type=F32)
        s0 = s_ref[h]
        kd_t = k_dec[:, sl].T.astype(BF16)
        kv0 = jnp.dot(kd_t, jnp.where(first_vrows, vh, jnp.zeros_like(vh)), preferred_element_type=F32)
        kv1 = jnp.dot(kd_t, jnp.where(first_vrows, jnp.zeros_like(vh), vh), preferred_element_type=F32)
        b_t = b[:, sl].T
        s1 = s0 * jnp.exp(b_t[:, LC - 1:LC]) + kv0
        s_ref[h] = s1 * jnp.exp(b_t[:, LB - 1:LB]) + kv1
        y = y + jnp.dot(jnp.where(first_rows, qd, 0.0).astype(BF16), s0.astype(BF16), preferred_element_type=F32)
        y = y + jnp.dot(jnp.where(first_rows, 0.0, qd).astype(BF16), s1.astype(BF16), preferred_element_type=F32)
        ms = jnp.mean(y * y, -1, keepdims=True)
        yn = y * lax.rsqrt(ms + RMS_EPS) * nw_ref[:, vs]
        outs.append(yn * _silu(r_ref[:, vs].astype(F32)))
    o_ref[...] = jnp.concatenate(outs, axis=1).astype(o_ref.dtype)


def _gla_mixer(proj, small, bsz, seq, w_alpha, b_alpha, norm_w):
    LB = GLA_BLOCK
    nb = seq // LB
    wa = jnp.zeros((LANES, GLA_QK), F32).at[SSD_HEADS:SSD_HEADS + GLA_GATE_RANK].set(w_alpha.astype(F32))
    idx = np.arange(LB)
    tblk = ((idx[:, None] >= idx[None, :]) & ((idx[:, None] // GLA_CHUNK) == (idx[None, :] // GLA_CHUNK)))
    rows = lambda b, c: b * nb + c
    full = lambda shape: pl.BlockSpec(shape, lambda b, c: (0,) * len(shape))
    return pl.pallas_call(
        _gla_kernel,
        grid=(bsz, nb),
        in_specs=[
            pl.BlockSpec((LB, 512), lambda b, c: (rows(b, c), _CB_GQ)),
            pl.BlockSpec((LB, 512), lambda b, c: (rows(b, c), _CB_GK)),
            pl.BlockSpec((LB, 1024), lambda b, c: (rows(b, c), _CB_GV)),
            pl.BlockSpec((LB, 1024), lambda b, c: (rows(b, c), _CB_GR)),
            pl.BlockSpec((LB, LANES), lambda b, c: (rows(b, c), 0)),
            full((LANES, GLA_QK)), full((1, GLA_QK)), full((LB, LB)), full((1, GLA_WIDTH)),
        ],
        out_specs=pl.BlockSpec((LB, GLA_WIDTH), lambda b, c: (rows(b, c), 0)),
        out_shape=jax.ShapeDtypeStruct((bsz * seq, GLA_WIDTH), BF16),
        scratch_shapes=[pltpu.VMEM((GLA_HEADS, GLA_QK_DIM, GLA_V_DIM), F32)],
        compiler_params=_cparams(("parallel", "arbitrary")),
        name="gla_mixer",
    )(proj, proj, proj, proj, small, wa, b_alpha.astype(F32)[None, :], jnp.asarray(tblk.astype(np.float32)),
      norm_w.astype(F32)[None, :])


def _merge_kernel(ya_ref, yb_ref, yc_ref, yd_ref, g0_ref, g1_ref, g2_ref, g3_ref, wb_ref, o_ref):
    acc = None
    for n, (y_ref, g_ref) in enumerate(((ya_ref, g0_ref), (yb_ref, g1_ref), (yc_ref, g2_ref), (yd_ref, g3_ref))):
        br = jnp.dot(y_ref[...], wb_ref[n], preferred_element_type=F32)
        term = jax.nn.sigmoid(g_ref[...].astype(F32)) * br
        acc = term if acc is None else acc + term
    o_ref[...] = acc.astype(o_ref.dtype)


def _merge(ys, proj, w_branch, tm=1024, tn=512):
    t = ys[0].shape[0]
    tm = min(tm, t)
    nj = D_MODEL // tn
    yspec = pl.BlockSpec((tm, BRANCH_WIDTH), lambda i, j: (i, 0))
    gspec = lambda n: pl.BlockSpec((tm, tn), lambda i, j: (i, n * nj + j))
    return pl.pallas_call(
        _merge_kernel,
        grid=(t // tm, nj),
        in_specs=[yspec] * 4 + [gspec(n) for n in range(N_BRANCH)]
                 + [pl.BlockSpec((N_BRANCH, BRANCH_WIDTH, tn), lambda i, j: (0, 0, j))],
        out_specs=pl.BlockSpec((tm, tn), lambda i, j: (i, j)),
        out_shape=jax.ShapeDtypeStruct((t, D_MODEL), BF16),
        compiler_params=_cparams(("parallel", "parallel")),
        name="branch_merge",
    )(*ys, proj, proj, proj, proj, w_branch.astype(BF16))


def _proj_ln_kernel(m_ref, w_ref, h_ref, lw_ref, lb_ref, o_ref, op_ref):
    mix = jnp.dot(m_ref[...], w_ref[...], preferred_element_type=F32)
    out = _layer_norm(DEEPNORM_ALPHA * h_ref[...] + mix, lw_ref[...], lb_ref[...])
    o_ref[...] = out
    _store_token_tiles(op_ref, 0, _pack_rows(out.astype(BF16)))


def _proj_ln(merged, w_out, h, ln_w, ln_b, tm=512):
    t = h.shape[0]
    tm = min(tm, t)
    row = pl.BlockSpec((tm, D_MODEL), lambda i: (i, 0))
    vec = pl.BlockSpec((1, D_MODEL), lambda i: (0, 0))
    return pl.pallas_call(
        _proj_ln_kernel,
        grid=(t // tm,),
        in_specs=[row, pl.BlockSpec((D_MODEL, D_MODEL), lambda i: (0, 0)), row, vec, vec],
        out_specs=[row, pl.BlockSpec((tm * TILE_ROWS, LANES), lambda i: (i, 0))],
        out_shape=[jax.ShapeDtypeStruct((t, D_MODEL), F32), jax.ShapeDtypeStruct((t * TILE_ROWS, LANES), jnp.uint32)],
        compiler_params=_cparams(("parallel",)),
        name="out_proj_ln",
    )(merged, w_out.astype(BF16), h, ln_w.astype(F32)[None, :], ln_b.astype(F32)[None, :])


def _router_kernel(h_ref, w_ref, b_ref, idx_ref, gate_ref, cnt_ref, run_ref):
    @pl.when(pl.program_id(0) == 0)
    def _():
        run_ref[...] = jnp.zeros_like(run_ref)

    logits = _dot3(w_ref[...], h_ref[...], ((1,), (1,))) + b_ref[...]
    eid = lax.broadcasted_iota(jnp.int32, logits.shape, 0)
    vals = logits
    run = run_ref[...]
    top_v, top_i = [], []
    for _ in range(TOP_K):
        m = jnp.max(vals, axis=0, keepdims=True)
        sel = jnp.min(jnp.where(vals == m, eid, N_EXPERTS), axis=0, keepdims=True)
        hit = eid == sel
        top_v.append(m)
        top_i.append(sel)
        vals = jnp.where(hit, -jnp.inf, vals)
        run = run + jnp.sum(jnp.where(hit, 1.0, 0.0), axis=1, keepdims=True)
    run_ref[...] = run
    ex = [jnp.exp(v - top_v[0]) for v in top_v]
    den = ex[0] + ex[1] + ex[2] + ex[3]
    zi = jnp.zeros_like(top_i[0])
    zf = jnp.zeros_like(den)
    idx_ref[...] = jnp.concatenate(top_i + [zi] * (8 - TOP_K), axis=0)
    gate_ref[...] = jnp.concatenate([e / den for e in ex] + [zf] * (8 - TOP_K), axis=0)
    cnt_ref[...] = jnp.broadcast_to(run, cnt_ref.shape).astype(jnp.int32)


def _router(h, router_w, router_b, tm=512):
    t = h.shape[0]
    tm = min(tm, t)
    tok = pl.BlockSpec((8, tm), lambda i: (0, i))
    return pl.pallas_call(
        _router_kernel,
        grid=(t // tm,),
        in_specs=[pl.BlockSpec((tm, D_MODEL), lambda i: (i, 0)),
                  pl.BlockSpec((N_EXPERTS, D_MODEL), lambda i: (0, 0)),
                  pl.BlockSpec((N_EXPERTS, 1), lambda i: (0, 0))],
        out_specs=[tok, tok, pl.BlockSpec((N_EXPERTS, LANES), lambda i: (0, 0))],
        out_shape=[jax.ShapeDtypeStruct((8, t), jnp.int32), jax.ShapeDtypeStruct((8, t), F32),
                   jax.ShapeDtypeStruct((N_EXPERTS, LANES), jnp.int32)],
        scratch_shapes=[pltpu.VMEM((N_EXPERTS, 1), F32)],
        compiler_params=_cparams(("arbitrary",)),
        name="router_topk",
    )(h, router_w.astype(F32).T, router_b.astype(F32)[:, None])


def _pack_rows(x):
    n = x.shape[1] // 2
    lo = pltpu.bitcast(x[:, :n].astype(F32), jnp.uint32) >> 16
    hi = pltpu.bitcast(x[:, n:].astype(F32), jnp.uint32)
    return lo | hi


def _unpack_rows(w):
    lo = pltpu.bitcast(w << 16, F32)
    hi = pltpu.bitcast(w & jnp.uint32(0xFFFF0000), F32)
    return lo, hi


TILE_ROWS = (D_MODEL // 2) // LANES


def _store_token_tiles(ref, first, words):
    m = words.shape[0]
    for s in range(TILE_ROWS):
        ref[pl.ds(first * TILE_ROWS + s, m, stride=TILE_ROWS), :] = words[:, s * LANES:(s + 1) * LANES]


def _load_token_tiles(ref, first, m):
    return jnp.concatenate([ref[pl.ds(first * TILE_ROWS + s, m, stride=TILE_ROWS), :] for s in range(TILE_ROWS)],
                           axis=1)


def _expert_kernel(be_ref, nb_ref, tab_ref, hp_ref, wgu_ref, bgu_ref, wd_ref, bd_ref, y_ref,
                   idx_ref, xbuf_ref, ybuf_ref, isem, gsem, ssem):
    i = pl.program_id(0)
    n_used = nb_ref[0]
    blk = MOE_BLOCK
    tr = TILE_ROWS

    def idx_copy(b):
        return pltpu.make_async_copy(tab_ref.at[b], idx_ref.at[b % 4], isem.at[b % 4])

    def tile(r):
        return pl.ds(r * tr if isinstance(r, int) else pl.multiple_of(r * tr, tr), tr)

    def gather_start(b, r):
        src = pl.multiple_of(idx_ref[b % 4, r], tr)
        pltpu.make_async_copy(hp_ref.at[pl.ds(src, tr)], xbuf_ref.at[b % 2, tile(r)], gsem.at[b % 2]).start()

    def scatter_start(b, r):
        dst = pl.multiple_of(idx_ref[b % 4, blk + r], tr)
        pltpu.make_async_copy(ybuf_ref.at[b % 2, tile(r)], y_ref.at[pl.ds(dst, tr)], ssem.at[b % 2]).start(priority=1)

    def gather_wait(b):
        pltpu.make_async_copy(hp_ref.at[pl.ds(0, blk * tr)], xbuf_ref.at[b % 2], gsem.at[b % 2]).wait()

    def scatter_wait(b):
        pltpu.make_async_copy(ybuf_ref.at[b % 2], y_ref.at[pl.ds(0, blk * tr)], ssem.at[b % 2]).wait()

    def ffn(b):
        lo, hi = _unpack_rows(_load_token_tiles(xbuf_ref.at[b % 2], 0, blk))
        xb = jnp.concatenate([lo.astype(BF16), hi.astype(BF16)], axis=1)
        hgu = jnp.dot(xb, wgu_ref[0], preferred_element_type=F32) + bgu_ref[0]
        g = jnp.minimum(hgu[:, :D_EXPERT], SWIGLU_LIMIT)
        u = jnp.clip(hgu[:, D_EXPERT:], -SWIGLU_LIMIT, SWIGLU_LIMIT)
        act = (u + 1.0) * (g * jax.nn.sigmoid(SWIGLU_ALPHA * g))
        out = jnp.dot(act.astype(BF16), wd_ref[0], preferred_element_type=F32) + bd_ref[0]
        packed = _pack_rows(out.astype(BF16))

        @pl.when(i >= 2)
        def _():
            scatter_wait(i - 2)

        _store_token_tiles(ybuf_ref.at[b % 2], 0, packed)

    @pl.when(i == 0)
    def _():
        idx_copy(0).start()
        ybuf_ref[1] = jnp.zeros(ybuf_ref.shape[1:], ybuf_ref.dtype)
        spare = pltpu.make_async_copy(ybuf_ref.at[1], y_ref.at[pl.ds(y_ref.shape[0] - blk * tr, blk * tr)],
                                      ssem.at[1])
        spare.start()
        idx_copy(0).wait()
        lax.fori_loop(0, blk, lambda r, c: (gather_start(0, r), c)[1], 0, unroll=8)
        idx_copy(1).start()
        spare.wait()

    @pl.when(i <= n_used)
    def _():
        idx_copy(i + 1).wait()
        gather_wait(i)

    @pl.when(i < n_used)
    def _():
        idx_copy(i + 2).start()

    @pl.when(i == 0)
    def _():
        for r in range(blk):
            gather_start(1, r)
        ffn(0)

    @pl.when((i > 0) & (i < n_used))
    def _():
        for r in range(blk):
            gather_start(i + 1, r)
            scatter_start(i - 1, r)
        ffn(i)

    @pl.when(i == n_used)
    def _():
        @pl.when(i >= 2)
        def _():
            scatter_wait(i - 2)

        lax.fori_loop(0, blk, lambda r, c: (scatter_start(i - 1, r), c)[1], 0, unroll=8)
        scatter_wait(i - 1)


def _expert_params(w_gate_up, b_gate_up, w_down, b_down):
    n = w_gate_up.shape[0] * N_EXPERTS
    return (w_gate_up.astype(BF16).reshape(n, D_MODEL, 2 * D_EXPERT), b_gate_up.astype(F32).reshape(n, 1, 2 * D_EXPERT),
            w_down.astype(BF16).reshape(n, D_EXPERT, D_MODEL), b_down.astype(F32).reshape(n, 1, D_MODEL))


def _experts(h_packed, table, block_expert, n_used, expert_params):
    w_gate_up, b_gate_up, w_down, b_down = expert_params
    blk = MOE_BLOCK
    n_blocks = block_expert.shape[0]
    n_out = TOP_K * (h_packed.shape[0] // TILE_ROWS) + blk
    grid_spec = pltpu.PrefetchScalarGridSpec(
        num_scalar_prefetch=2,
        grid=(n_blocks,),
        in_specs=[
            pl.BlockSpec(memory_space=pl.ANY),
            pl.BlockSpec(memory_space=pl.ANY),
            pl.BlockSpec((1, D_MODEL, 2 * D_EXPERT), lambda i, be, nb: (be[i], 0, 0)),
            pl.BlockSpec((1, 1, 2 * D_EXPERT), lambda i, be, nb: (be[i], 0, 0)),
            pl.BlockSpec((1, D_EXPERT, D_MODEL), lambda i, be, nb: (be[i], 0, 0)),
            pl.BlockSpec((1, 1, D_MODEL), lambda i, be, nb: (be[i], 0, 0)),
        ],
        out_specs=pl.BlockSpec(memory_space=pl.ANY),
        scratch_shapes=[pltpu.SMEM((4, 2 * blk), jnp.int32), pltpu.VMEM((2, blk * TILE_ROWS, LANES), jnp.uint32),
                        pltpu.VMEM((2, blk * TILE_ROWS, LANES), jnp.uint32), pltpu.SemaphoreType.DMA((4,)),
                        pltpu.SemaphoreType.DMA((2,)), pltpu.SemaphoreType.DMA((2,))],
    )
    return pl.pallas_call(
        _expert_kernel,
        grid_spec=grid_spec,
        out_shape=jax.ShapeDtypeStruct((n_out * TILE_ROWS, LANES), jnp.uint32),
        compiler_params=_cparams(("arbitrary",)),
        name="expert_ffn",
    )(block_expert, n_used, table, h_packed, w_gate_up, b_gate_up, w_down, b_down)


def _combine_kernel(y0_ref, y1_ref, y2_ref, y3_ref, gate_ref, h_ref, lw_ref, lb_ref, o_ref, ob_ref, *, tm):
    rows = min(tm, 16)
    for r0 in range(0, tm, rows):
        rs = slice(r0, r0 + rows)
        gates = gate_ref[rs, :]
        ffn = None
        for k, y_ref in enumerate((y0_ref, y1_ref, y2_ref, y3_ref)):
            lo, hi = _unpack_rows(_load_token_tiles(y_ref, r0, rows))
            term = gates[:, k:k + 1] * jnp.concatenate([lo, hi], axis=1)
            ffn = term if ffn is None else ffn + term
        out = _layer_norm(DEEPNORM_ALPHA * h_ref[rs, :] + ffn, lw_ref[...], lb_ref[...])
        o_ref[rs, :] = out
        ob_ref[rs, :] = out.astype(BF16)


def _combine(y, gates, h, ln_w, ln_b, tm=512):
    t = h.shape[0]
    tm = min(tm, t)
    nt = t // tm
    row = pl.BlockSpec((tm, D_MODEL), lambda i: (i, 0))
    vec = pl.BlockSpec((1, D_MODEL), lambda i: (0, 0))
    yspec = lambda k: pl.BlockSpec((tm * TILE_ROWS, LANES), lambda i: (k * nt + i, 0))
    return pl.pallas_call(
        functools.partial(_combine_kernel, tm=tm),
        grid=(nt,),
        in_specs=[yspec(k) for k in range(TOP_K)] + [pl.BlockSpec((tm, 8), lambda i: (i, 0)), row, vec, vec],
        out_specs=[row, row],
        out_shape=[jax.ShapeDtypeStruct((t, D_MODEL), F32), jax.ShapeDtypeStruct((t, D_MODEL), BF16)],
        compiler_params=_cparams(("parallel",)),
        name="moe_combine_ln",
    )(y, y, y, y, gates, h, ln_w.astype(F32)[None, :], ln_b.astype(F32)[None, :])


def _moe(h, h_packed, router_w, router_b, expert_params, layer, ln_w, ln_b):
    t = h.shape[0]
    blk = MOE_BLOCK
    top_idx, top_gate, counts = _router(h, router_w, router_b)
    counts = counts[:, 0]
    n_assign = TOP_K * t
    n_blocks = n_assign // blk + N_EXPERTS
    a_ids = jnp.arange(n_assign, dtype=jnp.int32)
    a_sorted = jnp.sort(top_idx[:TOP_K].reshape(-1) * n_assign + a_ids) % n_assign
    padded = (counts + blk - 1) // blk * blk
    pad_end = jnp.cumsum(padded)
    grp_end = jnp.cumsum(counts)
    blocks = jnp.arange(n_blocks, dtype=jnp.int32)
    block_expert = jnp.sum(((pad_end // blk)[None, :] <= blocks[:, None]).astype(jnp.int32), axis=1)
    block_expert = jnp.minimum(block_expert, N_EXPERTS - 1)
    n_used = (pad_end[-1] // blk).astype(jnp.int32).reshape(1)
    is_e = block_expert[:, None] == jnp.arange(N_EXPERTS, dtype=jnp.int32)[None, :]
    pick = lambda v: jnp.sum(jnp.where(is_e, v[None, :], 0), axis=1)
    offset = blocks * blk - pick(pad_end - padded)
    first = pick(grp_end - counts) + offset
    n_valid = jnp.clip(pick(counts) - offset, 0, blk)
    j = jnp.arange(blk, dtype=jnp.int32)[None, :]
    valid = j < n_valid[:, None]
    a_rows = a_sorted[jnp.clip(first[:, None] + j, 0, n_assign - 1)]
    tok_rows = jnp.where(valid, a_rows % t, 0)
    out_rows = jnp.where(valid, a_rows, n_assign + j)
    table = (jnp.concatenate([tok_rows, out_rows], axis=1) * TILE_ROWS).astype(jnp.int32)
    table = jnp.concatenate([table, jnp.zeros((2, 2 * blk), jnp.int32)], axis=0)
    y = _experts(h_packed, table, block_expert + layer * N_EXPERTS, n_used, expert_params)
    return _combine(y, top_gate.T, h, ln_w, ln_b)


def _ple_kernel(hb_ref, h_ref, p_ref, wg_ref, wp_ref, lw_ref, lb_ref, o_ref, ob_ref):
    gate = jax.nn.sigmoid(jnp.dot(hb_ref[...], wg_ref[...], preferred_element_type=F32))
    emb = jnp.dot(p_ref[...].astype(BF16), wp_ref[...], preferred_element_type=F32)
    out = _layer_norm(DEEPNORM_ALPHA * h_ref[...] + gate * emb, lw_ref[...], lb_ref[...])
    o_ref[...] = out
    ob_ref[...] = out.astype(BF16)


def _ple(hb, h, p_i, w_gate, w_proj, ln_w, ln_b, tm=512):
    t = h.shape[0]
    tm = min(tm, t)
    row = pl.BlockSpec((tm, D_MODEL), lambda i: (i, 0))
    vec = pl.BlockSpec((1, D_MODEL), lambda i: (0, 0))
    return pl.pallas_call(
        _ple_kernel,
        grid=(t // tm,),
        in_specs=[row, row, pl.BlockSpec((tm, PLE_DIM), lambda i: (i, 0)),
                  pl.BlockSpec((D_MODEL, D_MODEL), lambda i: (0, 0)),
                  pl.BlockSpec((PLE_DIM, D_MODEL), lambda i: (0, 0)), vec, vec],
        out_specs=[row, row],
        out_shape=[jax.ShapeDtypeStruct((t, D_MODEL), F32), jax.ShapeDtypeStruct((t, D_MODEL), BF16)],
        compiler_params=_cparams(("parallel",)),
        name="ple_ln",
    )(hb, h, p_i, w_gate.astype(BF16), w_proj.astype(BF16), ln_w.astype(F32)[None, :], ln_b.astype(F32)[None, :])


def _token_mixer(h, hb, bsz, seq, positions, w_in, ssd_conv_w, ssd_conv_b, ssd_dt_bias, ssd_a_log, ssd_d,
                 ssd_norm_w, s5_lambda_re, s5_lambda_im, s5_log_dt, s5_b_re, s5_b_im, s5_c_re, s5_c_im, s5_d,
                 s5_w_glu, s5_b_glu, ret_norm_w, gla_w_alpha, gla_b_alpha, gla_norm_w, w_branch, s5_rows):
    w_main = _main_weight(w_in)
    w_u = w_in[:, _OFF_U:_OFF_U + S5_WIDTH].astype(BF16)
    w_small = jnp.concatenate([w_in[:, _OFF_DT:_OFF_DT + SSD_HEADS], w_in[:, _OFF_GC:_OFF_GC + GLA_GATE_RANK],
                               jnp.zeros((D_MODEL, LANES - SSD_HEADS - GLA_GATE_RANK), w_in.dtype)], axis=1)
    proj = _matmul(hb, w_main, BF16, 1024, 1536, name="in_proj_main")
    u = _matmul(hb, w_u, F32, 1024, 1024, name="in_proj_s5")
    small = _matmul_f32(h, w_small.astype(F32), 512, name="in_proj_small")
    y_a = _ssd_mixer(proj, small, bsz, seq, ssd_conv_w, ssd_conv_b, ssd_dt_bias, ssd_a_log, ssd_d, ssd_norm_w)
    tables = _s5_tables(s5_lambda_re, s5_lambda_im, s5_log_dt, s5_b_re, s5_b_im, s5_c_re, s5_c_im)
    y_b = _s5_post(_s5_scan(u, bsz, seq, tables, s5_rows), u, s5_d, s5_w_glu, s5_b_glu)
    y_c = _ret_mixer(proj, bsz, seq, positions, ret_norm_w)
    y_d = _gla_mixer(proj, small, bsz, seq, gla_w_alpha, gla_b_alpha, gla_norm_w)
    return _merge((y_a, y_b, y_c, y_d), proj, w_branch)


def kernel(x, p, positions, w_in, ssd_conv_w, ssd_conv_b, ssd_dt_bias, ssd_a_log, ssd_d, ssd_norm_w, s5_lambda_re, s5_lambda_im, s5_log_dt, s5_b_re, s5_b_im, s5_c_re, s5_c_im, s5_d, s5_w_glu, s5_b_glu, ret_norm_w, gla_w_alpha, gla_b_alpha, gla_norm_w, w_branch, w_out, ln1_w, ln1_b, router_w, router_b, moe_w_gate_up, moe_b_gate_up, moe_w_down, moe_b_down, ln2_w, ln2_b, ple_w_gate, ple_w_proj, ln3_w, ln3_b):
    s5_rows = 256
    bsz, seq, d = x.shape
    t = bsz * seq
    h = x.reshape(t, d).astype(F32)
    hb = h.astype(BF16)
    expert_params = _expert_params(moe_w_gate_up, moe_b_gate_up, moe_w_down, moe_b_down)
    for i in range(DEPTH):
        merged = _token_mixer(h, hb, bsz, seq, positions, w_in[i], ssd_conv_w[i], ssd_conv_b[i], ssd_dt_bias[i],
                              ssd_a_log[i], ssd_d[i], ssd_norm_w[i], s5_lambda_re[i], s5_lambda_im[i],
                              s5_log_dt[i], s5_b_re[i], s5_b_im[i], s5_c_re[i], s5_c_im[i], s5_d[i],
                              s5_w_glu[i], s5_b_glu[i], ret_norm_w[i], gla_w_alpha[i], gla_b_alpha[i],
                              gla_norm_w[i], w_branch[i], s5_rows)
        h, h_packed = _proj_ln(merged, w_out[i], h, ln1_w[i], ln1_b[i])
        h, hb = _moe(h, h_packed, router_w[i], router_b[i], expert_params, i, ln2_w[i], ln2_b[i])
        h, hb = _ple(hb, h, p[i].reshape(t, PLE_DIM), ple_w_gate[i], ple_w_proj[i], ln3_w[i], ln3_b[i])
    return h.reshape(bsz, seq, d).astype(x.dtype)
```

```python
import functools
import math

import numpy as np
import jax
import jax.numpy as jnp
from jax import lax
from jax.experimental import pallas as pl
from jax.experimental.pallas import tpu as pltpu

F32 = jnp.float32
BF16 = jnp.bfloat16
HIGHEST = lax.Precision.HIGHEST

D_MODEL = 2048
DEPTH = 2
PLE_DIM = 256
N_BRANCH = 4
BRANCH_WIDTH = 1024

SSD_HEADS = 16
SSD_HEAD_DIM = 64
SSD_INNER = SSD_HEADS * SSD_HEAD_DIM
SSD_GROUPS = 2
SSD_STATE = 128
SSD_CONV = 4
SSD_CHUNK = 128

S5_WIDTH = 1024
S5_GROUP = 16
S5_GROUPS = S5_WIDTH // S5_GROUP
S5_STATE = 64
S5_CHUNK = 16
S5_TILE_GROUPS = 8
S5_SCAN_ROWS = 8

RET_HEADS = 8
RET_QK_DIM = 64
RET_V_DIM = 128
RET_QK = RET_HEADS * RET_QK_DIM
RET_WIDTH = RET_HEADS * RET_V_DIM
RET_CHUNK = 128
ROPE_BASE = 10000.0

GLA_HEADS = 4
GLA_QK_DIM = 128
GLA_V_DIM = 256
GLA_QK = GLA_HEADS * GLA_QK_DIM
GLA_WIDTH = GLA_HEADS * GLA_V_DIM
GLA_GATE_RANK = 16
GLA_TAU = 16.0
GLA_CHUNK = 64
GLA_BLOCK = 128

N_EXPERTS = 32
TOP_K = 4
D_EXPERT = 1024
SWIGLU_ALPHA = 1.702
SWIGLU_LIMIT = 7.0
MOE_BLOCK = 512

DEEPNORM_ALPHA = (2 * DEPTH) ** 0.25
LN_EPS = 1e-5
RMS_EPS = 1e-6

LANES = 128
VMEM_LIMIT = 56 * 1024 * 1024

_OFF_Z = 0
_OFF_XS = 1024
_OFF_BC = 2048
_OFF_DT = 2560
_OFF_U = 2576
_OFF_RQ = 3600
_OFF_RK = 4112
_OFF_RV = 4624
_OFF_RG = 5648
_OFF_GQ = 6672
_OFF_GK = 7184
_OFF_GV = 7696
_OFF_GR = 8720
_OFF_GC = 9744
_OFF_GATE = 9760

_CB_Z, _CB_XS, _CB_RV, _CB_RG, _CB_GV, _CB_GR = 8, 9, 10, 11, 12, 13
_CB_BC, _CB_RQ, _CB_RK, _CB_GQ, _CB_GK = 28, 29, 30, 31, 32
N_MAIN = 16896


def _main_weight(w_in):
    def cols(off, n):
        return w_in[:, off:off + n].astype(BF16)

    def rotary(off):
        half = RET_QK_DIM // 2
        w = cols(off, RET_QK).reshape(D_MODEL, RET_HEADS // 2, 2, 2, half)
        return w.transpose(0, 1, 3, 2, 4).reshape(D_MODEL, RET_QK)

    return jnp.concatenate([
        cols(_OFF_GATE, N_BRANCH * D_MODEL), cols(_OFF_Z, 1024), cols(_OFF_XS, 1024), cols(_OFF_RV, 1024),
        cols(_OFF_RG, 1024), cols(_OFF_GV, 1024), cols(_OFF_GR, 1024), cols(_OFF_BC, 512),
        rotary(_OFF_RQ), rotary(_OFF_RK), cols(_OFF_GQ, 512), cols(_OFF_GK, 512)], axis=1)


def _cparams(sem):
    return pltpu.CompilerParams(dimension_semantics=sem, vmem_limit_bytes=VMEM_LIMIT)


def _silu(x):
    return x * jax.nn.sigmoid(x)


def _softplus(x):
    return jnp.maximum(x, 0.0) + jnp.log1p(jnp.exp(-jnp.abs(x)))


def _layer_norm(x, w, b):
    mu = jnp.mean(x, -1, keepdims=True)
    xc = x - mu
    var = jnp.mean(xc * xc, -1, keepdims=True)
    return xc * lax.rsqrt(var + LN_EPS) * w + b


def _mm_kernel(a_ref, b_ref, o_ref):
    o_ref[...] = jnp.dot(a_ref[...], b_ref[...], preferred_element_type=F32).astype(o_ref.dtype)


def _matmul(a, b, out_dtype, tm, tn, name="matmul"):
    m, k = a.shape
    n = b.shape[1]
    tm, tn = min(tm, m), min(tn, n)
    return pl.pallas_call(
        _mm_kernel,
        grid=(m // tm, n // tn),
        in_specs=[pl.BlockSpec((tm, k), lambda i, j: (i, 0)),
                  pl.BlockSpec((k, tn), lambda i, j: (0, j))],
        out_specs=pl.BlockSpec((tm, tn), lambda i, j: (i, j)),
        out_shape=jax.ShapeDtypeStruct((m, n), out_dtype),
        compiler_params=_cparams(("parallel", "parallel")),
        name=name,
    )(a, b)


def _split_bf16(x):
    hi = x.astype(BF16)
    return hi, (x - hi.astype(F32)).astype(BF16)


def _dot3(a, b, dims):
    a_hi, a_lo = _split_bf16(a)
    b_hi, b_lo = _split_bf16(b)
    dg = functools.partial(lax.dot_general, dimension_numbers=(dims, ((), ())), preferred_element_type=F32)
    return dg(a_hi, b_hi) + (dg(a_hi, b_lo) + dg(a_lo, b_hi))


def _mm3_kernel(a_ref, b_ref, o_ref):
    o_ref[...] = _dot3(a_ref[...], b_ref[...], ((1,), (0,)))


def _matmul_f32(a, b, tm, name):
    m, k = a.shape
    n = b.shape[1]
    tm = min(tm, m)
    return pl.pallas_call(
        _mm3_kernel,
        grid=(m // tm,),
        in_specs=[pl.BlockSpec((tm, k), lambda i: (i, 0)), pl.BlockSpec((k, n), lambda i: (0, 0))],
        out_specs=pl.BlockSpec((tm, n), lambda i: (i, 0)),
        out_shape=jax.ShapeDtypeStruct((m, n), F32),
        compiler_params=_cparams(("parallel",)),
        name=name,
    )(a, b)


MIXER_CHUNKS = 2


def _chunked_kernel(chunk_fn, n_row, n_const, length):
    def kernel(*refs):
        row_refs, const_refs = refs[:n_row], refs[n_row:n_row + n_const]
        o_ref, scratch = refs[n_row + n_const], refs[n_row + n_const + 1:]

        @pl.when(pl.program_id(1) == 0)
        def _():
            for ref in scratch:
                ref[...] = jnp.zeros_like(ref)

        for c in range(MIXER_CHUNKS):
            rows = pl.ds(c * length, length)
            chunk_fn(*[r.at[rows] for r in row_refs], *const_refs, o_ref.at[rows], *scratch)

    return kernel


def _causal_conv(x, tail, w, b):
    n_tap = SSD_CONV
    acc = x * w[n_tap - 1:n_tap] + b
    x8 = x[0:8]
    acc8 = x8 * w[n_tap - 1:n_tap] + b
    row8 = lax.broadcasted_iota(jnp.int32, x8.shape, 0)
    for s in range(1, n_tap):
        wk = w[n_tap - 1 - s:n_tap - s]
        acc = acc + pltpu.roll(x, s, 0) * wk
        v8 = jnp.where(row8 < s, pltpu.roll(tail, s, 0), pltpu.roll(x8, s, 0))
        acc8 = acc8 + v8 * wk
    return jnp.concatenate([acc8, acc[8:]], axis=0)


def _ssd_chunk(z_ref, xs_ref, bc_ref, sm_ref, cwx_ref, cwbc_ref, cbx_ref, cbbc_ref, dtb_ref,
               nega_ref, dsk_ref, nw_ref, exp_ref, tril_ref, o_ref, tailx_ref, tailbc_ref, ht_ref):
    L = SSD_CHUNK

    x_raw = xs_ref[...].astype(F32)
    bc_raw = bc_ref[...].astype(F32)
    xs = _silu(_causal_conv(x_raw, tailx_ref[...], cwx_ref[...], cbx_ref[...]))
    bc = _silu(_causal_conv(bc_raw, tailbc_ref[...], cwbc_ref[...], cbbc_ref[...]))
    tailx_ref[...] = x_raw[L - 8:L]
    tailbc_ref[...] = bc_raw[L - 8:L]

    dt = _softplus(sm_ref[...] + dtb_ref[...])
    loga = dt * nega_ref[...]
    acs = jnp.dot(tril_ref[...], loga, preferred_element_type=F32, precision=HIGHEST)
    acs_t = acs.T
    expand = exp_ref[...]
    dt_e = jnp.dot(dt, expand, preferred_element_type=F32, precision=HIGHEST)
    acs_e = jnp.dot(acs, expand, preferred_element_type=F32, precision=HIGHEST)
    last_e = acs_e[L - 1:L]
    xdt = xs * dt_e
    xdt_b = xdt.astype(BF16)
    xw_b = (xdt * jnp.exp(last_e - acs_e)).astype(BF16)
    dec_in = jnp.exp(acs_e)
    dec_chunk = jnp.exp(last_e)

    row = lax.broadcasted_iota(jnp.int32, (L, L), 0)
    col = lax.broadcasted_iota(jnp.int32, (L, L), 1)
    causal = row >= col
    lo_half = col < SSD_HEAD_DIM
    heads_per_group = SSD_HEADS // SSD_GROUPS
    n_bc = SSD_GROUPS * SSD_STATE

    y_tiles = []
    for g in range(SSD_GROUPS):
        bm = bc[:, g * SSD_STATE:(g + 1) * SSD_STATE]
        cm_b = bc[:, n_bc + g * SSD_STATE:n_bc + (g + 1) * SSD_STATE].astype(BF16)
        bm_t_b = bm.T.astype(BF16)
        cb = jnp.dot(cm_b, bm_t_b, preferred_element_type=F32)
        for j in range(g * heads_per_group // 2, (g + 1) * heads_per_group // 2):
            sl = slice(j * LANES, (j + 1) * LANES)
            xt = xdt_b[:, sl]
            y_pair = None
            for sub in range(2):
                h = 2 * j + sub
                seg = acs[:, h:h + 1] - acs_t[h:h + 1, :]
                m = (cb * jnp.exp(jnp.where(causal, seg, -jnp.inf))).astype(BF16)
                xh = jnp.where(lo_half if sub == 0 else ~lo_half, xt, jnp.zeros_like(xt))
                yd = jnp.dot(m, xh, preferred_element_type=F32)
                y_pair = yd if y_pair is None else y_pair + yd
            h_t = ht_ref[:, sl]
            y_off = jnp.dot(cm_b, h_t.astype(BF16), preferred_element_type=F32) * dec_in[:, sl]
            st = jnp.dot(bm_t_b, xw_b[:, sl], preferred_element_type=F32)
            ht_ref[:, sl] = h_t * dec_chunk[:, sl] + st
            y_tiles.append(y_pair + y_off)
    y = jnp.concatenate(y_tiles, axis=1) + dsk_ref[...] * xs
    y = y * _silu(z_ref[...].astype(F32))
    gw = SSD_INNER // SSD_GROUPS
    outs = []
    for g in range(SSD_GROUPS):
        yg = y[:, g * gw:(g + 1) * gw]
        ms = jnp.mean(yg * yg, -1, keepdims=True)
        outs.append(yg * lax.rsqrt(ms + RMS_EPS) * nw_ref[:, g * gw:(g + 1) * gw])
    o_ref[...] = jnp.concatenate(outs, axis=1).astype(o_ref.dtype)


def _ssd_mixer(proj, small, bsz, seq, conv_w, conv_b, dt_bias, a_log, d_skip, norm_w):
    L = SSD_CHUNK
    lb = min(L * MIXER_CHUNKS, seq)
    nc = seq // lb
    pad_lanes = LANES - SSD_HEADS
    cw = jnp.pad(conv_w.astype(F32), ((0, 8 - SSD_CONV), (0, 0)))
    cwx, cwbc = cw[:, :SSD_INNER], cw[:, SSD_INNER:]
    cb = conv_b.astype(F32)[None, :]
    cbx, cbbc = cb[:, :SSD_INNER], cb[:, SSD_INNER:]
    dtb = jnp.pad(dt_bias.astype(F32), (0, pad_lanes))[None, :]
    nega = jnp.pad(-jnp.exp(a_log.astype(F32)), (0, pad_lanes))[None, :]
    dsk = jnp.repeat(d_skip.astype(F32), SSD_HEAD_DIM)[None, :]
    nw = norm_w.astype(F32)[None, :]
    expand = (np.arange(LANES)[:, None] == (np.arange(SSD_INNER)[None, :] // SSD_HEAD_DIM)).astype(np.float32)
    tril = np.tril(np.ones((L, L), np.float32))
    rows = lambda b, c: b * nc + c
    full = lambda shape: pl.BlockSpec(shape, lambda b, c: (0,) * len(shape))
    return pl.pallas_call(
        _chunked_kernel(_ssd_chunk, 4, 10, L),
        grid=(bsz, nc),
        in_specs=[
            pl.BlockSpec((lb, 1024), lambda b, c: (rows(b, c), _CB_Z)),
            pl.BlockSpec((lb, 1024), lambda b, c: (rows(b, c), _CB_XS)),
            pl.BlockSpec((lb, 512), lambda b, c: (rows(b, c), _CB_BC)),
            pl.BlockSpec((lb, LANES), lambda b, c: (rows(b, c), 0)),
            full((8, SSD_INNER)), full((8, 512)), full((1, SSD_INNER)), full((1, 512)),
            full((1, LANES)), full((1, LANES)), full((1, SSD_INNER)), full((1, SSD_INNER)),
            full((LANES, SSD_INNER)), full((L, L)),
        ],
        out_specs=pl.BlockSpec((lb, SSD_INNER), lambda b, c: (rows(b, c), 0)),
        out_shape=jax.ShapeDtypeStruct((bsz * seq, SSD_INNER), BF16),
        scratch_shapes=[pltpu.VMEM((8, SSD_INNER), F32), pltpu.VMEM((8, 512), F32),
                        pltpu.VMEM((SSD_STATE, SSD_INNER), F32)],
        compiler_params=_cparams(("parallel", "arbitrary")),
        name="ssd_mixer",
    )(proj, proj, proj, small, cwx, cwbc, cbx, cbbc, dtb, nega, dsk, nw, jnp.asarray(expand), jnp.asarray(tril))


def _s5_scan_kernel(u_ref, bd_ref, w_ref, v_ref, pw_ref, y_ref, toep_ref, sc_ref, hin_ref, hc_ref, *, rows):
    R = rows

    @pl.when((pl.program_id(1) == 0) & (pl.program_id(2) == 0))
    def _():
        toep_ref[...] = jnp.zeros_like(toep_ref)
        for s in range(S5_CHUNK):
            for l in range(s, S5_CHUNK):
                toep_ref[s * LANES:(s + 1) * LANES, l * LANES:(l + 1) * LANES] = bd_ref[0, l - s]

    @pl.when(pl.program_id(2) == 0)
    def _():
        hc_ref[...] = jnp.zeros_like(hc_ref)

    parts = [u_ref[pl.ds(s, R, stride=S5_CHUNK), :].astype(BF16) for s in range(S5_CHUNK)]
    uf = jnp.concatenate(parts, axis=1)
    y_intra = jnp.dot(uf, toep_ref[...], preferred_element_type=F32)
    sc_ref[...] = jnp.dot(uf, w_ref[0], preferred_element_type=F32)
    pw = pw_ref[0]
    half = S5_TILE_GROUPS * S5_STATE
    grp = S5_SCAN_ROWS
    row = lax.broadcasted_iota(jnp.int32, (grp, pw.shape[1]), 0)

    def times(m, z):
        return pw[m:m + 1] * z + pw[grp + 1 + m:grp + 2 + m] * pltpu.roll(z, half, 1)

    def step(g, h):
        rows8 = pl.ds(pl.multiple_of(g * grp, grp), grp)
        x = sc_ref[rows8, :]
        for sh in (1, 2, 4):
            x = x + jnp.where(row >= sh, times(sh, pltpu.roll(x, sh, 0)), 0.0)
        hb = jnp.broadcast_to(h, x.shape)
        a_j_h = pw[0:grp] * hb + pw[grp + 1:2 * grp + 1] * pltpu.roll(hb, half, 1)
        hin_ref[rows8, :] = a_j_h + jnp.where(row >= 1, pltpu.roll(x, 1, 0), 0.0)
        return times(grp, h) + x[grp - 1:grp]

    hc_ref[...] = lax.fori_loop(0, R // grp, step, hc_ref[...])
    y = y_intra + jnp.dot(hin_ref[...].astype(BF16), v_ref[0], preferred_element_type=F32)
    for l in range(S5_CHUNK):
        y_ref[pl.ds(l, R, stride=S5_CHUNK), :] = y[:, l * LANES:(l + 1) * LANES]


def _s5_tables(lam_re, lam_im, log_dt, b_re, b_im, c_re, c_im):
    f32 = F32
    lc = S5_CHUNK
    tg = S5_TILE_GROUPS
    nt = S5_GROUPS // tg
    lr = jnp.minimum(lam_re.astype(f32), -1e-4)[None, :]
    li = lam_im.astype(f32)[None, :]
    dt = jnp.exp(log_dt.astype(f32))[:, None]
    mag = jnp.exp(lr * dt)
    ab_re, ab_im = mag * jnp.cos(li * dt), mag * jnp.sin(li * dt)
    den = lr * lr + li * li
    nr, ni = ab_re - 1.0, ab_im
    f_re, f_im = (nr * lr + ni * li) / den, (ni * lr - nr * li) / den
    b_re, b_im = b_re.astype(f32), b_im.astype(f32)
    bb_re = f_re[..., None] * b_re - f_im[..., None] * b_im
    bb_im = f_re[..., None] * b_im + f_im[..., None] * b_re
    tau = jnp.arange(lc + 1, dtype=f32)[:, None, None]
    pw_mag = jnp.exp(tau * (lr * dt)[None])
    pw_re = pw_mag * jnp.cos(tau * (li * dt)[None])
    pw_im = pw_mag * jnp.sin(tau * (li * dt)[None])
    c_re, c_im = c_re.astype(f32), c_im.astype(f32)
    cl_re = c_re[None] * pw_re[:, :, None, :] - c_im[None] * pw_im[:, :, None, :]
    cl_im = c_re[None] * pw_im[:, :, None, :] + c_im[None] * pw_re[:, :, None, :]
    hp = lax.Precision.HIGHEST
    kern = (jnp.einsum('tgop,gpi->tgoi', cl_re[:lc], bb_re, precision=hp)
            - jnp.einsum('tgop,gpi->tgoi', cl_im[:lc], bb_im, precision=hp))
    eye = jnp.eye(tg, dtype=f32)
    toep = jnp.einsum('tTgoi,gh->Ttgiho', kern.reshape(lc, nt, tg, S5_GROUP, S5_GROUP), eye)
    toep = toep.reshape(nt, lc, LANES, LANES)
    rev = pw_re[lc - 1 - np.arange(lc)], pw_im[lc - 1 - np.arange(lc)]
    wr = rev[0][..., None] * bb_re[None] - rev[1][..., None] * bb_im[None]
    wi = rev[0][..., None] * bb_im[None] + rev[1][..., None] * bb_re[None]
    wst = jnp.stack([wr, wi], 0).reshape(2, lc, nt, tg, S5_STATE, S5_GROUP)
    w_in = jnp.einsum('rsTgpi,gh->Tsgirhp', wst, eye).reshape(nt, lc * LANES, 2 * tg * S5_STATE)
    vst = jnp.stack([cl_re[1:], -cl_im[1:]], 0).reshape(2, lc, nt, tg, S5_GROUP, S5_STATE)
    v_out = jnp.einsum('rlTgop,gh->Trgplho', vst, eye).reshape(nt, 2 * tg * S5_STATE, lc * LANES)
    m = (lc * jnp.arange(S5_SCAN_ROWS + 1, dtype=f32))[:, None, None]
    am_mag = jnp.exp(m * (lr * dt)[None])
    am_re = (am_mag * jnp.cos(m * (li * dt)[None])).reshape(S5_SCAN_ROWS + 1, nt, tg * S5_STATE).transpose(1, 0, 2)
    am_im = (am_mag * jnp.sin(m * (li * dt)[None])).reshape(S5_SCAN_ROWS + 1, nt, tg * S5_STATE).transpose(1, 0, 2)
    pw = jnp.concatenate([jnp.concatenate([am_re, am_re], -1), jnp.concatenate([-am_im, am_im], -1)], axis=1)
    pw = jnp.pad(pw, ((0, 0), (0, 24 - pw.shape[1]), (0, 0)))
    return toep.astype(BF16), w_in.astype(BF16), v_out.astype(BF16), pw


def _s5_scan(u, bsz, seq, tables, rows):
    toep, w_in, v_out, pw = tables
    nt = S5_GROUPS // S5_TILE_GROUPS
    lc = S5_CHUNK
    rows = min(rows, seq // lc)
    nblk = seq // (lc * rows)
    tok = rows * lc
    ns = 2 * S5_TILE_GROUPS * S5_STATE
    return pl.pallas_call(
        functools.partial(_s5_scan_kernel, rows=rows),
        grid=(nt, bsz, nblk),
        in_specs=[
            pl.BlockSpec((tok, LANES), lambda t, b, r: (b * nblk + r, t)),
            pl.BlockSpec((1, lc, LANES, LANES), lambda t, b, r: (t, 0, 0, 0)),
            pl.BlockSpec((1, lc * LANES, ns), lambda t, b, r: (t, 0, 0)),
            pl.BlockSpec((1, ns, lc * LANES), lambda t, b, r: (t, 0, 0)),
            pl.BlockSpec((1, 24, ns), lambda t, b, r: (t, 0, 0)),
        ],
        out_specs=pl.BlockSpec((tok, LANES), lambda t, b, r: (b * nblk + r, t)),
        out_shape=jax.ShapeDtypeStruct((bsz * seq, S5_WIDTH), F32),
        scratch_shapes=[pltpu.VMEM((lc * LANES, lc * LANES), BF16), pltpu.VMEM((rows, ns), F32),
                        pltpu.VMEM((rows, ns), F32), pltpu.VMEM((1, ns), F32)],
        compiler_params=_cparams(("arbitrary", "arbitrary", "arbitrary")),
        name="s5_scan",
    )(u, toep, w_in, v_out, pw)


def _s5_post_kernel(y_ref, u_ref, d_ref, w_ref, b_ref, o_ref):
    g = jax.nn.gelu(y_ref[...] + d_ref[...] * u_ref[...], approximate=True)
    gate = jnp.dot(g.astype(BF16), w_ref[...], preferred_element_type=F32) + b_ref[...]
    o_ref[...] = (g * jax.nn.sigmoid(gate)).astype(o_ref.dtype)


def _s5_post(y, u, d_skip, w_glu, b_glu, tm=512):
    t = y.shape[0]
    tm = min(tm, t)
    row = pl.BlockSpec((tm, S5_WIDTH), lambda i: (i, 0))
    vec = pl.BlockSpec((1, S5_WIDTH), lambda i: (0, 0))
    return pl.pallas_call(
        _s5_post_kernel,
        grid=(t // tm,),
        in_specs=[row, row, vec, pl.BlockSpec((S5_WIDTH, S5_WIDTH), lambda i: (0, 0)), vec],
        out_specs=row,
        out_shape=jax.ShapeDtypeStruct((t, S5_WIDTH), BF16),
        compiler_params=_cparams(("parallel",)),
        name="s5_post",
    )(y, u, d_skip.astype(F32)[None, :], w_glu.astype(BF16), b_glu.astype(F32)[None, :])


def _ret_chunk(q_ref, k_ref, v_ref, g_ref, cos_ref, sin_ref, intra_ref, dend_ref, osc_ref, nw_ref,
               o_ref, r_ref, *, chunk_decay):
    L = RET_CHUNK

    cos = cos_ref[...]
    sin = sin_ref[...]
    lane = lax.broadcasted_iota(jnp.int32, (L, LANES), 1)
    rowi = lax.broadcasted_iota(jnp.int32, (LANES, L), 0)
    half = RET_QK_DIM // 2
    outs = []
    for t in range(RET_HEADS // 2):
        sl = slice(t * LANES, (t + 1) * LANES)
        qt = q_ref[:, sl].astype(F32)
        kt = k_ref[:, sl].astype(F32)
        qr = qt * cos + pltpu.roll(qt, 64, 1) * sin
        kr = (kt * cos + pltpu.roll(kt, 64, 1) * sin) * (RET_QK_DIM ** -0.5)
        kr_b = kr.astype(BF16)
        kr_t = kr.T
        for sub in range(2):
            h = 2 * t + sub
            vs = slice(h * RET_V_DIM, (h + 1) * RET_V_DIM)
            qm = jnp.where(((lane // half) % 2) == sub, qr, 0.0).astype(BF16)
            s = lax.dot_general(qm, kr_b, (((1,), (1,)), ((), ())), preferred_element_type=F32)
            p = (s * intra_ref[h]).astype(BF16)
            vh = v_ref[:, vs]
            state = r_ref[h]
            y_in = jnp.dot(p, vh, preferred_element_type=F32)
            y_off = jnp.dot(qm, state.astype(BF16), preferred_element_type=F32) * osc_ref[:, vs]
            ktm = jnp.where(((rowi // half) % 2) == sub, kr_t * dend_ref[h:h + 1, :], 0.0).astype(BF16)
            r_ref[h] = state * chunk_decay[h] + jnp.dot(ktm, vh, preferred_element_type=F32)
            y = y_in + y_off
            mu = jnp.mean(y, -1, keepdims=True)
            yc = y - mu
            var = jnp.mean(yc * yc, -1, keepdims=True)
            yn = yc * lax.rsqrt(var + LN_EPS) * nw_ref[:, vs]
            outs.append(_silu(g_ref[:, vs].astype(F32)) * yn)
    o_ref[...] = jnp.concatenate(outs, axis=1).astype(o_ref.dtype)


def _ret_mixer(proj, bsz, seq, positions, norm_w):
    L = RET_CHUNK
    lb = min(L * MIXER_CHUNKS, seq)
    nc = seq // lb
    half = RET_QK_DIM // 2
    inv_freq = 1.0 / (ROPE_BASE ** (jnp.arange(half, dtype=F32) / half))
    ang = positions.astype(F32).reshape(bsz * seq, 1) * inv_freq[None, :]
    cos32, sin32 = jnp.cos(ang), jnp.sin(ang)
    cos_t = jnp.concatenate([cos32] * 4, axis=1)
    sin_t = jnp.concatenate([-sin32, -sin32, sin32, sin32], axis=1)
    log_gamma = np.log1p(-np.exp2(-5.0 - np.arange(RET_HEADS, dtype=np.float64)))
    pos = np.arange(L, dtype=np.float64)
    rel = pos[:, None] - pos[None, :]
    intra = np.where(rel[None] >= 0, np.exp(rel[None] * log_gamma[:, None, None]), 0.0).astype(np.float32)
    dend = np.exp((L - 1 - pos)[None, :] * log_gamma[:, None]).astype(np.float32)
    osc = np.repeat(np.exp((pos + 1)[:, None] * log_gamma[None, :]), RET_V_DIM, axis=1).astype(np.float32)
    chunk_decay = tuple(float(v) for v in np.exp(L * log_gamma))
    rows = lambda b, c: b * nc + c
    full = lambda shape: pl.BlockSpec(shape, lambda b, c: (0,) * len(shape))
    return pl.pallas_call(
        _chunked_kernel(functools.partial(_ret_chunk, chunk_decay=chunk_decay), 6, 4, L),
        grid=(bsz, nc),
        in_specs=[
            pl.BlockSpec((lb, 512), lambda b, c: (rows(b, c), _CB_RQ)),
            pl.BlockSpec((lb, 512), lambda b, c: (rows(b, c), _CB_RK)),
            pl.BlockSpec((lb, 1024), lambda b, c: (rows(b, c), _CB_RV)),
            pl.BlockSpec((lb, 1024), lambda b, c: (rows(b, c), _CB_RG)),
            pl.BlockSpec((lb, LANES), lambda b, c: (rows(b, c), 0)),
            pl.BlockSpec((lb, LANES), lambda b, c: (rows(b, c), 0)),
            full((RET_HEADS, L, L)), full((RET_HEADS, L)), full((L, RET_WIDTH)), full((1, RET_WIDTH)),
        ],
        out_specs=pl.BlockSpec((lb, RET_WIDTH), lambda b, c: (rows(b, c), 0)),
        out_shape=jax.ShapeDtypeStruct((bsz * seq, RET_WIDTH), BF16),
        scratch_shapes=[pltpu.VMEM((RET_HEADS, LANES, RET_V_DIM), F32)],
        compiler_params=_cparams(("parallel", "arbitrary")),
        name="retention_mixer",
    )(proj, proj, proj, proj, cos_t, sin_t, jnp.asarray(intra), jnp.asarray(dend), jnp.asarray(osc),
      norm_w.astype(F32)[None, :])


def _gla_chunk(q_ref, k_ref, v_ref, r_ref, sm_ref, wa_ref, ba_ref, tblk_ref, nw_ref, o_ref, s_ref):
    LB = GLA_BLOCK
    LC = GLA_CHUNK

    logit = jnp.dot(sm_ref[...], wa_ref[...], preferred_element_type=F32, precision=HIGHEST) + ba_ref[...]
    la = (jnp.minimum(logit, 0.0) - jnp.log1p(jnp.exp(-jnp.abs(logit)))) * (1.0 / GLA_TAU)
    b = jnp.dot(tblk_ref[...], la, preferred_element_type=F32, precision=HIGHEST)
    rowf = lax.broadcasted_iota(jnp.int32, (LB, GLA_QK), 0)
    b_end = jnp.where(rowf < LC, b[LC - 1:LC], b[LB - 1:LB])
    q = q_ref[...].astype(F32) * (GLA_QK_DIM ** -0.5)
    k = k_ref[...].astype(F32)
    q_dec = q * jnp.exp(b)
    k_inv = (k * jnp.exp(-b)).astype(BF16)
    k_dec = k * jnp.exp(b_end - b)
    row = lax.broadcasted_iota(jnp.int32, (LB, LB), 0)
    col = lax.broadcasted_iota(jnp.int32, (LB, LB), 1)
    blk_causal = (row >= col) & ((row // LC) == (col // LC))
    first_rows = row < LC
    first_vrows = lax.broadcasted_iota(jnp.int32, (LB, GLA_V_DIM), 0) < LC
    outs = []
    for h in range(GLA_HEADS):
        sl = slice(h * GLA_QK_DIM, (h + 1) * GLA_QK_DIM)
        vs = slice(h * GLA_V_DIM, (h + 1) * GLA_V_DIM)
        qd = q_dec[:, sl]
        qd_b = qd.astype(BF16)
        att = lax.dot_general(qd_b, k_inv[:, sl], (((1,), (1,)), ((), ())), preferred_element_type=F32)
        att = jnp.where(blk_causal, att, 0.0).astype(BF16)
        vh = v_ref[:, vs]
        y = jnp.dot(att, vh, preferred_element_type=F32)
        s0 = s_ref[h]
        kd_t = k_dec[:, sl].T.astype(BF16)
        kv0 = jnp.dot(kd_t, jnp.where(first_vrows, vh, jnp.zeros_like(vh)), preferred_element_type=F32)
        kv1 = jnp.dot(kd_t, jnp.where(first_vrows, jnp.zeros_like(vh), vh), preferred_element_type=F32)
        b_t = b[:, sl].T
        s1 = s0 * jnp.exp(b_t[:, LC - 1:LC]) + kv0
        s_ref[h] = s1 * jnp.exp(b_t[:, LB - 1:LB]) + kv1
        y = y + jnp.dot(jnp.where(first_rows, qd, 0.0).astype(BF16), s0.astype(BF16), preferred_element_type=F32)
        y = y + jnp.dot(jnp.where(first_rows, 0.0, qd).astype(BF16), s1.astype(BF16), preferred_element_type=F32)
        ms = jnp.mean(y * y, -1, keepdims=True)
        yn = y * lax.rsqrt(ms + RMS_EPS) * nw_ref[:, vs]
        outs.append(yn * _silu(r_ref[:, vs].astype(F32)))
    o_ref[...] = jnp.concatenate(outs, axis=1).astype(o_ref.dtype)


def _gla_mixer(proj, small, bsz, seq, w_alpha, b_alpha, norm_w):
    LB = GLA_BLOCK
    lb = min(LB * MIXER_CHUNKS, seq)
    nb = seq // lb
    wa = jnp.zeros((LANES, GLA_QK), F32).at[SSD_HEADS:SSD_HEADS + GLA_GATE_RANK].set(w_alpha.astype(F32))
    idx = np.arange(LB)
    tblk = ((idx[:, None] >= idx[None, :]) & ((idx[:, None] // GLA_CHUNK) == (idx[None, :] // GLA_CHUNK)))
    rows = lambda b, c: b * nb + c
    full = lambda shape: pl.BlockSpec(shape, lambda b, c: (0,) * len(shape))
    return pl.pallas_call(
        _chunked_kernel(_gla_chunk, 5, 4, LB),
        grid=(bsz, nb),
        in_specs=[
            pl.BlockSpec((lb, 512), lambda b, c: (rows(b, c), _CB_GQ)),
            pl.BlockSpec((lb, 512), lambda b, c: (rows(b, c), _CB_GK)),
            pl.BlockSpec((lb, 1024), lambda b, c: (rows(b, c), _CB_GV)),
            pl.BlockSpec((lb, 1024), lambda b, c: (rows(b, c), _CB_GR)),
            pl.BlockSpec((lb, LANES), lambda b, c: (rows(b, c), 0)),
            full((LANES, GLA_QK)), full((1, GLA_QK)), full((LB, LB)), full((1, GLA_WIDTH)),
        ],
        out_specs=pl.BlockSpec((lb, GLA_WIDTH), lambda b, c: (rows(b, c), 0)),
        out_shape=jax.ShapeDtypeStruct((bsz * seq, GLA_WIDTH), BF16),
        scratch_shapes=[pltpu.VMEM((GLA_HEADS, GLA_QK_DIM, GLA_V_DIM), F32)],
        compiler_params=_cparams(("parallel", "arbitrary")),
        name="gla_mixer",
    )(proj, proj, proj, proj, small, wa, b_alpha.astype(F32)[None, :], jnp.asarray(tblk.astype(np.float32)),
      norm_w.astype(F32)[None, :])


def _merge_kernel(ya_ref, yb_ref, yc_ref, yd_ref, g0_ref, g1_ref, g2_ref, g3_ref, wb_ref, o_ref):
    acc = None
    for n, (y_ref, g_ref) in enumerate(((ya_ref, g0_ref), (yb_ref, g1_ref), (yc_ref, g2_ref), (yd_ref, g3_ref))):
        br = jnp.dot(y_ref[...], wb_ref[n], preferred_element_type=F32)
        term = jax.nn.sigmoid(g_ref[...].astype(F32)) * br
        acc = term if acc is None else acc + term
    o_ref[...] = acc.astype(o_ref.dtype)


def _merge(ys, proj, w_branch, tm=1024, tn=512):
    t = ys[0].shape[0]
    tm = min(tm, t)
    nj = D_MODEL // tn
    yspec = pl.BlockSpec((tm, BRANCH_WIDTH), lambda i, j: (i, 0))
    gspec = lambda n: pl.BlockSpec((tm, tn), lambda i, j: (i, n * nj + j))
    return pl.pallas_call(
        _merge_kernel,
        grid=(t // tm, nj),
        in_specs=[yspec] * 4 + [gspec(n) for n in range(N_BRANCH)]
                 + [pl.BlockSpec((N_BRANCH, BRANCH_WIDTH, tn), lambda i, j: (0, 0, j))],
        out_specs=pl.BlockSpec((tm, tn), lambda i, j: (i, j)),
        out_shape=jax.ShapeDtypeStruct((t, D_MODEL), BF16),
        compiler_params=_cparams(("parallel", "parallel")),
        name="branch_merge",
    )(*ys, proj, proj, proj, proj, w_branch.astype(BF16))


def _proj_ln_kernel(m_ref, w_ref, h_ref, lw_ref, lb_ref, o_ref, op_ref):
    mix = jnp.dot(m_ref[...], w_ref[...], preferred_element_type=F32)
    out = _layer_norm(DEEPNORM_ALPHA * h_ref[...] + mix, lw_ref[...], lb_ref[...])
    o_ref[...] = out
    _store_token_tiles(op_ref, 0, _pack_rows(out.astype(BF16)))


def _proj_ln(merged, w_out, h, ln_w, ln_b, tm=512):
    t = h.shape[0]
    tm = min(tm, t)
    row = pl.BlockSpec((tm, D_MODEL), lambda i: (i, 0))
    vec = pl.BlockSpec((1, D_MODEL), lambda i: (0, 0))
    return pl.pallas_call(
        _proj_ln_kernel,
        grid=(t // tm,),
        in_specs=[row, pl.BlockSpec((D_MODEL, D_MODEL), lambda i: (0, 0)), row, vec, vec],
        out_specs=[row, pl.BlockSpec((tm * TILE_ROWS, LANES), lambda i: (i, 0))],
        out_shape=[jax.ShapeDtypeStruct((t, D_MODEL), F32), jax.ShapeDtypeStruct((t * TILE_ROWS, LANES), jnp.uint32)],
        compiler_params=_cparams(("parallel",)),
        name="out_proj_ln",
    )(merged, w_out.astype(BF16), h, ln_w.astype(F32)[None, :], ln_b.astype(F32)[None, :])


def _router_kernel(h_ref, w_ref, b_ref, idx_ref, gate_ref, cnt_ref, run_ref):
    @pl.when(pl.program_id(0) == 0)
    def _():
        run_ref[...] = jnp.zeros_like(run_ref)

    logits = _dot3(w_ref[...], h_ref[...], ((1,), (1,))) + b_ref[...]
    eid = lax.broadcasted_iota(jnp.int32, logits.shape, 0)
    vals = logits
    run = run_ref[...]
    top_v, top_i = [], []
    for _ in range(TOP_K):
        m = jnp.max(vals, axis=0, keepdims=True)
        sel = jnp.min(jnp.where(vals == m, eid, N_EXPERTS), axis=0, keepdims=True)
        hit = eid == sel
        top_v.append(m)
        top_i.append(sel)
        vals = jnp.where(hit, -jnp.inf, vals)
        run = run + jnp.sum(jnp.where(hit, 1.0, 0.0), axis=1, keepdims=True)
    run_ref[...] = run
    ex = [jnp.exp(v - top_v[0]) for v in top_v]
    den = ex[0] + ex[1] + ex[2] + ex[3]
    zi = jnp.zeros_like(top_i[0])
    zf = jnp.zeros_like(den)
    idx_ref[...] = jnp.concatenate(top_i + [zi] * (8 - TOP_K), axis=0)
    gate_ref[...] = jnp.concatenate([e / den for e in ex] + [zf] * (8 - TOP_K), axis=0)
    cnt_ref[...] = jnp.broadcast_to(run, cnt_ref.shape).astype(jnp.int32)


def _router(h, router_w, router_b, tm=512):
    t = h.shape[0]
    tm = min(tm, t)
    tok = pl.BlockSpec((8, tm), lambda i: (0, i))
    return pl.pallas_call(
        _router_kernel,
        grid=(t // tm,),
        in_specs=[pl.BlockSpec((tm, D_MODEL), lambda i: (i, 0)),
                  pl.BlockSpec((N_EXPERTS, D_MODEL), lambda i: (0, 0)),
                  pl.BlockSpec((N_EXPERTS, 1), lambda i: (0, 0))],
        out_specs=[tok, tok, pl.BlockSpec((N_EXPERTS, LANES), lambda i: (0, 0))],
        out_shape=[jax.ShapeDtypeStruct((8, t), jnp.int32), jax.ShapeDtypeStruct((8, t), F32),
                   jax.ShapeDtypeStruct((N_EXPERTS, LANES), jnp.int32)],
        scratch_shapes=[pltpu.VMEM((N_EXPERTS, 1), F32)],
        compiler_params=_cparams(("arbitrary",)),
        name="router_topk",
    )(h, router_w.astype(F32).T, router_b.astype(F32)[:, None])


def _pack_rows(x):
    n = x.shape[1] // 2
    lo = pltpu.bitcast(x[:, :n].astype(F32), jnp.uint32) >> 16
    hi = pltpu.bitcast(x[:, n:].astype(F32), jnp.uint32)
    return lo | hi


def _unpack_rows(w):
    lo = pltpu.bitcast(w << 16, F32)
    hi = pltpu.bitcast(w & jnp.uint32(0xFFFF0000), F32)
    return lo, hi


TILE_ROWS = (D_MODEL // 2) // LANES


def _store_token_tiles(ref, first, words):
    m = words.shape[0]
    for s in range(TILE_ROWS):
        ref[pl.ds(first * TILE_ROWS + s, m, stride=TILE_ROWS), :] = words[:, s * LANES:(s + 1) * LANES]


def _load_token_tiles(ref, first, m):
    return jnp.concatenate([ref[pl.ds(first * TILE_ROWS + s, m, stride=TILE_ROWS), :] for s in range(TILE_ROWS)],
                           axis=1)


def _expert_kernel(be_ref, nb_ref, tab_ref, hp_ref, wgu_ref, bgu_ref, wd_ref, bd_ref, y_ref,
                   idx_ref, xbuf_ref, ybuf_ref, isem, gsem, ssem):
    i = pl.program_id(0)
    n_used = nb_ref[0]
    blk = MOE_BLOCK
    tr = TILE_ROWS

    def idx_copy(b):
        return pltpu.make_async_copy(tab_ref.at[b], idx_ref.at[b % 4], isem.at[b % 4])

    def tile(r):
        return pl.ds(r * tr if isinstance(r, int) else pl.multiple_of(r * tr, tr), tr)

    def gather_start(b, r):
        src = pl.multiple_of(idx_ref[b % 4, r], tr)
        pltpu.make_async_copy(hp_ref.at[pl.ds(src, tr)], xbuf_ref.at[b % 2, tile(r)], gsem.at[b % 2]).start()

    def scatter_start(b, r):
        dst = pl.multiple_of(idx_ref[b % 4, blk + r], tr)
        pltpu.make_async_copy(ybuf_ref.at[b % 2, tile(r)], y_ref.at[pl.ds(dst, tr)], ssem.at[b % 2]).start(priority=1)

    def gather_wait(b):
        pltpu.make_async_copy(hp_ref.at[pl.ds(0, blk * tr)], xbuf_ref.at[b % 2], gsem.at[b % 2]).wait()

    def scatter_wait(b):
        pltpu.make_async_copy(ybuf_ref.at[b % 2], y_ref.at[pl.ds(0, blk * tr)], ssem.at[b % 2]).wait()

    def ffn(b):
        lo, hi = _unpack_rows(_load_token_tiles(xbuf_ref.at[b % 2], 0, blk))
        xb = jnp.concatenate([lo.astype(BF16), hi.astype(BF16)], axis=1)
        hgu = jnp.dot(xb, wgu_ref[0], preferred_element_type=F32) + bgu_ref[0]
        g = jnp.minimum(hgu[:, :D_EXPERT], SWIGLU_LIMIT)
        u = jnp.clip(hgu[:, D_EXPERT:], -SWIGLU_LIMIT, SWIGLU_LIMIT)
        act = (u + 1.0) * (g * jax.nn.sigmoid(SWIGLU_ALPHA * g))
        out = jnp.dot(act.astype(BF16), wd_ref[0], preferred_element_type=F32) + bd_ref[0]
        packed = _pack_rows(out.astype(BF16))

        @pl.when(i >= 2)
        def _():
            scatter_wait(i - 2)

        _store_token_tiles(ybuf_ref.at[b % 2], 0, packed)

    @pl.when(i == 0)
    def _():
        idx_copy(0).start()
        ybuf_ref[1] = jnp.zeros(ybuf_ref.shape[1:], ybuf_ref.dtype)
        spare = pltpu.make_async_copy(ybuf_ref.at[1], y_ref.at[pl.ds(y_ref.shape[0] - blk * tr, blk * tr)],
                                      ssem.at[1])
        spare.start()
        idx_copy(0).wait()
        lax.fori_loop(0, blk, lambda r, c: (gather_start(0, r), c)[1], 0, unroll=8)
        idx_copy(1).start()
        spare.wait()

    @pl.when(i <= n_used)
    def _():
        idx_copy(i + 1).wait()
        gather_wait(i)

    @pl.when(i < n_used)
    def _():
        idx_copy(i + 2).start()

    @pl.when(i == 0)
    def _():
        for r in range(blk):
            gather_start(1, r)
        ffn(0)

    @pl.when((i > 0) & (i < n_used))
    def _():
        for r in range(blk):
            gather_start(i + 1, r)
            scatter_start(i - 1, r)
        ffn(i)

    @pl.when(i == n_used)
    def _():
        @pl.when(i >= 2)
        def _():
            scatter_wait(i - 2)

        lax.fori_loop(0, blk, lambda r, c: (scatter_start(i - 1, r), c)[1], 0, unroll=8)
        scatter_wait(i - 1)


def _expert_params(w_gate_up, b_gate_up, w_down, b_down):
    n = w_gate_up.shape[0] * N_EXPERTS
    return (w_gate_up.astype(BF16).reshape(n, D_MODEL, 2 * D_EXPERT), b_gate_up.astype(F32).reshape(n, 1, 2 * D_EXPERT),
            w_down.astype(BF16).reshape(n, D_EXPERT, D_MODEL), b_down.astype(F32).reshape(n, 1, D_MODEL))


def _experts(h_packed, table, block_expert, n_used, expert_params):
    w_gate_up, b_gate_up, w_down, b_down = expert_params
    blk = MOE_BLOCK
    n_blocks = block_expert.shape[0]
    n_out = TOP_K * (h_packed.shape[0] // TILE_ROWS) + blk
    grid_spec = pltpu.PrefetchScalarGridSpec(
        num_scalar_prefetch=2,
        grid=(n_blocks,),
        in_specs=[
            pl.BlockSpec(memory_space=pl.ANY),
            pl.BlockSpec(memory_space=pl.ANY),
            pl.BlockSpec((1, D_MODEL, 2 * D_EXPERT), lambda i, be, nb: (be[i], 0, 0)),
            pl.BlockSpec((1, 1, 2 * D_EXPERT), lambda i, be, nb: (be[i], 0, 0)),
            pl.BlockSpec((1, D_EXPERT, D_MODEL), lambda i, be, nb: (be[i], 0, 0)),
            pl.BlockSpec((1, 1, D_MODEL), lambda i, be, nb: (be[i], 0, 0)),
        ],
        out_specs=pl.BlockSpec(memory_space=pl.ANY),
        scratch_shapes=[pltpu.SMEM((4, 2 * blk), jnp.int32), pltpu.VMEM((2, blk * TILE_ROWS, LANES), jnp.uint32),
                        pltpu.VMEM((2, blk * TILE_ROWS, LANES), jnp.uint32), pltpu.SemaphoreType.DMA((4,)),
                        pltpu.SemaphoreType.DMA((2,)), pltpu.SemaphoreType.DMA((2,))],
    )
    return pl.pallas_call(
        _expert_kernel,
        grid_spec=grid_spec,
        out_shape=jax.ShapeDtypeStruct((n_out * TILE_ROWS, LANES), jnp.uint32),
        compiler_params=_cparams(("arbitrary",)),
        name="expert_ffn",
    )(block_expert, n_used, table, h_packed, w_gate_up, b_gate_up, w_down, b_down)


def _combine_kernel(y0_ref, y1_ref, y2_ref, y3_ref, gate_ref, h_ref, lw_ref, lb_ref, o_ref, ob_ref, *, tm):
    rows = min(tm, 16)
    for r0 in range(0, tm, rows):
        rs = slice(r0, r0 + rows)
        gates = gate_ref[rs, :]
        ffn = None
        for k, y_ref in enumerate((y0_ref, y1_ref, y2_ref, y3_ref)):
            lo, hi = _unpack_rows(_load_token_tiles(y_ref, r0, rows))
            term = gates[:, k:k + 1] * jnp.concatenate([lo, hi], axis=1)
            ffn = term if ffn is None else ffn + term
        out = _layer_norm(DEEPNORM_ALPHA * h_ref[rs, :] + ffn, lw_ref[...], lb_ref[...])
        o_ref[rs, :] = out
        ob_ref[rs, :] = out.astype(BF16)


def _combine(y, gates, h, ln_w, ln_b, tm=512):
    t = h.shape[0]
    tm = min(tm, t)
    nt = t // tm
    row = pl.BlockSpec((tm, D_MODEL), lambda i: (i, 0))
    vec = pl.BlockSpec((1, D_MODEL), lambda i: (0, 0))
    yspec = lambda k: pl.BlockSpec((tm * TILE_ROWS, LANES), lambda i: (k * nt + i, 0))
    return pl.pallas_call(
        functools.partial(_combine_kernel, tm=tm),
        grid=(nt,),
        in_specs=[yspec(k) for k in range(TOP_K)] + [pl.BlockSpec((tm, 8), lambda i: (i, 0)), row, vec, vec],
        out_specs=[row, row],
        out_shape=[jax.ShapeDtypeStruct((t, D_MODEL), F32), jax.ShapeDtypeStruct((t, D_MODEL), BF16)],
        compiler_params=_cparams(("parallel",)),
        name="moe_combine_ln",
    )(y, y, y, y, gates, h, ln_w.astype(F32)[None, :], ln_b.astype(F32)[None, :])


def _moe(h, h_packed, router_w, router_b, expert_params, layer, ln_w, ln_b):
    t = h.shape[0]
    blk = MOE_BLOCK
    top_idx, top_gate, counts = _router(h, router_w, router_b)
    counts = counts[:, 0]
    n_assign = TOP_K * t
    n_blocks = n_assign // blk + N_EXPERTS
    a_ids = jnp.arange(n_assign, dtype=jnp.int32)
    a_sorted = jnp.sort(top_idx[:TOP_K].reshape(-1) * n_assign + a_ids) % n_assign
    padded = (counts + blk - 1) // blk * blk
    pad_end = jnp.cumsum(padded)
    grp_end = jnp.cumsum(counts)
    blocks = jnp.arange(n_blocks, dtype=jnp.int32)
    block_expert = jnp.sum(((pad_end // blk)[None, :] <= blocks[:, None]).astype(jnp.int32), axis=1)
    block_expert = jnp.minimum(block_expert, N_EXPERTS - 1)
    n_used = (pad_end[-1] // blk).astype(jnp.int32).reshape(1)
    is_e = block_expert[:, None] == jnp.arange(N_EXPERTS, dtype=jnp.int32)[None, :]
    pick = lambda v: jnp.sum(jnp.where(is_e, v[None, :], 0), axis=1)
    offset = blocks * blk - pick(pad_end - padded)
    first = pick(grp_end - counts) + offset
    n_valid = jnp.clip(pick(counts) - offset, 0, blk)
    j = jnp.arange(blk, dtype=jnp.int32)[None, :]
    valid = j < n_valid[:, None]
    a_rows = a_sorted[jnp.clip(first[:, None] + j, 0, n_assign - 1)]
    tok_rows = jnp.where(valid, a_rows % t, 0)
    out_rows = jnp.where(valid, a_rows, n_assign + j)
    table = (jnp.concatenate([tok_rows, out_rows], axis=1) * TILE_ROWS).astype(jnp.int32)
    table = jnp.concatenate([table, jnp.zeros((2, 2 * blk), jnp.int32)], axis=0)
    y = _experts(h_packed, table, block_expert + layer * N_EXPERTS, n_used, expert_params)
    return _combine(y, top_gate.T, h, ln_w, ln_b)


def _ple_kernel(hb_ref, h_ref, p_ref, wg_ref, wp_ref, lw_ref, lb_ref, o_ref, ob_ref):
    gate = jax.nn.sigmoid(jnp.dot(hb_ref[...], wg_ref[...], preferred_element_type=F32))
    emb = jnp.dot(p_ref[...].astype(BF16), wp_ref[...], preferred_element_type=F32)
    out = _layer_norm(DEEPNORM_ALPHA * h_ref[...] + gate * emb, lw_ref[...], lb_ref[...])
    o_ref[...] = out
    ob_ref[...] = out.astype(BF16)


def _ple(hb, h, p_i, w_gate, w_proj, ln_w, ln_b, tm=512):
    t = h.shape[0]
    tm = min(tm, t)
    row = pl.BlockSpec((tm, D_MODEL), lambda i: (i, 0))
    vec = pl.BlockSpec((1, D_MODEL), lambda i: (0, 0))
    return pl.pallas_call(
        _ple_kernel,
        grid=(t // tm,),
        in_specs=[row, row, pl.BlockSpec((tm, PLE_DIM), lambda i: (i, 0)),
                  pl.BlockSpec((D_MODEL, D_MODEL), lambda i: (0, 0)),
                  pl.BlockSpec((PLE_DIM, D_MODEL), lambda i: (0, 0)), vec, vec],
        out_specs=[row, row],
        out_shape=[jax.ShapeDtypeStruct((t, D_MODEL), F32), jax.ShapeDtypeStruct((t, D_MODEL), BF16)],
        compiler_params=_cparams(("parallel",)),
        name="ple_ln",
    )(hb, h, p_i, w_gate.astype(BF16), w_proj.astype(BF16), ln_w.astype(F32)[None, :], ln_b.astype(F32)[None, :])


def _token_mixer(h, hb, bsz, seq, positions, w_in, ssd_conv_w, ssd_conv_b, ssd_dt_bias, ssd_a_log, ssd_d,
                 ssd_norm_w, s5_lambda_re, s5_lambda_im, s5_log_dt, s5_b_re, s5_b_im, s5_c_re, s5_c_im, s5_d,
                 s5_w_glu, s5_b_glu, ret_norm_w, gla_w_alpha, gla_b_alpha, gla_norm_w, w_branch, s5_rows):
    w_main = _main_weight(w_in)
    w_u = w_in[:, _OFF_U:_OFF_U + S5_WIDTH].astype(BF16)
    w_small = jnp.concatenate([w_in[:, _OFF_DT:_OFF_DT + SSD_HEADS], w_in[:, _OFF_GC:_OFF_GC + GLA_GATE_RANK],
                               jnp.zeros((D_MODEL, LANES - SSD_HEADS - GLA_GATE_RANK), w_in.dtype)], axis=1)
    proj = _matmul(hb, w_main, BF16, 1024, 1536, name="in_proj_main")
    u = _matmul(hb, w_u, F32, 1024, 1024, name="in_proj_s5")
    small = _matmul_f32(h, w_small.astype(F32), 512, name="in_proj_small")
    y_a = _ssd_mixer(proj, small, bsz, seq, ssd_conv_w, ssd_conv_b, ssd_dt_bias, ssd_a_log, ssd_d, ssd_norm_w)
    tables = _s5_tables(s5_lambda_re, s5_lambda_im, s5_log_dt, s5_b_re, s5_b_im, s5_c_re, s5_c_im)
    y_b = _s5_post(_s5_scan(u, bsz, seq, tables, s5_rows), u, s5_d, s5_w_glu, s5_b_glu)
    y_c = _ret_mixer(proj, bsz, seq, positions, ret_norm_w)
    y_d = _gla_mixer(proj, small, bsz, seq, gla_w_alpha, gla_b_alpha, gla_norm_w)
    return _merge((y_a, y_b, y_c, y_d), proj, w_branch)


def kernel(x, p, positions, w_in, ssd_conv_w, ssd_conv_b, ssd_dt_bias, ssd_a_log, ssd_d, ssd_norm_w, s5_lambda_re, s5_lambda_im, s5_log_dt, s5_b_re, s5_b_im, s5_c_re, s5_c_im, s5_d, s5_w_glu, s5_b_glu, ret_norm_w, gla_w_alpha, gla_b_alpha, gla_norm_w, w_branch, w_out, ln1_w, ln1_b, router_w, router_b, moe_w_gate_up, moe_b_gate_up, moe_w_down, moe_b_down, ln2_w, ln2_b, ple_w_gate, ple_w_proj, ln3_w, ln3_b):
    s5_rows = 256
    bsz, seq, d = x.shape
    t = bsz * seq
    h = x.reshape(t, d).astype(F32)
    hb = h.astype(BF16)
    expert_params = _expert_params(moe_w_gate_up, moe_b_gate_up, moe_w_down, moe_b_down)
    for i in range(DEPTH):
        merged = _token_mixer(h, hb, bsz, seq, positions, w_in[i], ssd_conv_w[i], ssd_conv_b[i], ssd_dt_bias[i],
                              ssd_a_log[i], ssd_d[i], ssd_norm_w[i], s5_lambda_re[i], s5_lambda_im[i],
                              s5_log_dt[i], s5_b_re[i], s5_b_im[i], s5_c_re[i], s5_c_im[i], s5_d[i],
                              s5_w_glu[i], s5_b_glu[i], ret_norm_w[i], gla_w_alpha[i], gla_b_alpha[i],
                              gla_norm_w[i], w_branch[i], s5_rows)
        h, h_packed = _proj_ln(merged, w_out[i], h, ln1_w[i], ln1_b[i])
        h, hb = _moe(h, h_packed, router_w[i], router_b[i], expert_params, i, ln2_w[i], ln2_b[i])
        h, hb = _ple(hb, h, p[i].reshape(t, PLE_DIM), ple_w_gate[i], ple_w_proj[i], ln3_w[i], ln3_b[i])
    return h.reshape(bsz, seq, d).astype(x.dtype)
```

```python
import functools
import math

import numpy as np
import jax
import jax.numpy as jnp
from jax import lax
from jax.experimental import pallas as pl
from jax.experimental.pallas import tpu as pltpu

F32 = jnp.float32
BF16 = jnp.bfloat16
HIGHEST = lax.Precision.HIGHEST

D_MODEL = 2048
DEPTH = 2
PLE_DIM = 256
N_BRANCH = 4
BRANCH_WIDTH = 1024

SSD_HEADS = 16
SSD_HEAD_DIM = 64
SSD_INNER = SSD_HEADS * SSD_HEAD_DIM
SSD_GROUPS = 2
SSD_STATE = 128
SSD_CONV = 4
SSD_CHUNK = 128

S5_WIDTH = 1024
S5_GROUP = 16
S5_GROUPS = S5_WIDTH // S5_GROUP
S5_STATE = 64
S5_CHUNK = 16
S5_TILE_GROUPS = 8
S5_SCAN_ROWS = 8

RET_HEADS = 8
RET_QK_DIM = 64
RET_V_DIM = 128
RET_QK = RET_HEADS * RET_QK_DIM
RET_WIDTH = RET_HEADS * RET_V_DIM
RET_CHUNK = 128
ROPE_BASE = 10000.0

GLA_HEADS = 4
GLA_QK_DIM = 128
GLA_V_DIM = 256
GLA_QK = GLA_HEADS * GLA_QK_DIM
GLA_WIDTH = GLA_HEADS * GLA_V_DIM
GLA_GATE_RANK = 16
GLA_TAU = 16.0
GLA_CHUNK = 64
GLA_BLOCK = 128

N_EXPERTS = 32
TOP_K = 4
D_EXPERT = 1024
SWIGLU_ALPHA = 1.702
SWIGLU_LIMIT = 7.0
MOE_BLOCK = 512

DEEPNORM_ALPHA = (2 * DEPTH) ** 0.25
LN_EPS = 1e-5
RMS_EPS = 1e-6

LANES = 128
VMEM_LIMIT = 56 * 1024 * 1024

_OFF_Z = 0
_OFF_XS = 1024
_OFF_BC = 2048
_OFF_DT = 2560
_OFF_U = 2576
_OFF_RQ = 3600
_OFF_RK = 4112
_OFF_RV = 4624
_OFF_RG = 5648
_OFF_GQ = 6672
_OFF_GK = 7184
_OFF_GV = 7696
_OFF_GR = 8720
_OFF_GC = 9744
_OFF_GATE = 9760

_CB_Z, _CB_XS, _CB_RV, _CB_RG, _CB_GV, _CB_GR = 8, 9, 10, 11, 12, 13
_CB_BC, _CB_RQ, _CB_RK, _CB_GQ, _CB_GK = 28, 29, 30, 31, 32
N_MAIN = 16896


def _main_weight(w_in):
    def cols(off, n):
        return w_in[:, off:off + n].astype(BF16)

    def rotary(off):
        half = RET_QK_DIM // 2
        w = cols(off, RET_QK).reshape(D_MODEL, RET_HEADS // 2, 2, 2, half)
        return w.transpose(0, 1, 3, 2, 4).reshape(D_MODEL, RET_QK)

    return jnp.concatenate([
        cols(_OFF_GATE, N_BRANCH * D_MODEL), cols(_OFF_Z, 1024), cols(_OFF_XS, 1024), cols(_OFF_RV, 1024),
        cols(_OFF_RG, 1024), cols(_OFF_GV, 1024), cols(_OFF_GR, 1024), cols(_OFF_BC, 512),
        rotary(_OFF_RQ), rotary(_OFF_RK), cols(_OFF_GQ, 512), cols(_OFF_GK, 512)], axis=1)


def _cparams(sem):
    return pltpu.CompilerParams(dimension_semantics=sem, vmem_limit_bytes=VMEM_LIMIT)


def _silu(x):
    return x * jax.nn.sigmoid(x)


def _softplus(x):
    return jnp.maximum(x, 0.0) + jnp.log1p(jnp.exp(-jnp.abs(x)))


def _layer_norm(x, w, b):
    mu = jnp.mean(x, -1, keepdims=True)
    xc = x - mu
    var = jnp.mean(xc * xc, -1, keepdims=True)
    return xc * lax.rsqrt(var + LN_EPS) * w + b


def _mm_kernel(a_ref, b_ref, o_ref):
    o_ref[...] = jnp.dot(a_ref[...], b_ref[...], preferred_element_type=F32).astype(o_ref.dtype)


def _matmul(a, b, out_dtype, tm, tn, name="matmul"):
    m, k = a.shape
    n = b.shape[1]
    tm, tn = min(tm, m), min(tn, n)
    return pl.pallas_call(
        _mm_kernel,
        grid=(m // tm, n // tn),
        in_specs=[pl.BlockSpec((tm, k), lambda i, j: (i, 0)),
                  pl.BlockSpec((k, tn), lambda i, j: (0, j))],
        out_specs=pl.BlockSpec((tm, tn), lambda i, j: (i, j)),
        out_shape=jax.ShapeDtypeStruct((m, n), out_dtype),
        compiler_params=_cparams(("parallel", "parallel")),
        name=name,
    )(a, b)


def _split_bf16(x):
    hi = x.astype(BF16)
    return hi, (x - hi.astype(F32)).astype(BF16)


def _dot3(a, b, dims):
    a_hi, a_lo = _split_bf16(a)
    b_hi, b_lo = _split_bf16(b)
    dg = functools.partial(lax.dot_general, dimension_numbers=(dims, ((), ())), preferred_element_type=F32)
    return dg(a_hi, b_hi) + (dg(a_hi, b_lo) + dg(a_lo, b_hi))


def _mm3_kernel(a_ref, b_ref, o_ref):
    o_ref[...] = _dot3(a_ref[...], b_ref[...], ((1,), (0,)))


def _matmul_f32(a, b, tm, name):
    m, k = a.shape
    n = b.shape[1]
    tm = min(tm, m)
    return pl.pallas_call(
        _mm3_kernel,
        grid=(m // tm,),
        in_specs=[pl.BlockSpec((tm, k), lambda i: (i, 0)), pl.BlockSpec((k, n), lambda i: (0, 0))],
        out_specs=pl.BlockSpec((tm, n), lambda i: (i, 0)),
        out_shape=jax.ShapeDtypeStruct((m, n), F32),
        compiler_params=_cparams(("parallel",)),
        name=name,
    )(a, b)


MIXER_CHUNKS = 2


def _chunked_kernel(chunk_fn, n_row, n_const, length):
    def kernel(*refs):
        row_refs, const_refs = refs[:n_row], refs[n_row:n_row + n_const]
        o_ref, scratch = refs[n_row + n_const], refs[n_row + n_const + 1:]

        @pl.when(pl.program_id(1) == 0)
        def _():
            for ref in scratch:
                ref[...] = jnp.zeros_like(ref)

        for c in range(MIXER_CHUNKS):
            rows = pl.ds(c * length, length)
            chunk_fn(*[r.at[rows] for r in row_refs], *const_refs, o_ref.at[rows], *scratch)

    return kernel


def _causal_conv(x, tail, w, b):
    n_tap = SSD_CONV
    acc = x * w[n_tap - 1:n_tap] + b
    x8 = x[0:8]
    acc8 = x8 * w[n_tap - 1:n_tap] + b
    row8 = lax.broadcasted_iota(jnp.int32, x8.shape, 0)
    for s in range(1, n_tap):
        wk = w[n_tap - 1 - s:n_tap - s]
        acc = acc + pltpu.roll(x, s, 0) * wk
        v8 = jnp.where(row8 < s, pltpu.roll(tail, s, 0), pltpu.roll(x8, s, 0))
        acc8 = acc8 + v8 * wk
    return jnp.concatenate([acc8, acc[8:]], axis=0)


def _ssd_chunk(z_ref, xs_ref, bc_ref, sm_ref, cwx_ref, cwbc_ref, cbx_ref, cbbc_ref, dtb_ref,
               nega_ref, dsk_ref, nw_ref, exp_ref, tril_ref, o_ref, tailx_ref, tailbc_ref, ht_ref):
    L = SSD_CHUNK

    x_raw = xs_ref[...].astype(F32)
    bc_raw = bc_ref[...].astype(F32)
    xs = _silu(_causal_conv(x_raw, tailx_ref[...], cwx_ref[...], cbx_ref[...]))
    bc = _silu(_causal_conv(bc_raw, tailbc_ref[...], cwbc_ref[...], cbbc_ref[...]))
    tailx_ref[...] = x_raw[L - 8:L]
    tailbc_ref[...] = bc_raw[L - 8:L]

    dt = _softplus(sm_ref[...] + dtb_ref[...])
    loga = dt * nega_ref[...]
    acs = jnp.dot(tril_ref[...], loga, preferred_element_type=F32, precision=HIGHEST)
    acs_t = acs.T
    expand = exp_ref[...]
    dt_e = jnp.dot(dt, expand, preferred_element_type=F32, precision=HIGHEST)
    acs_e = jnp.dot(acs, expand, preferred_element_type=F32, precision=HIGHEST)
    last_e = acs_e[L - 1:L]
    xdt = xs * dt_e
    xdt_b = xdt.astype(BF16)
    xw_b = (xdt * jnp.exp(last_e - acs_e)).astype(BF16)
    dec_in = jnp.exp(acs_e)
    dec_chunk = jnp.exp(last_e)

    row = lax.broadcasted_iota(jnp.int32, (L, L), 0)
    col = lax.broadcasted_iota(jnp.int32, (L, L), 1)
    causal = row >= col
    lo_half = col < SSD_HEAD_DIM
    heads_per_group = SSD_HEADS // SSD_GROUPS
    n_bc = SSD_GROUPS * SSD_STATE

    y_tiles = []
    for g in range(SSD_GROUPS):
        bm = bc[:, g * SSD_STATE:(g + 1) * SSD_STATE]
        cm_b = bc[:, n_bc + g * SSD_STATE:n_bc + (g + 1) * SSD_STATE].astype(BF16)
        bm_t_b = bm.T.astype(BF16)
        cb = jnp.dot(cm_b, bm_t_b, preferred_element_type=F32)
        for j in range(g * heads_per_group // 2, (g + 1) * heads_per_group // 2):
            sl = slice(j * LANES, (j + 1) * LANES)
            xt = xdt_b[:, sl]
            y_pair = None
            for sub in range(2):
                h = 2 * j + sub
                seg = acs[:, h:h + 1] - acs_t[h:h + 1, :]
                m = (cb * jnp.exp(jnp.where(causal, seg, -jnp.inf))).astype(BF16)
                xh = jnp.where(lo_half if sub == 0 else ~lo_half, xt, jnp.zeros_like(xt))
                yd = jnp.dot(m, xh, preferred_element_type=F32)
                y_pair = yd if y_pair is None else y_pair + yd
            h_t = ht_ref[:, sl]
            y_off = jnp.dot(cm_b, h_t.astype(BF16), preferred_element_type=F32) * dec_in[:, sl]
            st = jnp.dot(bm_t_b, xw_b[:, sl], preferred_element_type=F32)
            ht_ref[:, sl] = h_t * dec_chunk[:, sl] + st
            y_tiles.append(y_pair + y_off)
    y = jnp.concatenate(y_tiles, axis=1) + dsk_ref[...] * xs
    y = y * _silu(z_ref[...].astype(F32))
    gw = SSD_INNER // SSD_GROUPS
    outs = []
    for g in range(SSD_GROUPS):
        yg = y[:, g * gw:(g + 1) * gw]
        ms = jnp.mean(yg * yg, -1, keepdims=True)
        outs.append(yg * lax.rsqrt(ms + RMS_EPS) * nw_ref[:, g * gw:(g + 1) * gw])
    o_ref[...] = jnp.concatenate(outs, axis=1).astype(o_ref.dtype)


def _ssd_mixer(proj, small, bsz, seq, conv_w, conv_b, dt_bias, a_log, d_skip, norm_w):
    L = SSD_CHUNK
    lb = min(L * MIXER_CHUNKS, seq)
    nc = seq // lb
    pad_lanes = LANES - SSD_HEADS
    cw = jnp.pad(conv_w.astype(F32), ((0, 8 - SSD_CONV), (0, 0)))
    cwx, cwbc = cw[:, :SSD_INNER], cw[:, SSD_INNER:]
    cb = conv_b.astype(F32)[None, :]
    cbx, cbbc = cb[:, :SSD_INNER], cb[:, SSD_INNER:]
    dtb = jnp.pad(dt_bias.astype(F32), (0, pad_lanes))[None, :]
    nega = jnp.pad(-jnp.exp(a_log.astype(F32)), (0, pad_lanes))[None, :]
    dsk = jnp.repeat(d_skip.astype(F32), SSD_HEAD_DIM)[None, :]
    nw = norm_w.astype(F32)[None, :]
    expand = (np.arange(LANES)[:, None] == (np.arange(SSD_INNER)[None, :] // SSD_HEAD_DIM)).astype(np.float32)
    tril = np.tril(np.ones((L, L), np.float32))
    rows = lambda b, c: b * nc + c
    full = lambda shape: pl.BlockSpec(shape, lambda b, c: (0,) * len(shape))
    return pl.pallas_call(
        _chunked_kernel(_ssd_chunk, 4, 10, L),
        grid=(bsz, nc),
        in_specs=[
            pl.BlockSpec((lb, 1024), lambda b, c: (rows(b, c), _CB_Z)),
            pl.BlockSpec((lb, 1024), lambda b, c: (rows(b, c), _CB_XS)),
            pl.BlockSpec((lb, 512), lambda b, c: (rows(b, c), _CB_BC)),
            pl.BlockSpec((lb, LANES), lambda b, c: (rows(b, c), 0)),
            full((8, SSD_INNER)), full((8, 512)), full((1, SSD_INNER)), full((1, 512)),
            full((1, LANES)), full((1, LANES)), full((1, SSD_INNER)), full((1, SSD_INNER)),
            full((LANES, SSD_INNER)), full((L, L)),
        ],
        out_specs=pl.BlockSpec((lb, SSD_INNER), lambda b, c: (rows(b, c), 0)),
        out_shape=jax.ShapeDtypeStruct((bsz * seq, SSD_INNER), BF16),
        scratch_shapes=[pltpu.VMEM((8, SSD_INNER), F32), pltpu.VMEM((8, 512), F32),
                        pltpu.VMEM((SSD_STATE, SSD_INNER), F32)],
        compiler_params=_cparams(("parallel", "arbitrary")),
        name="ssd_mixer",
    )(proj, proj, proj, small, cwx, cwbc, cbx, cbbc, dtb, nega, dsk, nw, jnp.asarray(expand), jnp.asarray(tril))


def _s5_scan_kernel(u_ref, bd_ref, w_ref, v_ref, pw_ref, y_ref, toep_ref, sc_ref, hin_ref, hc_ref, *, rows):
    R = rows

    @pl.when((pl.program_id(1) == 0) & (pl.program_id(2) == 0))
    def _():
        toep_ref[...] = jnp.zeros_like(toep_ref)
        for s in range(S5_CHUNK):
            for l in range(s, S5_CHUNK):
                toep_ref[s * LANES:(s + 1) * LANES, l * LANES:(l + 1) * LANES] = bd_ref[0, l - s]

    @pl.when(pl.program_id(2) == 0)
    def _():
        hc_ref[...] = jnp.zeros_like(hc_ref)

    parts = [u_ref[pl.ds(s, R, stride=S5_CHUNK), :].astype(BF16) for s in range(S5_CHUNK)]
    uf = jnp.concatenate(parts, axis=1)
    y_intra = jnp.dot(uf, toep_ref[...], preferred_element_type=F32)
    sc_ref[...] = jnp.dot(uf, w_ref[0], preferred_element_type=F32)
    pw = pw_ref[0]
    half = S5_TILE_GROUPS * S5_STATE
    grp = S5_SCAN_ROWS
    row = lax.broadcasted_iota(jnp.int32, (grp, pw.shape[1]), 0)

    def times(m, z):
        return pw[m:m + 1] * z + pw[grp + 1 + m:grp + 2 + m] * pltpu.roll(z, half, 1)

    def step(g, h):
        rows8 = pl.ds(pl.multiple_of(g * grp, grp), grp)
        x = sc_ref[rows8, :]
        for sh in (1, 2, 4):
            x = x + jnp.where(row >= sh, times(sh, pltpu.roll(x, sh, 0)), 0.0)
        hb = jnp.broadcast_to(h, x.shape)
        a_j_h = pw[0:grp] * hb + pw[grp + 1:2 * grp + 1] * pltpu.roll(hb, half, 1)
        hin_ref[rows8, :] = a_j_h + jnp.where(row >= 1, pltpu.roll(x, 1, 0), 0.0)
        return times(grp, h) + x[grp - 1:grp]

    hc_ref[...] = lax.fori_loop(0, R // grp, step, hc_ref[...])
    y = y_intra + jnp.dot(hin_ref[...].astype(BF16), v_ref[0], preferred_element_type=F32)
    for l in range(S5_CHUNK):
        y_ref[pl.ds(l, R, stride=S5_CHUNK), :] = y[:, l * LANES:(l + 1) * LANES]


def _s5_tables(lam_re, lam_im, log_dt, b_re, b_im, c_re, c_im):
    f32 = F32
    lc = S5_CHUNK
    tg = S5_TILE_GROUPS
    nt = S5_GROUPS // tg
    lr = jnp.minimum(lam_re.astype(f32), -1e-4)[None, :]
    li = lam_im.astype(f32)[None, :]
    dt = jnp.exp(log_dt.astype(f32))[:, None]
    mag = jnp.exp(lr * dt)
    ab_re, ab_im = mag * jnp.cos(li * dt), mag * jnp.sin(li * dt)
    den = lr * lr + li * li
    nr, ni = ab_re - 1.0, ab_im
    f_re, f_im = (nr * lr + ni * li) / den, (ni * lr - nr * li) / den
    b_re, b_im = b_re.astype(f32), b_im.astype(f32)
    bb_re = f_re[..., None] * b_re - f_im[..., None] * b_im
    bb_im = f_re[..., None] * b_im + f_im[..., None] * b_re
    tau = jnp.arange(lc + 1, dtype=f32)[:, None, None]
    pw_mag = jnp.exp(tau * (lr * dt)[None])
    pw_re = pw_mag * jnp.cos(tau * (li * dt)[None])
    pw_im = pw_mag * jnp.sin(tau * (li * dt)[None])
    c_re, c_im = c_re.astype(f32), c_im.astype(f32)
    cl_re = c_re[None] * pw_re[:, :, None, :] - c_im[None] * pw_im[:, :, None, :]
    cl_im = c_re[None] * pw_im[:, :, None, :] + c_im[None] * pw_re[:, :, None, :]
    hp = lax.Precision.HIGHEST
    kern = (jnp.einsum('tgop,gpi->tgoi', cl_re[:lc], bb_re, precision=hp)
            - jnp.einsum('tgop,gpi->tgoi', cl_im[:lc], bb_im, precision=hp))
    eye = jnp.eye(tg, dtype=f32)
    toep = jnp.einsum('tTgoi,gh->Ttgiho', kern.reshape(lc, nt, tg, S5_GROUP, S5_GROUP), eye)
    toep = toep.reshape(nt, lc, LANES, LANES)
    rev = pw_re[lc - 1 - np.arange(lc)], pw_im[lc - 1 - np.arange(lc)]
    wr = rev[0][..., None] * bb_re[None] - rev[1][..., None] * bb_im[None]
    wi = rev[0][..., None] * bb_im[None] + rev[1][..., None] * bb_re[None]
    wst = jnp.stack([wr, wi], 0).reshape(2, lc, nt, tg, S5_STATE, S5_GROUP)
    w_in = jnp.einsum('rsTgpi,gh->Tsgirhp', wst, eye).reshape(nt, lc * LANES, 2 * tg * S5_STATE)
    vst = jnp.stack([cl_re[1:], -cl_im[1:]], 0).reshape(2, lc, nt, tg, S5_GROUP, S5_STATE)
    v_out = jnp.einsum('rlTgop,gh->Trgplho', vst, eye).reshape(nt, 2 * tg * S5_STATE, lc * LANES)
    m = (lc * jnp.arange(S5_SCAN_ROWS + 1, dtype=f32))[:, None, None]
    am_mag = jnp.exp(m * (lr * dt)[None])
    am_re = (am_mag * jnp.cos(m * (li * dt)[None])).reshape(S5_SCAN_ROWS + 1, nt, tg * S5_STATE).transpose(1, 0, 2)
    am_im = (am_mag * jnp.sin(m * (li * dt)[None])).reshape(S5_SCAN_ROWS + 1, nt, tg * S5_STATE).transpose(1, 0, 2)
    pw = jnp.concatenate([jnp.concatenate([am_re, am_re], -1), jnp.concatenate([-am_im, am_im], -1)], axis=1)
    pw = jnp.pad(pw, ((0, 0), (0, 24 - pw.shape[1]), (0, 0)))
    return toep.astype(BF16), w_in.astype(BF16), v_out.astype(BF16), pw


def _s5_scan(u, bsz, seq, tables, rows):
    toep, w_in, v_out, pw = tables
    nt = S5_GROUPS // S5_TILE_GROUPS
    lc = S5_CHUNK
    rows = min(rows, seq // lc)
    nblk = seq // (lc * rows)
    tok = rows * lc
    ns = 2 * S5_TILE_GROUPS * S5_STATE
    return pl.pallas_call(
        functools.partial(_s5_scan_kernel, rows=rows),
        grid=(nt, bsz, nblk),
        in_specs=[
            pl.BlockSpec((tok, LANES), lambda t, b, r: (b * nblk + r, t)),
            pl.BlockSpec((1, lc, LANES, LANES), lambda t, b, r: (t, 0, 0, 0)),
            pl.BlockSpec((1, lc * LANES, ns), lambda t, b, r: (t, 0, 0)),
            pl.BlockSpec((1, ns, lc * LANES), lambda t, b, r: (t, 0, 0)),
            pl.BlockSpec((1, 24, ns), lambda t, b, r: (t, 0, 0)),
        ],
        out_specs=pl.BlockSpec((tok, LANES), lambda t, b, r: (b * nblk + r, t)),
        out_shape=jax.ShapeDtypeStruct((bsz * seq, S5_WIDTH), F32),
        scratch_shapes=[pltpu.VMEM((lc * LANES, lc * LANES), BF16), pltpu.VMEM((rows, ns), F32),
                        pltpu.VMEM((rows, ns), F32), pltpu.VMEM((1, ns), F32)],
        compiler_params=_cparams(("arbitrary", "arbitrary", "arbitrary")),
        name="s5_scan",
    )(u, toep, w_in, v_out, pw)


def _s5_post_kernel(y_ref, u_ref, d_ref, w_ref, b_ref, o_ref):
    g = jax.nn.gelu(y_ref[...] + d_ref[...] * u_ref[...], approximate=True)
    gate = jnp.dot(g.astype(BF16), w_ref[...], preferred_element_type=F32) + b_ref[...]
    o_ref[...] = (g * jax.nn.sigmoid(gate)).astype(o_ref.dtype)


def _s5_post(y, u, d_skip, w_glu, b_glu, tm=512):
    t = y.shape[0]
    tm = min(tm, t)
    row = pl.BlockSpec((tm, S5_WIDTH), lambda i: (i, 0))
    vec = pl.BlockSpec((1, S5_WIDTH), lambda i: (0, 0))
    return pl.pallas_call(
        _s5_post_kernel,
        grid=(t // tm,),
        in_specs=[row, row, vec, pl.BlockSpec((S5_WIDTH, S5_WIDTH), lambda i: (0, 0)), vec],
        out_specs=row,
        out_shape=jax.ShapeDtypeStruct((t, S5_WIDTH), BF16),
        compiler_params=_cparams(("parallel",)),
        name="s5_post",
    )(y, u, d_skip.astype(F32)[None, :], w_glu.astype(BF16), b_glu.astype(F32)[None, :])


def _ret_chunk(q_ref, k_ref, v_ref, g_ref, cos_ref, sin_ref, intra_ref, dend_ref, osc_ref, nw_ref,
               o_ref, r_ref, *, chunk_decay):
    L = RET_CHUNK

    cos = cos_ref[...]
    sin = sin_ref[...]
    lane = lax.broadcasted_iota(jnp.int32, (L, LANES), 1)
    rowi = lax.broadcasted_iota(jnp.int32, (LANES, L), 0)
    half = RET_QK_DIM // 2
    outs = []
    for t in range(RET_HEADS // 2):
        sl = slice(t * LANES, (t + 1) * LANES)
        qt = q_ref[:, sl].astype(F32)
        kt = k_ref[:, sl].astype(F32)
        qr = qt * cos + pltpu.roll(qt, 64, 1) * sin
        kr = (kt * cos + pltpu.roll(kt, 64, 1) * sin) * (RET_QK_DIM ** -0.5)
        kr_b = kr.astype(BF16)
        kr_t = kr.T
        for sub in range(2):
            h = 2 * t + sub
            vs = slice(h * RET_V_DIM, (h + 1) * RET_V_DIM)
            qm = jnp.where(((lane // half) % 2) == sub, qr, 0.0).astype(BF16)
            s = lax.dot_general(qm, kr_b, (((1,), (1,)), ((), ())), preferred_element_type=F32)
            p = (s * intra_ref[h]).astype(BF16)
            vh = v_ref[:, vs]
            state = r_ref[h]
            y_in = jnp.dot(p, vh, preferred_element_type=F32)
            y_off = jnp.dot(qm, state.astype(BF16), preferred_element_type=F32) * osc_ref[:, vs]
            ktm = jnp.where(((rowi // half) % 2) == sub, kr_t * dend_ref[h:h + 1, :], 0.0).astype(BF16)
            r_ref[h] = state * chunk_decay[h] + jnp.dot(ktm, vh, preferred_element_type=F32)
            y = y_in + y_off
            mu = jnp.mean(y, -1, keepdims=True)
            yc = y - mu
            var = jnp.mean(yc * yc, -1, keepdims=True)
            yn = yc * lax.rsqrt(var + LN_EPS) * nw_ref[:, vs]
            outs.append(_silu(g_ref[:, vs].astype(F32)) * yn)
    o_ref[...] = jnp.concatenate(outs, axis=1).astype(o_ref.dtype)


def _ret_mixer(proj, bsz, seq, positions, norm_w):
    L = RET_CHUNK
    lb = min(L * MIXER_CHUNKS, seq)
    nc = seq // lb
    half = RET_QK_DIM // 2
    inv_freq = 1.0 / (ROPE_BASE ** (jnp.arange(half, dtype=F32) / half))
    ang = positions.astype(F32).reshape(bsz * seq, 1) * inv_freq[None, :]
    cos32, sin32 = jnp.cos(ang), jnp.sin(ang)
    cos_t = jnp.concatenate([cos32] * 4, axis=1)
    sin_t = jnp.concatenate([-sin32, -sin32, sin32, sin32], axis=1)
    log_gamma = np.log1p(-np.exp2(-5.0 - np.arange(RET_HEADS, dtype=np.float64)))
    pos = np.arange(L, dtype=np.float64)
    rel = pos[:, None] - pos[None, :]
    intra = np.where(rel[None] >= 0, np.exp(rel[None] * log_gamma[:, None, None]), 0.0).astype(np.float32)
    dend = np.exp((L - 1 - pos)[None, :] * log_gamma[:, None]).astype(np.float32)
    osc = np.repeat(np.exp((pos + 1)[:, None] * log_gamma[None, :]), RET_V_DIM, axis=1).astype(np.float32)
    chunk_decay = tuple(float(v) for v in np.exp(L * log_gamma))
    rows = lambda b, c: b * nc + c
    full = lambda shape: pl.BlockSpec(shape, lambda b, c: (0,) * len(shape))
    return pl.pallas_call(
        _chunked_kernel(functools.partial(_ret_chunk, chunk_decay=chunk_decay), 6, 4, L),
        grid=(bsz, nc),
        in_specs=[
            pl.BlockSpec((lb, 512), lambda b, c: (rows(b, c), _CB_RQ)),
            pl.BlockSpec((lb, 512), lambda b, c: (rows(b, c), _CB_RK)),
            pl.BlockSpec((lb, 1024), lambda b, c: (rows(b, c), _CB_RV)),
            pl.BlockSpec((lb, 1024), lambda b, c: (rows(b, c), _CB_RG)),
            pl.BlockSpec((lb, LANES), lambda b, c: (rows(b, c), 0)),
            pl.BlockSpec((lb, LANES), lambda b, c: (rows(b, c), 0)),
            full((RET_HEADS, L, L)), full((RET_HEADS, L)), full((L, RET_WIDTH)), full((1, RET_WIDTH)),
        ],
        out_specs=pl.BlockSpec((lb, RET_WIDTH), lambda b, c: (rows(b, c), 0)),
        out_shape=jax.ShapeDtypeStruct((bsz * seq, RET_WIDTH), BF16),
        scratch_shapes=[pltpu.VMEM((RET_HEADS, LANES, RET_V_DIM), F32)],
        compiler_params=_cparams(("parallel", "arbitrary")),
        name="retention_mixer",
    )(proj, proj, proj, proj, cos_t, sin_t, jnp.asarray(intra), jnp.asarray(dend), jnp.asarray(osc),
      norm_w.astype(F32)[None, :])


def _gla_chunk(q_ref, k_ref, v_ref, r_ref, sm_ref, wa_ref, ba_ref, tblk_ref, nw_ref, o_ref, s_ref):
    LB = GLA_BLOCK
    LC = GLA_CHUNK

    logit = jnp.dot(sm_ref[...], wa_ref[...], preferred_element_type=F32, precision=HIGHEST) + ba_ref[...]
    la = (jnp.minimum(logit, 0.0) - jnp.log1p(jnp.exp(-jnp.abs(logit)))) * (1.0 / GLA_TAU)
    b = jnp.dot(tblk_ref[...], la, preferred_element_type=F32, precision=HIGHEST)
    rowf = lax.broadcasted_iota(jnp.int32, (LB, GLA_QK), 0)
    b_end = jnp.where(rowf < LC, b[LC - 1:LC], b[LB - 1:LB])
    q = q_ref[...].astype(F32) * (GLA_QK_DIM ** -0.5)
    k = k_ref[...].astype(F32)
    q_dec = q * jnp.exp(b)
    k_inv = (k * jnp.exp(-b)).astype(BF16)
    k_dec = k * jnp.exp(b_end - b)
    row = lax.broadcasted_iota(jnp.int32, (LB, LB), 0)
    col = lax.broadcasted_iota(jnp.int32, (LB, LB), 1)
    blk_causal = (row >= col) & ((row // LC) == (col // LC))
    first_rows = row < LC
    first_vrows = lax.broadcasted_iota(jnp.int32, (LB, GLA_V_DIM), 0) < LC
    outs = []
    for h in range(GLA_HEADS):
        sl = slice(h * GLA_QK_DIM, (h + 1) * GLA_QK_DIM)
        vs = slice(h * GLA_V_DIM, (h + 1) * GLA_V_DIM)
        qd = q_dec[:, sl]
        qd_b = qd.astype(BF16)
        att = lax.dot_general(qd_b, k_inv[:, sl], (((1,), (1,)), ((), ())), preferred_element_type=F32)
        att = jnp.where(blk_causal, att, 0.0).astype(BF16)
        vh = v_ref[:, vs]
        y = jnp.dot(att, vh, preferred_element_type=F32)
        s0 = s_ref[h]
        kd_t = k_dec[:, sl].T.astype(BF16)
        kv0 = jnp.dot(kd_t, jnp.where(first_vrows, vh, jnp.zeros_like(vh)), preferred_element_type=F32)
        kv1 = jnp.dot(kd_t, jnp.where(first_vrows, jnp.zeros_like(vh), vh), preferred_element_type=F32)
        b_t = b[:, sl].T
        s1 = s0 * jnp.exp(b_t[:, LC - 1:LC]) + kv0
        s_ref[h] = s1 * jnp.exp(b_t[:, LB - 1:LB]) + kv1
        y = y + jnp.dot(jnp.where(first_rows, qd, 0.0).astype(BF16), s0.astype(BF16), preferred_element_type=F32)
        y = y + jnp.dot(jnp.where(first_rows, 0.0, qd).astype(BF16), s1.astype(BF16), preferred_element_type=F32)
        ms = jnp.mean(y * y, -1, keepdims=True)
        yn = y * lax.rsqrt(ms + RMS_EPS) * nw_ref[:, vs]
        outs.append(yn * _silu(r_ref[:, vs].astype(F32)))
    o_ref[...] = jnp.concatenate(outs, axis=1).astype(o_ref.dtype)


def _gla_mixer(proj, small, bsz, seq, w_alpha, b_alpha, norm_w):
    LB = GLA_BLOCK
    lb = min(LB * MIXER_CHUNKS, seq)
    nb = seq // lb
    wa = jnp.zeros((LANES, GLA_QK), F32).at[SSD_HEADS:SSD_HEADS + GLA_GATE_RANK].set(w_alpha.astype(F32))
    idx = np.arange(LB)
    tblk = ((idx[:, None] >= idx[None, :]) & ((idx[:, None] // GLA_CHUNK) == (idx[None, :] // GLA_CHUNK)))
    rows = lambda b, c: b * nb + c
    full = lambda shape: pl.BlockSpec(shape, lambda b, c: (0,) * len(shape))
    return pl.pallas_call(
        _chunked_kernel(_gla_chunk, 5, 4, LB),
        grid=(bsz, nb),
        in_specs=[
            pl.BlockSpec((lb, 512), lambda b, c: (rows(b, c), _CB_GQ)),
            pl.BlockSpec((lb, 512), lambda b, c: (rows(b, c), _CB_GK)),
            pl.BlockSpec((lb, 1024), lambda b, c: (rows(b, c), _CB_GV)),
            pl.BlockSpec((lb, 1024), lambda b, c: (rows(b, c), _CB_GR)),
            pl.BlockSpec((lb, LANES), lambda b, c: (rows(b, c), 0)),
            full((LANES, GLA_QK)), full((1, GLA_QK)), full((LB, LB)), full((1, GLA_WIDTH)),
        ],
        out_specs=pl.BlockSpec((lb, GLA_WIDTH), lambda b, c: (rows(b, c), 0)),
        out_shape=jax.ShapeDtypeStruct((bsz * seq, GLA_WIDTH), BF16),
        scratch_shapes=[pltpu.VMEM((GLA_HEADS, GLA_QK_DIM, GLA_V_DIM), F32)],
        compiler_params=_cparams(("parallel", "arbitrary")),
        name="gla_mixer",
    )(proj, proj, proj, proj, small, wa, b_alpha.astype(F32)[None, :], jnp.asarray(tblk.astype(np.float32)),
      norm_w.astype(F32)[None, :])


def _merge_kernel(ya_ref, yb_ref, yc_ref, yd_ref, g0_ref, g1_ref, g2_ref, g3_ref, wb_ref, o_ref):
    acc = None
    for n, (y_ref, g_ref) in enumerate(((ya_ref, g0_ref), (yb_ref, g1_ref), (yc_ref, g2_ref), (yd_ref, g3_ref))):
        br = jnp.dot(y_ref[...], wb_ref[n], preferred_element_type=F32)
        term = jax.nn.sigmoid(g_ref[...].astype(F32)) * br
        acc = term if acc is None else acc + term
    o_ref[...] = acc.astype(o_ref.dtype)


def _merge(ys, proj, w_branch, tm=1024, tn=512):
    t = ys[0].shape[0]
    tm = min(tm, t)
    nj = D_MODEL // tn
    yspec = pl.BlockSpec((tm, BRANCH_WIDTH), lambda i, j: (i, 0))
    gspec = lambda n: pl.BlockSpec((tm, tn), lambda i, j: (i, n * nj + j))
    return pl.pallas_call(
        _merge_kernel,
        grid=(t // tm, nj),
        in_specs=[yspec] * 4 + [gspec(n) for n in range(N_BRANCH)]
                 + [pl.BlockSpec((N_BRANCH, BRANCH_WIDTH, tn), lambda i, j: (0, 0, j))],
        out_specs=pl.BlockSpec((tm, tn), lambda i, j: (i, j)),
        out_shape=jax.ShapeDtypeStruct((t, D_MODEL), BF16),
        compiler_params=_cparams(("parallel", "parallel")),
        name="branch_merge",
    )(*ys, proj, proj, proj, proj, w_branch.astype(BF16))


def _row_halves(tm):
    half = tm // 2 if tm % 32 == 0 else tm
    return [(r0, slice(r0, r0 + half)) for r0 in range(0, tm, half)]


def _proj_ln_kernel(m_ref, w_ref, h_ref, lw_ref, lb_ref, o_ref, op_ref):
    for r0, rs in _row_halves(m_ref.shape[0]):
        mix = jnp.dot(m_ref[rs, :], w_ref[...], preferred_element_type=F32)
        out = _layer_norm(DEEPNORM_ALPHA * h_ref[rs, :] + mix, lw_ref[...], lb_ref[...])
        o_ref[rs, :] = out
        _store_token_tiles(op_ref, r0, _pack_rows(out.astype(BF16)))


def _proj_ln(merged, w_out, h, ln_w, ln_b, tm=512):
    t = h.shape[0]
    tm = min(tm, t)
    row = pl.BlockSpec((tm, D_MODEL), lambda i: (i, 0))
    vec = pl.BlockSpec((1, D_MODEL), lambda i: (0, 0))
    return pl.pallas_call(
        _proj_ln_kernel,
        grid=(t // tm,),
        in_specs=[row, pl.BlockSpec((D_MODEL, D_MODEL), lambda i: (0, 0)), row, vec, vec],
        out_specs=[row, pl.BlockSpec((tm * TILE_ROWS, LANES), lambda i: (i, 0))],
        out_shape=[jax.ShapeDtypeStruct((t, D_MODEL), F32), jax.ShapeDtypeStruct((t * TILE_ROWS, LANES), jnp.uint32)],
        compiler_params=_cparams(("parallel",)),
        name="out_proj_ln",
    )(merged, w_out.astype(BF16), h, ln_w.astype(F32)[None, :], ln_b.astype(F32)[None, :])


def _router_kernel(h_ref, w_ref, b_ref, idx_ref, gate_ref, cnt_ref, run_ref):
    @pl.when(pl.program_id(0) == 0)
    def _():
        run_ref[...] = jnp.zeros_like(run_ref)

    logits = _dot3(w_ref[...], h_ref[...], ((1,), (1,))) + b_ref[...]
    eid = lax.broadcasted_iota(jnp.int32, logits.shape, 0)
    vals = logits
    run = run_ref[...]
    top_v, top_i = [], []
    for _ in range(TOP_K):
        m = jnp.max(vals, axis=0, keepdims=True)
        sel = jnp.min(jnp.where(vals == m, eid, N_EXPERTS), axis=0, keepdims=True)
        hit = eid == sel
        top_v.append(m)
        top_i.append(sel)
        vals = jnp.where(hit, -jnp.inf, vals)
        run = run + jnp.sum(jnp.where(hit, 1.0, 0.0), axis=1, keepdims=True)
    run_ref[...] = run
    ex = [jnp.exp(v - top_v[0]) for v in top_v]
    den = ex[0] + ex[1] + ex[2] + ex[3]
    zi = jnp.zeros_like(top_i[0])
    zf = jnp.zeros_like(den)
    idx_ref[...] = jnp.concatenate(top_i + [zi] * (8 - TOP_K), axis=0)
    gate_ref[...] = jnp.concatenate([e / den for e in ex] + [zf] * (8 - TOP_K), axis=0)
    cnt_ref[...] = jnp.broadcast_to(run, cnt_ref.shape).astype(jnp.int32)


def _router(h, router_w, router_b, tm=512):
    t = h.shape[0]
    tm = min(tm, t)
    tok = pl.BlockSpec((8, tm), lambda i: (0, i))
    return pl.pallas_call(
        _router_kernel,
        grid=(t // tm,),
        in_specs=[pl.BlockSpec((tm, D_MODEL), lambda i: (i, 0)),
                  pl.BlockSpec((N_EXPERTS, D_MODEL), lambda i: (0, 0)),
                  pl.BlockSpec((N_EXPERTS, 1), lambda i: (0, 0))],
        out_specs=[tok, tok, pl.BlockSpec((N_EXPERTS, LANES), lambda i: (0, 0))],
        out_shape=[jax.ShapeDtypeStruct((8, t), jnp.int32), jax.ShapeDtypeStruct((8, t), F32),
                   jax.ShapeDtypeStruct((N_EXPERTS, LANES), jnp.int32)],
        scratch_shapes=[pltpu.VMEM((N_EXPERTS, 1), F32)],
        compiler_params=_cparams(("arbitrary",)),
        name="router_topk",
    )(h, router_w.astype(F32).T, router_b.astype(F32)[:, None])


def _pack_rows(x):
    n = x.shape[1] // 2
    lo = pltpu.bitcast(x[:, :n].astype(F32), jnp.uint32) >> 16
    hi = pltpu.bitcast(x[:, n:].astype(F32), jnp.uint32)
    return lo | hi


def _unpack_rows(w):
    lo = pltpu.bitcast(w << 16, F32)
    hi = pltpu.bitcast(w & jnp.uint32(0xFFFF0000), F32)
    return lo, hi


TILE_ROWS = (D_MODEL // 2) // LANES


def _store_token_tiles(ref, first, words):
    m = words.shape[0]
    for s in range(TILE_ROWS):
        ref[pl.ds(first * TILE_ROWS + s, m, stride=TILE_ROWS), :] = words[:, s * LANES:(s + 1) * LANES]


def _load_token_tiles(ref, first, m):
    return jnp.concatenate([ref[pl.ds(first * TILE_ROWS + s, m, stride=TILE_ROWS), :] for s in range(TILE_ROWS)],
                           axis=1)


def _expert_kernel(be_ref, nb_ref, tab_ref, hp_ref, wgu_ref, bgu_ref, wd_ref, bd_ref, y_ref,
                   idx_ref, xbuf_ref, ybuf_ref, isem, gsem, ssem):
    i = pl.program_id(0)
    n_used = nb_ref[0]
    blk = MOE_BLOCK
    tr = TILE_ROWS

    def idx_copy(b):
        return pltpu.make_async_copy(tab_ref.at[b], idx_ref.at[b % 4], isem.at[b % 4])

    def tile(r):
        return pl.ds(r * tr if isinstance(r, int) else pl.multiple_of(r * tr, tr), tr)

    def queue(r):
        return r % 2 if isinstance(r, int) else 0

    def gather_start(b, r):
        src = pl.multiple_of(idx_ref[b % 4, r], tr)
        pltpu.make_async_copy(hp_ref.at[pl.ds(src, tr)], xbuf_ref.at[b % 2, tile(r)],
                              gsem.at[b % 2]).start(priority=queue(r))

    def scatter_start(b, r):
        dst = pl.multiple_of(idx_ref[b % 4, blk + r], tr)
        pltpu.make_async_copy(ybuf_ref.at[b % 2, tile(r)], y_ref.at[pl.ds(dst, tr)],
                              ssem.at[b % 2]).start(priority=queue(r))

    def gather_wait(b):
        pltpu.make_async_copy(hp_ref.at[pl.ds(0, blk * tr)], xbuf_ref.at[b % 2], gsem.at[b % 2]).wait()

    def scatter_wait(b):
        pltpu.make_async_copy(ybuf_ref.at[b % 2], y_ref.at[pl.ds(0, blk * tr)], ssem.at[b % 2]).wait()

    def ffn(b):
        lo, hi = _unpack_rows(_load_token_tiles(xbuf_ref.at[b % 2], 0, blk))
        xb = jnp.concatenate([lo.astype(BF16), hi.astype(BF16)], axis=1)
        hgu = jnp.dot(xb, wgu_ref[0], preferred_element_type=F32) + bgu_ref[0]
        g = jnp.minimum(hgu[:, :D_EXPERT], SWIGLU_LIMIT)
        u = jnp.clip(hgu[:, D_EXPERT:], -SWIGLU_LIMIT, SWIGLU_LIMIT)
        act = (u + 1.0) * (g * jax.nn.sigmoid(SWIGLU_ALPHA * g))
        out = jnp.dot(act.astype(BF16), wd_ref[0], preferred_element_type=F32) + bd_ref[0]
        packed = _pack_rows(out.astype(BF16))

        @pl.when(i >= 2)
        def _():
            scatter_wait(i - 2)

        _store_token_tiles(ybuf_ref.at[b % 2], 0, packed)

    @pl.when(i == 0)
    def _():
        idx_copy(0).start()
        ybuf_ref[1] = jnp.zeros(ybuf_ref.shape[1:], ybuf_ref.dtype)
        spare = pltpu.make_async_copy(ybuf_ref.at[1], y_ref.at[pl.ds(y_ref.shape[0] - blk * tr, blk * tr)],
                                      ssem.at[1])
        spare.start()
        idx_copy(0).wait()
        lax.fori_loop(0, blk, lambda r, c: (gather_start(0, r), c)[1], 0, unroll=8)
        idx_copy(1).start()
        spare.wait()

    @pl.when(i <= n_used)
    def _():
        idx_copy(i + 1).wait()
        gather_wait(i)

    @pl.when(i < n_used)
    def _():
        idx_copy(i + 2).start()

    @pl.when(i == 0)
    def _():
        for r in range(blk):
            gather_start(1, r)
        ffn(0)

    @pl.when((i > 0) & (i < n_used))
    def _():
        for r in range(blk):
            gather_start(i + 1, r)
            scatter_start(i - 1, r)
        ffn(i)

    @pl.when(i == n_used)
    def _():
        @pl.when(i >= 2)
        def _():
            scatter_wait(i - 2)

        lax.fori_loop(0, blk, lambda r, c: (scatter_start(i - 1, r), c)[1], 0, unroll=8)
        scatter_wait(i - 1)


def _expert_params(w_gate_up, b_gate_up, w_down, b_down):
    n = w_gate_up.shape[0] * N_EXPERTS
    return (w_gate_up.astype(BF16).reshape(n, D_MODEL, 2 * D_EXPERT), b_gate_up.astype(F32).reshape(n, 1, 2 * D_EXPERT),
            w_down.astype(BF16).reshape(n, D_EXPERT, D_MODEL), b_down.astype(F32).reshape(n, 1, D_MODEL))


def _experts(h_packed, table, block_expert, n_used, expert_params):
    w_gate_up, b_gate_up, w_down, b_down = expert_params
    blk = MOE_BLOCK
    n_blocks = block_expert.shape[0]
    n_out = TOP_K * (h_packed.shape[0] // TILE_ROWS) + blk
    grid_spec = pltpu.PrefetchScalarGridSpec(
        num_scalar_prefetch=2,
        grid=(n_blocks,),
        in_specs=[
            pl.BlockSpec(memory_space=pl.ANY),
            pl.BlockSpec(memory_space=pl.ANY),
            pl.BlockSpec((1, D_MODEL, 2 * D_EXPERT), lambda i, be, nb: (be[i], 0, 0)),
            pl.BlockSpec((1, 1, 2 * D_EXPERT), lambda i, be, nb: (be[i], 0, 0)),
            pl.BlockSpec((1, D_EXPERT, D_MODEL), lambda i, be, nb: (be[i], 0, 0)),
            pl.BlockSpec((1, 1, D_MODEL), lambda i, be, nb: (be[i], 0, 0)),
        ],
        out_specs=pl.BlockSpec(memory_space=pl.ANY),
        scratch_shapes=[pltpu.SMEM((4, 2 * blk), jnp.int32), pltpu.VMEM((2, blk * TILE_ROWS, LANES), jnp.uint32),
                        pltpu.VMEM((2, blk * TILE_ROWS, LANES), jnp.uint32), pltpu.SemaphoreType.DMA((4,)),
                        pltpu.SemaphoreType.DMA((2,)), pltpu.SemaphoreType.DMA((2,))],
    )
    return pl.pallas_call(
        _expert_kernel,
        grid_spec=grid_spec,
        out_shape=jax.ShapeDtypeStruct((n_out * TILE_ROWS, LANES), jnp.uint32),
        compiler_params=_cparams(("arbitrary",)),
        name="expert_ffn",
    )(block_expert, n_used, table, h_packed, w_gate_up, b_gate_up, w_down, b_down)


def _combine_kernel(y0_ref, y1_ref, y2_ref, y3_ref, gate_ref, h_ref, lw_ref, lb_ref, o_ref, ob_ref, *, tm):
    rows = min(tm, 16)
    for r0 in range(0, tm, rows):
        rs = slice(r0, r0 + rows)
        gates = gate_ref[rs, :]
        ffn = None
        for k, y_ref in enumerate((y0_ref, y1_ref, y2_ref, y3_ref)):
            lo, hi = _unpack_rows(_load_token_tiles(y_ref, r0, rows))
            term = gates[:, k:k + 1] * jnp.concatenate([lo, hi], axis=1)
            ffn = term if ffn is None else ffn + term
        out = _layer_norm(DEEPNORM_ALPHA * h_ref[rs, :] + ffn, lw_ref[...], lb_ref[...])
        o_ref[rs, :] = out
        ob_ref[rs, :] = out.astype(BF16)


def _combine(y, gates, h, ln_w, ln_b, tm=512):
    t = h.shape[0]
    tm = min(tm, t)
    nt = t // tm
    row = pl.BlockSpec((tm, D_MODEL), lambda i: (i, 0))
    vec = pl.BlockSpec((1, D_MODEL), lambda i: (0, 0))
    yspec = lambda k: pl.BlockSpec((tm * TILE_ROWS, LANES), lambda i: (k * nt + i, 0))
    return pl.pallas_call(
        functools.partial(_combine_kernel, tm=tm),
        grid=(nt,),
        in_specs=[yspec(k) for k in range(TOP_K)] + [pl.BlockSpec((tm, 8), lambda i: (i, 0)), row, vec, vec],
        out_specs=[row, row],
        out_shape=[jax.ShapeDtypeStruct((t, D_MODEL), F32), jax.ShapeDtypeStruct((t, D_MODEL), BF16)],
        compiler_params=_cparams(("parallel",)),
        name="moe_combine_ln",
    )(y, y, y, y, gates, h, ln_w.astype(F32)[None, :], ln_b.astype(F32)[None, :])


def _moe(h, h_packed, router_w, router_b, expert_params, layer, ln_w, ln_b):
    t = h.shape[0]
    blk = MOE_BLOCK
    top_idx, top_gate, counts = _router(h, router_w, router_b)
    counts = counts[:, 0]
    n_assign = TOP_K * t
    n_blocks = n_assign // blk + N_EXPERTS
    a_ids = jnp.arange(n_assign, dtype=jnp.int32)
    a_sorted = jnp.sort(top_idx[:TOP_K].reshape(-1) * n_assign + a_ids) % n_assign
    padded = (counts + blk - 1) // blk * blk
    pad_end = jnp.cumsum(padded)
    grp_end = jnp.cumsum(counts)
    blocks = jnp.arange(n_blocks, dtype=jnp.int32)
    block_expert = jnp.sum(((pad_end // blk)[None, :] <= blocks[:, None]).astype(jnp.int32), axis=1)
    block_expert = jnp.minimum(block_expert, N_EXPERTS - 1)
    n_used = (pad_end[-1] // blk).astype(jnp.int32).reshape(1)
    is_e = block_expert[:, None] == jnp.arange(N_EXPERTS, dtype=jnp.int32)[None, :]
    pick = lambda v: jnp.sum(jnp.where(is_e, v[None, :], 0), axis=1)
    offset = blocks * blk - pick(pad_end - padded)
    first = pick(grp_end - counts) + offset
    n_valid = jnp.clip(pick(counts) - offset, 0, blk)
    j = jnp.arange(blk, dtype=jnp.int32)[None, :]
    valid = j < n_valid[:, None]
    a_rows = a_sorted[jnp.clip(first[:, None] + j, 0, n_assign - 1)]
    tok_rows = jnp.where(valid, a_rows % t, 0)
    out_rows = jnp.where(valid, a_rows, n_assign + j)
    table = (jnp.concatenate([tok_rows, out_rows], axis=1) * TILE_ROWS).astype(jnp.int32)
    table = jnp.concatenate([table, jnp.zeros((2, 2 * blk), jnp.int32)], axis=0)
    y = _experts(h_packed, table, block_expert + layer * N_EXPERTS, n_used, expert_params)
    return _combine(y, top_gate.T, h, ln_w, ln_b)


def _ple_kernel(hb_ref, h_ref, p_ref, wg_ref, wp_ref, lw_ref, lb_ref, o_ref, ob_ref):
    for _, rs in _row_halves(hb_ref.shape[0]):
        gate = jax.nn.sigmoid(jnp.dot(hb_ref[rs, :], wg_ref[...], preferred_element_type=F32))
        emb = jnp.dot(p_ref[rs, :].astype(BF16), wp_ref[...], preferred_element_type=F32)
        out = _layer_norm(DEEPNORM_ALPHA * h_ref[rs, :] + gate * emb, lw_ref[...], lb_ref[...])
        o_ref[rs, :] = out
        ob_ref[rs, :] = out.astype(BF16)


def _ple(hb, h, p_i, w_gate, w_proj, ln_w, ln_b, tm=512):
    t = h.shape[0]
    tm = min(tm, t)
    row = pl.BlockSpec((tm, D_MODEL), lambda i: (i, 0))
    vec = pl.BlockSpec((1, D_MODEL), lambda i: (0, 0))
    return pl.pallas_call(
        _ple_kernel,
        grid=(t // tm,),
        in_specs=[row, row, pl.BlockSpec((tm, PLE_DIM), lambda i: (i, 0)),
                  pl.BlockSpec((D_MODEL, D_MODEL), lambda i: (0, 0)),
                  pl.BlockSpec((PLE_DIM, D_MODEL), lambda i: (0, 0)), vec, vec],
        out_specs=[row, row],
        out_shape=[jax.ShapeDtypeStruct((t, D_MODEL), F32), jax.ShapeDtypeStruct((t, D_MODEL), BF16)],
        compiler_params=_cparams(("parallel",)),
        name="ple_ln",
    )(hb, h, p_i, w_gate.astype(BF16), w_proj.astype(BF16), ln_w.astype(F32)[None, :], ln_b.astype(F32)[None, :])


def _token_mixer(h, hb, bsz, seq, positions, w_in, ssd_conv_w, ssd_conv_b, ssd_dt_bias, ssd_a_log, ssd_d,
                 ssd_norm_w, s5_lambda_re, s5_lambda_im, s5_log_dt, s5_b_re, s5_b_im, s5_c_re, s5_c_im, s5_d,
                 s5_w_glu, s5_b_glu, ret_norm_w, gla_w_alpha, gla_b_alpha, gla_norm_w, w_branch, s5_rows):
    w_main = _main_weight(w_in)
    w_u = w_in[:, _OFF_U:_OFF_U + S5_WIDTH].astype(BF16)
    w_small = jnp.concatenate([w_in[:, _OFF_DT:_OFF_DT + SSD_HEADS], w_in[:, _OFF_GC:_OFF_GC + GLA_GATE_RANK],
                               jnp.zeros((D_MODEL, LANES - SSD_HEADS - GLA_GATE_RANK), w_in.dtype)], axis=1)
    proj = _matmul(hb, w_main, BF16, 1024, 1536, name="in_proj_main")
    u = _matmul(hb, w_u, F32, 1024, 1024, name="in_proj_s5")
    small = _matmul_f32(h, w_small.astype(F32), 512, name="in_proj_small")
    y_a = _ssd_mixer(proj, small, bsz, seq, ssd_conv_w, ssd_conv_b, ssd_dt_bias, ssd_a_log, ssd_d, ssd_norm_w)
    tables = _s5_tables(s5_lambda_re, s5_lambda_im, s5_log_dt, s5_b_re, s5_b_im, s5_c_re, s5_c_im)
    y_b = _s5_post(_s5_scan(u, bsz, seq, tables, s5_rows), u, s5_d, s5_w_glu, s5_b_glu)
    y_c = _ret_mixer(proj, bsz, seq, positions, ret_norm_w)
    y_d = _gla_mixer(proj, small, bsz, seq, gla_w_alpha, gla_b_alpha, gla_norm_w)
    return _merge((y_a, y_b, y_c, y_d), proj, w_branch)


def kernel(x, p, positions, w_in, ssd_conv_w, ssd_conv_b, ssd_dt_bias, ssd_a_log, ssd_d, ssd_norm_w, s5_lambda_re, s5_lambda_im, s5_log_dt, s5_b_re, s5_b_im, s5_c_re, s5_c_im, s5_d, s5_w_glu, s5_b_glu, ret_norm_w, gla_w_alpha, gla_b_alpha, gla_norm_w, w_branch, w_out, ln1_w, ln1_b, router_w, router_b, moe_w_gate_up, moe_b_gate_up, moe_w_down, moe_b_down, ln2_w, ln2_b, ple_w_gate, ple_w_proj, ln3_w, ln3_b):
    s5_rows = 256
    bsz, seq, d = x.shape
    t = bsz * seq
    h = x.reshape(t, d).astype(F32)
    hb = h.astype(BF16)
    expert_params = _expert_params(moe_w_gate_up, moe_b_gate_up, moe_w_down, moe_b_down)
    for i in range(DEPTH):
        merged = _token_mixer(h, hb, bsz, seq, positions, w_in[i], ssd_conv_w[i], ssd_conv_b[i], ssd_dt_bias[i],
                              ssd_a_log[i], ssd_d[i], ssd_norm_w[i], s5_lambda_re[i], s5_lambda_im[i],
                              s5_log_dt[i], s5_b_re[i], s5_b_im[i], s5_c_re[i], s5_c_im[i], s5_d[i],
                              s5_w_glu[i], s5_b_glu[i], ret_norm_w[i], gla_w_alpha[i], gla_b_alpha[i],
                              gla_norm_w[i], w_branch[i], s5_rows)
        h, h_packed = _proj_ln(merged, w_out[i], h, ln1_w[i], ln1_b[i])
        h, hb = _moe(h, h_packed, router_w[i], router_b[i], expert_params, i, ln2_w[i], ln2_b[i])
        h, hb = _ple(hb, h, p[i].reshape(t, PLE_DIM), ple_w_gate[i], ple_w_proj[i], ln3_w[i], ln3_b[i])
    return h.reshape(bsz, seq, d).astype(x.dtype)
```

```python
import functools
import math

import numpy as np
import jax
import jax.numpy as jnp
from jax import lax
from jax.experimental import pallas as pl
from jax.experimental.pallas import tpu as pltpu

F32 = jnp.float32
BF16 = jnp.bfloat16
HIGHEST = lax.Precision.HIGHEST

D_MODEL = 2048
DEPTH = 2
PLE_DIM = 256
N_BRANCH = 4
BRANCH_WIDTH = 1024

SSD_HEADS = 16
SSD_HEAD_DIM = 64
SSD_INNER = SSD_HEADS * SSD_HEAD_DIM
SSD_GROUPS = 2
SSD_STATE = 128
SSD_CONV = 4
SSD_CHUNK = 128

S5_WIDTH = 1024
S5_GROUP = 16
S5_GROUPS = S5_WIDTH // S5_GROUP
S5_STATE = 64
S5_CHUNK = 16
S5_TILE_GROUPS = 8
S5_SCAN_ROWS = 8

RET_HEADS = 8
RET_QK_DIM = 64
RET_V_DIM = 128
RET_QK = RET_HEADS * RET_QK_DIM
RET_WIDTH = RET_HEADS * RET_V_DIM
RET_CHUNK = 128
ROPE_BASE = 10000.0

GLA_HEADS = 4
GLA_QK_DIM = 128
GLA_V_DIM = 256
GLA_QK = GLA_HEADS * GLA_QK_DIM
GLA_WIDTH = GLA_HEADS * GLA_V_DIM
GLA_GATE_RANK = 16
GLA_TAU = 16.0
GLA_CHUNK = 64
GLA_BLOCK = 128

N_EXPERTS = 32
TOP_K = 4
D_EXPERT = 1024
SWIGLU_ALPHA = 1.702
SWIGLU_LIMIT = 7.0
MOE_BLOCK = 512

DEEPNORM_ALPHA = (2 * DEPTH) ** 0.25
LN_EPS = 1e-5
RMS_EPS = 1e-6

LANES = 128
VMEM_LIMIT = 56 * 1024 * 1024

_OFF_Z = 0
_OFF_XS = 1024
_OFF_BC = 2048
_OFF_DT = 2560
_OFF_U = 2576
_OFF_RQ = 3600
_OFF_RK = 4112
_OFF_RV = 4624
_OFF_RG = 5648
_OFF_GQ = 6672
_OFF_GK = 7184
_OFF_GV = 7696
_OFF_GR = 8720
_OFF_GC = 9744
_OFF_GATE = 9760

_CB_Z, _CB_XS, _CB_RV, _CB_RG, _CB_GV, _CB_GR = 8, 9, 10, 11, 12, 13
_CB_BC, _CB_RQ, _CB_RK, _CB_GQ, _CB_GK = 28, 29, 30, 31, 32
N_MAIN = 16896


def _main_weight(w_in):
    def cols(off, n):
        return w_in[:, off:off + n].astype(BF16)

    def rotary(off):
        half = RET_QK_DIM // 2
        w = cols(off, RET_QK).reshape(D_MODEL, RET_HEADS // 2, 2, 2, half)
        return w.transpose(0, 1, 3, 2, 4).reshape(D_MODEL, RET_QK)

    return jnp.concatenate([
        cols(_OFF_GATE, N_BRANCH * D_MODEL), cols(_OFF_Z, 1024), cols(_OFF_XS, 1024), cols(_OFF_RV, 1024),
        cols(_OFF_RG, 1024), cols(_OFF_GV, 1024), cols(_OFF_GR, 1024), cols(_OFF_BC, 512),
        rotary(_OFF_RQ), rotary(_OFF_RK), cols(_OFF_GQ, 512), cols(_OFF_GK, 512)], axis=1)


def _cparams(sem):
    return pltpu.CompilerParams(dimension_semantics=sem, vmem_limit_bytes=VMEM_LIMIT)


def _silu(x):
    return x * jax.nn.sigmoid(x)


def _softplus(x):
    return jnp.maximum(x, 0.0) + jnp.log1p(jnp.exp(-jnp.abs(x)))


def _layer_norm(x, w, b):
    mu = jnp.mean(x, -1, keepdims=True)
    xc = x - mu
    var = jnp.mean(xc * xc, -1, keepdims=True)
    return xc * lax.rsqrt(var + LN_EPS) * w + b


def _mm_kernel(a_ref, b_ref, o_ref):
    o_ref[...] = jnp.dot(a_ref[...], b_ref[...], preferred_element_type=F32).astype(o_ref.dtype)


def _matmul(a, b, out_dtype, tm, tn, name="matmul"):
    m, k = a.shape
    n = b.shape[1]
    tm, tn = min(tm, m), min(tn, n)
    return pl.pallas_call(
        _mm_kernel,
        grid=(m // tm, n // tn),
        in_specs=[pl.BlockSpec((tm, k), lambda i, j: (i, 0)),
                  pl.BlockSpec((k, tn), lambda i, j: (0, j))],
        out_specs=pl.BlockSpec((tm, tn), lambda i, j: (i, j)),
        out_shape=jax.ShapeDtypeStruct((m, n), out_dtype),
        compiler_params=_cparams(("parallel", "parallel")),
        name=name,
    )(a, b)


def _split_bf16(x):
    hi = x.astype(BF16)
    return hi, (x - hi.astype(F32)).astype(BF16)


def _dot3(a, b, dims):
    a_hi, a_lo = _split_bf16(a)
    b_hi, b_lo = _split_bf16(b)
    dg = functools.partial(lax.dot_general, dimension_numbers=(dims, ((), ())), preferred_element_type=F32)
    return dg(a_hi, b_hi) + (dg(a_hi, b_lo) + dg(a_lo, b_hi))


def _mm3_kernel(a_ref, b_ref, o_ref):
    o_ref[...] = _dot3(a_ref[...], b_ref[...], ((1,), (0,)))


def _matmul_f32(a, b, tm, name):
    m, k = a.shape
    n = b.shape[1]
    tm = min(tm, m)
    return pl.pallas_call(
        _mm3_kernel,
        grid=(m // tm,),
        in_specs=[pl.BlockSpec((tm, k), lambda i: (i, 0)), pl.BlockSpec((k, n), lambda i: (0, 0))],
        out_specs=pl.BlockSpec((tm, n), lambda i: (i, 0)),
        out_shape=jax.ShapeDtypeStruct((m, n), F32),
        compiler_params=_cparams(("parallel",)),
        name=name,
    )(a, b)


MIXER_CHUNKS = 4


def _chunked_kernel(chunk_fn, n_row, n_const, length):
    def kernel(*refs):
        row_refs, const_refs = refs[:n_row], refs[n_row:n_row + n_const]
        o_ref, scratch = refs[n_row + n_const], refs[n_row + n_const + 1:]

        @pl.when(pl.program_id(1) == 0)
        def _():
            for ref in scratch:
                ref[...] = jnp.zeros_like(ref)

        for c in range(MIXER_CHUNKS):
            rows = pl.ds(c * length, length)
            chunk_fn(*[r.at[rows] for r in row_refs], *const_refs, o_ref.at[rows], *scratch)

    return kernel


def _causal_conv(x, tail, w, b):
    n_tap = SSD_CONV
    acc = x * w[n_tap - 1:n_tap] + b
    x8 = x[0:8]
    acc8 = x8 * w[n_tap - 1:n_tap] + b
    row8 = lax.broadcasted_iota(jnp.int32, x8.shape, 0)
    for s in range(1, n_tap):
        wk = w[n_tap - 1 - s:n_tap - s]
        acc = acc + pltpu.roll(x, s, 0) * wk
        v8 = jnp.where(row8 < s, pltpu.roll(tail, s, 0), pltpu.roll(x8, s, 0))
        acc8 = acc8 + v8 * wk
    return jnp.concatenate([acc8, acc[8:]], axis=0)


def _ssd_chunk(z_ref, xs_ref, bc_ref, sm_ref, cwx_ref, cwbc_ref, cbx_ref, cbbc_ref, dtb_ref,
               nega_ref, dsk_ref, nw_ref, exp_ref, tril_ref, o_ref, tailx_ref, tailbc_ref, ht_ref):
    L = SSD_CHUNK

    x_raw = xs_ref[...].astype(F32)
    bc_raw = bc_ref[...].astype(F32)
    xs = _silu(_causal_conv(x_raw, tailx_ref[...], cwx_ref[...], cbx_ref[...]))
    bc = _silu(_causal_conv(bc_raw, tailbc_ref[...], cwbc_ref[...], cbbc_ref[...]))
    tailx_ref[...] = x_raw[L - 8:L]
    tailbc_ref[...] = bc_raw[L - 8:L]

    dt = _softplus(sm_ref[...] + dtb_ref[...])
    loga = dt * nega_ref[...]
    acs = jnp.dot(tril_ref[...], loga, preferred_element_type=F32, precision=HIGHEST)
    acs_t = acs.T
    expand = exp_ref[...]
    dt_e = jnp.dot(dt, expand, preferred_element_type=F32, precision=HIGHEST)
    acs_e = jnp.dot(acs, expand, preferred_element_type=F32, precision=HIGHEST)
    last_e = acs_e[L - 1:L]
    xdt = xs * dt_e
    xdt_b = xdt.astype(BF16)
    xw_b = (xdt * jnp.exp(last_e - acs_e)).astype(BF16)
    dec_in = jnp.exp(acs_e)
    dec_chunk = jnp.exp(last_e)

    row = lax.broadcasted_iota(jnp.int32, (L, L), 0)
    col = lax.broadcasted_iota(jnp.int32, (L, L), 1)
    causal = row >= col
    lo_half = col < SSD_HEAD_DIM
    heads_per_group = SSD_HEADS // SSD_GROUPS
    n_bc = SSD_GROUPS * SSD_STATE

    y_tiles = []
    for g in range(SSD_GROUPS):
        bm = bc[:, g * SSD_STATE:(g + 1) * SSD_STATE]
        cm_b = bc[:, n_bc + g * SSD_STATE:n_bc + (g + 1) * SSD_STATE].astype(BF16)
        bm_t_b = bm.T.astype(BF16)
        cb = jnp.dot(cm_b, bm_t_b, preferred_element_type=F32)
        for j in range(g * heads_per_group // 2, (g + 1) * heads_per_group // 2):
            sl = slice(j * LANES, (j + 1) * LANES)
            xt = xdt_b[:, sl]
            y_pair = None
            for sub in range(2):
                h = 2 * j + sub
                seg = acs[:, h:h + 1] - acs_t[h:h + 1, :]
                m = (cb * jnp.exp(jnp.where(causal, seg, -jnp.inf))).astype(BF16)
                xh = jnp.where(lo_half if sub == 0 else ~lo_half, xt, jnp.zeros_like(xt))
                yd = jnp.dot(m, xh, preferred_element_type=F32)
                y_pair = yd if y_pair is None else y_pair + yd
            h_t = ht_ref[:, sl]
            y_off = jnp.dot(cm_b, h_t.astype(BF16), preferred_element_type=F32) * dec_in[:, sl]
            st = jnp.dot(bm_t_b, xw_b[:, sl], preferred_element_type=F32)
            ht_ref[:, sl] = h_t * dec_chunk[:, sl] + st
            y_tiles.append(y_pair + y_off)
    y = jnp.concatenate(y_tiles, axis=1) + dsk_ref[...] * xs
    y = y * _silu(z_ref[...].astype(F32))
    gw = SSD_INNER // SSD_GROUPS
    outs = []
    for g in range(SSD_GROUPS):
        yg = y[:, g * gw:(g + 1) * gw]
        ms = jnp.mean(yg * yg, -1, keepdims=True)
        outs.append(yg * lax.rsqrt(ms + RMS_EPS) * nw_ref[:, g * gw:(g + 1) * gw])
    o_ref[...] = jnp.concatenate(outs, axis=1).astype(o_ref.dtype)


def _ssd_mixer(proj, small, bsz, seq, conv_w, conv_b, dt_bias, a_log, d_skip, norm_w):
    L = SSD_CHUNK
    lb = min(L * MIXER_CHUNKS, seq)
    nc = seq // lb
    pad_lanes = LANES - SSD_HEADS
    cw = jnp.pad(conv_w.astype(F32), ((0, 8 - SSD_CONV), (0, 0)))
    cwx, cwbc = cw[:, :SSD_INNER], cw[:, SSD_INNER:]
    cb = conv_b.astype(F32)[None, :]
    cbx, cbbc = cb[:, :SSD_INNER], cb[:, SSD_INNER:]
    dtb = jnp.pad(dt_bias.astype(F32), (0, pad_lanes))[None, :]
    nega = jnp.pad(-jnp.exp(a_log.astype(F32)), (0, pad_lanes))[None, :]
    dsk = jnp.repeat(d_skip.astype(F32), SSD_HEAD_DIM)[None, :]
    nw = norm_w.astype(F32)[None, :]
    expand = (np.arange(LANES)[:, None] == (np.arange(SSD_INNER)[None, :] // SSD_HEAD_DIM)).astype(np.float32)
    tril = np.tril(np.ones((L, L), np.float32))
    rows = lambda b, c: b * nc + c
    full = lambda shape: pl.BlockSpec(shape, lambda b, c: (0,) * len(shape))
    return pl.pallas_call(
        _chunked_kernel(_ssd_chunk, 4, 10, L),
        grid=(bsz, nc),
        in_specs=[
            pl.BlockSpec((lb, 1024), lambda b, c: (rows(b, c), _CB_Z)),
            pl.BlockSpec((lb, 1024), lambda b, c: (rows(b, c), _CB_XS)),
            pl.BlockSpec((lb, 512), lambda b, c: (rows(b, c), _CB_BC)),
            pl.BlockSpec((lb, LANES), lambda b, c: (rows(b, c), 0)),
            full((8, SSD_INNER)), full((8, 512)), full((1, SSD_INNER)), full((1, 512)),
            full((1, LANES)), full((1, LANES)), full((1, SSD_INNER)), full((1, SSD_INNER)),
            full((LANES, SSD_INNER)), full((L, L)),
        ],
        out_specs=pl.BlockSpec((lb, SSD_INNER), lambda b, c: (rows(b, c), 0)),
        out_shape=jax.ShapeDtypeStruct((bsz * seq, SSD_INNER), BF16),
        scratch_shapes=[pltpu.VMEM((8, SSD_INNER), F32), pltpu.VMEM((8, 512), F32),
                        pltpu.VMEM((SSD_STATE, SSD_INNER), F32)],
        compiler_params=_cparams(("parallel", "arbitrary")),
        name="ssd_mixer",
    )(proj, proj, proj, small, cwx, cwbc, cbx, cbbc, dtb, nega, dsk, nw, jnp.asarray(expand), jnp.asarray(tril))


def _s5_scan_kernel(u_ref, bd_ref, w_ref, v_ref, pw_ref, y_ref, toep_ref, sc_ref, hin_ref, hc_ref, *, rows):
    R = rows

    @pl.when((pl.program_id(1) == 0) & (pl.program_id(2) == 0))
    def _():
        toep_ref[...] = jnp.zeros_like(toep_ref)
        for s in range(S5_CHUNK):
            for l in range(s, S5_CHUNK):
                toep_ref[s * LANES:(s + 1) * LANES, l * LANES:(l + 1) * LANES] = bd_ref[0, l - s]

    @pl.when(pl.program_id(2) == 0)
    def _():
        hc_ref[...] = jnp.zeros_like(hc_ref)

    parts = [u_ref[pl.ds(s, R, stride=S5_CHUNK), :].astype(BF16) for s in range(S5_CHUNK)]
    uf = jnp.concatenate(parts, axis=1)
    y_intra = jnp.dot(uf, toep_ref[...], preferred_element_type=F32)
    sc_ref[...] = jnp.dot(uf, w_ref[0], preferred_element_type=F32)
    pw = pw_ref[0]
    half = S5_TILE_GROUPS * S5_STATE
    grp = S5_SCAN_ROWS
    row = lax.broadcasted_iota(jnp.int32, (grp, pw.shape[1]), 0)

    def times(m, z):
        return pw[m:m + 1] * z + pw[grp + 1 + m:grp + 2 + m] * pltpu.roll(z, half, 1)

    def step(g, h):
        rows8 = pl.ds(pl.multiple_of(g * grp, grp), grp)
        x = sc_ref[rows8, :]
        for sh in (1, 2, 4):
            x = x + jnp.where(row >= sh, times(sh, pltpu.roll(x, sh, 0)), 0.0)
        hb = jnp.broadcast_to(h, x.shape)
        a_j_h = pw[0:grp] * hb + pw[grp + 1:2 * grp + 1] * pltpu.roll(hb, half, 1)
        hin_ref[rows8, :] = a_j_h + jnp.where(row >= 1, pltpu.roll(x, 1, 0), 0.0)
        return times(grp, h) + x[grp - 1:grp]

    hc_ref[...] = lax.fori_loop(0, R // grp, step, hc_ref[...])
    y = y_intra + jnp.dot(hin_ref[...].astype(BF16), v_ref[0], preferred_element_type=F32)
    for l in range(S5_CHUNK):
        y_ref[pl.ds(l, R, stride=S5_CHUNK), :] = y[:, l * LANES:(l + 1) * LANES]


def _s5_tables(lam_re, lam_im, log_dt, b_re, b_im, c_re, c_im):
    f32 = F32
    lc = S5_CHUNK
    tg = S5_TILE_GROUPS
    nt = S5_GROUPS // tg
    lr = jnp.minimum(lam_re.astype(f32), -1e-4)[None, :]
    li = lam_im.astype(f32)[None, :]
    dt = jnp.exp(log_dt.astype(f32))[:, None]
    mag = jnp.exp(lr * dt)
    ab_re, ab_im = mag * jnp.cos(li * dt), mag * jnp.sin(li * dt)
    den = lr * lr + li * li
    nr, ni = ab_re - 1.0, ab_im
    f_re, f_im = (nr * lr + ni * li) / den, (ni * lr - nr * li) / den
    b_re, b_im = b_re.astype(f32), b_im.astype(f32)
    bb_re = f_re[..., None] * b_re - f_im[..., None] * b_im
    bb_im = f_re[..., None] * b_im + f_im[..., None] * b_re
    tau = jnp.arange(lc + 1, dtype=f32)[:, None, None]
    pw_mag = jnp.exp(tau * (lr * dt)[None])
    pw_re = pw_mag * jnp.cos(tau * (li * dt)[None])
    pw_im = pw_mag * jnp.sin(tau * (li * dt)[None])
    c_re, c_im = c_re.astype(f32), c_im.astype(f32)
    cl_re = c_re[None] * pw_re[:, :, None, :] - c_im[None] * pw_im[:, :, None, :]
    cl_im = c_re[None] * pw_im[:, :, None, :] + c_im[None] * pw_re[:, :, None, :]
    hp = lax.Precision.HIGHEST
    kern = (jnp.einsum('tgop,gpi->tgoi', cl_re[:lc], bb_re, precision=hp)
            - jnp.einsum('tgop,gpi->tgoi', cl_im[:lc], bb_im, precision=hp))
    eye = jnp.eye(tg, dtype=f32)
    toep = jnp.einsum('tTgoi,gh->Ttgiho', kern.reshape(lc, nt, tg, S5_GROUP, S5_GROUP), eye)
    toep = toep.reshape(nt, lc, LANES, LANES)
    rev = pw_re[lc - 1 - np.arange(lc)], pw_im[lc - 1 - np.arange(lc)]
    wr = rev[0][..., None] * bb_re[None] - rev[1][..., None] * bb_im[None]
    wi = rev[0][..., None] * bb_im[None] + rev[1][..., None] * bb_re[None]
    wst = jnp.stack([wr, wi], 0).reshape(2, lc, nt, tg, S5_STATE, S5_GROUP)
    w_in = jnp.einsum('rsTgpi,gh->Tsgirhp', wst, eye).reshape(nt, lc * LANES, 2 * tg * S5_STATE)
    vst = jnp.stack([cl_re[1:], -cl_im[1:]], 0).reshape(2, lc, nt, tg, S5_GROUP, S5_STATE)
    v_out = jnp.einsum('rlTgop,gh->Trgplho', vst, eye).reshape(nt, 2 * tg * S5_STATE, lc * LANES)
    m = (lc * jnp.arange(S5_SCAN_ROWS + 1, dtype=f32))[:, None, None]
    am_mag = jnp.exp(m * (lr * dt)[None])
    am_re = (am_mag * jnp.cos(m * (li * dt)[None])).reshape(S5_SCAN_ROWS + 1, nt, tg * S5_STATE).transpose(1, 0, 2)
    am_im = (am_mag * jnp.sin(m * (li * dt)[None])).reshape(S5_SCAN_ROWS + 1, nt, tg * S5_STATE).transpose(1, 0, 2)
    pw = jnp.concatenate([jnp.concatenate([am_re, am_re], -1), jnp.concatenate([-am_im, am_im], -1)], axis=1)
    pw = jnp.pad(pw, ((0, 0), (0, 24 - pw.shape[1]), (0, 0)))
    return toep.astype(BF16), w_in.astype(BF16), v_out.astype(BF16), pw


def _s5_scan(u, bsz, seq, tables, rows):
    toep, w_in, v_out, pw = tables
    nt = S5_GROUPS // S5_TILE_GROUPS
    lc = S5_CHUNK
    rows = min(rows, seq // lc)
    nblk = seq // (lc * rows)
    tok = rows * lc
    ns = 2 * S5_TILE_GROUPS * S5_STATE
    return pl.pallas_call(
        functools.partial(_s5_scan_kernel, rows=rows),
        grid=(nt, bsz, nblk),
        in_specs=[
            pl.BlockSpec((tok, LANES), lambda t, b, r: (b * nblk + r, t)),
            pl.BlockSpec((1, lc, LANES, LANES), lambda t, b, r: (t, 0, 0, 0)),
            pl.BlockSpec((1, lc * LANES, ns), lambda t, b, r: (t, 0, 0)),
            pl.BlockSpec((1, ns, lc * LANES), lambda t, b, r: (t, 0, 0)),
            pl.BlockSpec((1, 24, ns), lambda t, b, r: (t, 0, 0)),
        ],
        out_specs=pl.BlockSpec((tok, LANES), lambda t, b, r: (b * nblk + r, t)),
        out_shape=jax.ShapeDtypeStruct((bsz * seq, S5_WIDTH), F32),
        scratch_shapes=[pltpu.VMEM((lc * LANES, lc * LANES), BF16), pltpu.VMEM((rows, ns), F32),
                        pltpu.VMEM((rows, ns), F32), pltpu.VMEM((1, ns), F32)],
        compiler_params=_cparams(("arbitrary", "arbitrary", "arbitrary")),
        name="s5_scan",
    )(u, toep, w_in, v_out, pw)


def _s5_post_kernel(y_ref, u_ref, d_ref, w_ref, b_ref, o_ref):
    g = jax.nn.gelu(y_ref[...] + d_ref[...] * u_ref[...], approximate=True)
    gate = jnp.dot(g.astype(BF16), w_ref[...], preferred_element_type=F32) + b_ref[...]
    o_ref[...] = (g * jax.nn.sigmoid(gate)).astype(o_ref.dtype)


def _s5_post(y, u, d_skip, w_glu, b_glu, tm=512):
    t = y.shape[0]
    tm = min(tm, t)
    row = pl.BlockSpec((tm, S5_WIDTH), lambda i: (i, 0))
    vec = pl.BlockSpec((1, S5_WIDTH), lambda i: (0, 0))
    return pl.pallas_call(
        _s5_post_kernel,
        grid=(t // tm,),
        in_specs=[row, row, vec, pl.BlockSpec((S5_WIDTH, S5_WIDTH), lambda i: (0, 0)), vec],
        out_specs=row,
        out_shape=jax.ShapeDtypeStruct((t, S5_WIDTH), BF16),
        compiler_params=_cparams(("parallel",)),
        name="s5_post",
    )(y, u, d_skip.astype(F32)[None, :], w_glu.astype(BF16), b_glu.astype(F32)[None, :])


def _ret_chunk(q_ref, k_ref, v_ref, g_ref, cos_ref, sin_ref, intra_ref, dend_ref, osc_ref, nw_ref,
               o_ref, r_ref, *, chunk_decay):
    L = RET_CHUNK

    cos = cos_ref[...]
    sin = sin_ref[...]
    lane = lax.broadcasted_iota(jnp.int32, (L, LANES), 1)
    rowi = lax.broadcasted_iota(jnp.int32, (LANES, L), 0)
    half = RET_QK_DIM // 2
    outs = []
    for t in range(RET_HEADS // 2):
        sl = slice(t * LANES, (t + 1) * LANES)
        qt = q_ref[:, sl].astype(F32)
        kt = k_ref[:, sl].astype(F32)
        qr = qt * cos + pltpu.roll(qt, 64, 1) * sin
        kr = (kt * cos + pltpu.roll(kt, 64, 1) * sin) * (RET_QK_DIM ** -0.5)
        kr_b = kr.astype(BF16)
        kr_t = kr.T
        for sub in range(2):
            h = 2 * t + sub
            vs = slice(h * RET_V_DIM, (h + 1) * RET_V_DIM)
            qm = jnp.where(((lane // half) % 2) == sub, qr, 0.0).astype(BF16)
            s = lax.dot_general(qm, kr_b, (((1,), (1,)), ((), ())), preferred_element_type=F32)
            p = (s * intra_ref[h]).astype(BF16)
            vh = v_ref[:, vs]
            state = r_ref[h]
            y_in = jnp.dot(p, vh, preferred_element_type=F32)
            y_off = jnp.dot(qm, state.astype(BF16), preferred_element_type=F32) * osc_ref[:, vs]
            ktm = jnp.where(((rowi // half) % 2) == sub, kr_t * dend_ref[h:h + 1, :], 0.0).astype(BF16)
            r_ref[h] = state * chunk_decay[h] + jnp.dot(ktm, vh, preferred_element_type=F32)
            y = y_in + y_off
            mu = jnp.mean(y, -1, keepdims=True)
            yc = y - mu
            var = jnp.mean(yc * yc, -1, keepdims=True)
            yn = yc * lax.rsqrt(var + LN_EPS) * nw_ref[:, vs]
            outs.append(_silu(g_ref[:, vs].astype(F32)) * yn)
    o_ref[...] = jnp.concatenate(outs, axis=1).astype(o_ref.dtype)


def _ret_mixer(proj, bsz, seq, positions, norm_w):
    L = RET_CHUNK
    lb = min(L * MIXER_CHUNKS, seq)
    nc = seq // lb
    half = RET_QK_DIM // 2
    inv_freq = 1.0 / (ROPE_BASE ** (jnp.arange(half, dtype=F32) / half))
    ang = positions.astype(F32).reshape(bsz * seq, 1) * inv_freq[None, :]
    cos32, sin32 = jnp.cos(ang), jnp.sin(ang)
    cos_t = jnp.concatenate([cos32] * 4, axis=1)
    sin_t = jnp.concatenate([-sin32, -sin32, sin32, sin32], axis=1)
    log_gamma = np.log1p(-np.exp2(-5.0 - np.arange(RET_HEADS, dtype=np.float64)))
    pos = np.arange(L, dtype=np.float64)
    rel = pos[:, None] - pos[None, :]
    intra = np.where(rel[None] >= 0, np.exp(rel[None] * log_gamma[:, None, None]), 0.0).astype(np.float32)
    dend = np.exp((L - 1 - pos)[None, :] * log_gamma[:, None]).astype(np.float32)
    osc = np.repeat(np.exp((pos + 1)[:, None] * log_gamma[None, :]), RET_V_DIM, axis=1).astype(np.float32)
    chunk_decay = tuple(float(v) for v in np.exp(L * log_gamma))
    rows = lambda b, c: b * nc + c
    full = lambda shape: pl.BlockSpec(shape, lambda b, c: (0,) * len(shape))
    return pl.pallas_call(
        _chunked_kernel(functools.partial(_ret_chunk, chunk_decay=chunk_decay), 6, 4, L),
        grid=(bsz, nc),
        in_specs=[
            pl.BlockSpec((lb, 512), lambda b, c: (rows(b, c), _CB_RQ)),
            pl.BlockSpec((lb, 512), lambda b, c: (rows(b, c), _CB_RK)),
            pl.BlockSpec((lb, 1024), lambda b, c: (rows(b, c), _CB_RV)),
            pl.BlockSpec((lb, 1024), lambda b, c: (rows(b, c), _CB_RG)),
            pl.BlockSpec((lb, LANES), lambda b, c: (rows(b, c), 0)),
            pl.BlockSpec((lb, LANES), lambda b, c: (rows(b, c), 0)),
            full((RET_HEADS, L, L)), full((RET_HEADS, L)), full((L, RET_WIDTH)), full((1, RET_WIDTH)),
        ],
        out_specs=pl.BlockSpec((lb, RET_WIDTH), lambda b, c: (rows(b, c), 0)),
        out_shape=jax.ShapeDtypeStruct((bsz * seq, RET_WIDTH), BF16),
        scratch_shapes=[pltpu.VMEM((RET_HEADS, LANES, RET_V_DIM), F32)],
        compiler_params=_cparams(("parallel", "arbitrary")),
        name="retention_mixer",
    )(proj, proj, proj, proj, cos_t, sin_t, jnp.asarray(intra), jnp.asarray(dend), jnp.asarray(osc),
      norm_w.astype(F32)[None, :])


def _gla_chunk(q_ref, k_ref, v_ref, r_ref, sm_ref, wa_ref, ba_ref, tblk_ref, nw_ref, o_ref, s_ref):
    LB = GLA_BLOCK
    LC = GLA_CHUNK

    logit = jnp.dot(sm_ref[...], wa_ref[...], preferred_element_type=F32, precision=HIGHEST) + ba_ref[...]
    la = (jnp.minimum(logit, 0.0) - jnp.log1p(jnp.exp(-jnp.abs(logit)))) * (1.0 / GLA_TAU)
    b = jnp.dot(tblk_ref[...], la, preferred_element_type=F32, precision=HIGHEST)
    rowf = lax.broadcasted_iota(jnp.int32, (LB, GLA_QK), 0)
    b_end = jnp.where(rowf < LC, b[LC - 1:LC], b[LB - 1:LB])
    q = q_ref[...].astype(F32) * (GLA_QK_DIM ** -0.5)
    k = k_ref[...].astype(F32)
    q_dec = q * jnp.exp(b)
    k_inv = (k * jnp.exp(-b)).astype(BF16)
    k_dec = k * jnp.exp(b_end - b)
    row = lax.broadcasted_iota(jnp.int32, (LB, LB), 0)
    col = lax.broadcasted_iota(jnp.int32, (LB, LB), 1)
    blk_causal = (row >= col) & ((row // LC) == (col // LC))
    first_rows = row < LC
    first_vrows = lax.broadcasted_iota(jnp.int32, (LB, GLA_V_DIM), 0) < LC
    outs = []
    for h in range(GLA_HEADS):
        sl = slice(h * GLA_QK_DIM, (h + 1) * GLA_QK_DIM)
        vs = slice(h * GLA_V_DIM, (h + 1) * GLA_V_DIM)
        qd = q_dec[:, sl]
        qd_b = qd.astype(BF16)
        att = lax.dot_general(qd_b, k_inv[:, sl], (((1,), (1,)), ((), ())), preferred_element_type=F32)
        att = jnp.where(blk_causal, att, 0.0).astype(BF16)
        vh = v_ref[:, vs]
        y = jnp.dot(att, vh, preferred_element_type=F32)
        s0 = s_ref[h]
        kd_t = k_dec[:, sl].T.astype(BF16)
        kv0 = jnp.dot(kd_t, jnp.where(first_vrows, vh, jnp.zeros_like(vh)), preferred_element_type=F32)
        kv1 = jnp.dot(kd_t, jnp.where(first_vrows, jnp.zeros_like(vh), vh), preferred_element_type=F32)
        b_t = b[:, sl].T
        s1 = s0 * jnp.exp(b_t[:, LC - 1:LC]) + kv0
        s_ref[h] = s1 * jnp.exp(b_t[:, LB - 1:LB]) + kv1
        y = y + jnp.dot(jnp.where(first_rows, qd, 0.0).astype(BF16), s0.astype(BF16), preferred_element_type=F32)
        y = y + jnp.dot(jnp.where(first_rows, 0.0, qd).astype(BF16), s1.astype(BF16), preferred_element_type=F32)
        ms = jnp.mean(y * y, -1, keepdims=True)
        yn = y * lax.rsqrt(ms + RMS_EPS) * nw_ref[:, vs]
        outs.append(yn * _silu(r_ref[:, vs].astype(F32)))
    o_ref[...] = jnp.concatenate(outs, axis=1).astype(o_ref.dtype)


def _gla_mixer(proj, small, bsz, seq, w_alpha, b_alpha, norm_w):
    LB = GLA_BLOCK
    lb = min(LB * MIXER_CHUNKS, seq)
    nb = seq // lb
    wa = jnp.zeros((LANES, GLA_QK), F32).at[SSD_HEADS:SSD_HEADS + GLA_GATE_RANK].set(w_alpha.astype(F32))
    idx = np.arange(LB)
    tblk = ((idx[:, None] >= idx[None, :]) & ((idx[:, None] // GLA_CHUNK) == (idx[None, :] // GLA_CHUNK)))
    rows = lambda b, c: b * nb + c
    full = lambda shape: pl.BlockSpec(shape, lambda b, c: (0,) * len(shape))
    return pl.pallas_call(
        _chunked_kernel(_gla_chunk, 5, 4, LB),
        grid=(bsz, nb),
        in_specs=[
            pl.BlockSpec((lb, 512), lambda b, c: (rows(b, c), _CB_GQ)),
            pl.BlockSpec((lb, 512), lambda b, c: (rows(b, c), _CB_GK)),
            pl.BlockSpec((lb, 1024), lambda b, c: (rows(b, c), _CB_GV)),
            pl.BlockSpec((lb, 1024), lambda b, c: (rows(b, c), _CB_GR)),
            pl.BlockSpec((lb, LANES), lambda b, c: (rows(b, c), 0)),
            full((LANES, GLA_QK)), full((1, GLA_QK)), full((LB, LB)), full((1, GLA_WIDTH)),
        ],
        out_specs=pl.BlockSpec((lb, GLA_WIDTH), lambda b, c: (rows(b, c), 0)),
        out_shape=jax.ShapeDtypeStruct((bsz * seq, GLA_WIDTH), BF16),
        scratch_shapes=[pltpu.VMEM((GLA_HEADS, GLA_QK_DIM, GLA_V_DIM), F32)],
        compiler_params=_cparams(("parallel", "arbitrary")),
        name="gla_mixer",
    )(proj, proj, proj, proj, small, wa, b_alpha.astype(F32)[None, :], jnp.asarray(tblk.astype(np.float32)),
      norm_w.astype(F32)[None, :])


def _merge_kernel(ya_ref, yb_ref, yc_ref, yd_ref, g0_ref, g1_ref, g2_ref, g3_ref, wb_ref, o_ref):
    acc = None
    for n, (y_ref, g_ref) in enumerate(((ya_ref, g0_ref), (yb_ref, g1_ref), (yc_ref, g2_ref), (yd_ref, g3_ref))):
        br = jnp.dot(y_ref[...], wb_ref[n], preferred_element_type=F32)
        term = jax.nn.sigmoid(g_ref[...].astype(F32)) * br
        acc = term if acc is None else acc + term
    o_ref[...] = acc.astype(o_ref.dtype)


def _merge(ys, proj, w_branch, tm=1024, tn=512):
    t = ys[0].shape[0]
    tm = min(tm, t)
    nj = D_MODEL // tn
    yspec = pl.BlockSpec((tm, BRANCH_WIDTH), lambda i, j: (i, 0))
    gspec = lambda n: pl.BlockSpec((tm, tn), lambda i, j: (i, n * nj + j))
    return pl.pallas_call(
        _merge_kernel,
        grid=(t // tm, nj),
        in_specs=[yspec] * 4 + [gspec(n) for n in range(N_BRANCH)]
                 + [pl.BlockSpec((N_BRANCH, BRANCH_WIDTH, tn), lambda i, j: (0, 0, j))],
        out_specs=pl.BlockSpec((tm, tn), lambda i, j: (i, j)),
        out_shape=jax.ShapeDtypeStruct((t, D_MODEL), BF16),
        compiler_params=_cparams(("parallel", "parallel")),
        name="branch_merge",
    )(*ys, proj, proj, proj, proj, w_branch.astype(BF16))


def _row_halves(tm):
    half = tm // 2 if tm % 32 == 0 else tm
    return [(r0, slice(r0, r0 + half)) for r0 in range(0, tm, half)]


def _proj_ln_kernel(m_ref, w_ref, h_ref, lw_ref, lb_ref, o_ref, op_ref):
    for r0, rs in _row_halves(m_ref.shape[0]):
        mix = jnp.dot(m_ref[rs, :], w_ref[...], preferred_element_type=F32)
        out = _layer_norm(DEEPNORM_ALPHA * h_ref[rs, :] + mix, lw_ref[...], lb_ref[...])
        o_ref[rs, :] = out
        _store_token_tiles(op_ref, r0, _pack_rows(out.astype(BF16)))


def _proj_ln(merged, w_out, h, ln_w, ln_b, tm=512):
    t = h.shape[0]
    tm = min(tm, t)
    row = pl.BlockSpec((tm, D_MODEL), lambda i: (i, 0))
    vec = pl.BlockSpec((1, D_MODEL), lambda i: (0, 0))
    return pl.pallas_call(
        _proj_ln_kernel,
        grid=(t // tm,),
        in_specs=[row, pl.BlockSpec((D_MODEL, D_MODEL), lambda i: (0, 0)), row, vec, vec],
        out_specs=[row, pl.BlockSpec((tm * TILE_ROWS, LANES), lambda i: (i, 0))],
        out_shape=[jax.ShapeDtypeStruct((t, D_MODEL), F32), jax.ShapeDtypeStruct((t * TILE_ROWS, LANES), jnp.uint32)],
        compiler_params=_cparams(("parallel",)),
        name="out_proj_ln",
    )(merged, w_out.astype(BF16), h, ln_w.astype(F32)[None, :], ln_b.astype(F32)[None, :])


def _router_kernel(h_ref, w_ref, b_ref, idx_ref, gate_ref, cnt_ref, run_ref):
    @pl.when(pl.program_id(0) == 0)
    def _():
        run_ref[...] = jnp.zeros_like(run_ref)

    logits = _dot3(w_ref[...], h_ref[...], ((1,), (1,))) + b_ref[...]
    eid = lax.broadcasted_iota(jnp.int32, logits.shape, 0)
    vals = logits
    run = run_ref[...]
    top_v, top_i = [], []
    for _ in range(TOP_K):
        m = jnp.max(vals, axis=0, keepdims=True)
        sel = jnp.min(jnp.where(vals == m, eid, N_EXPERTS), axis=0, keepdims=True)
        hit = eid == sel
        top_v.append(m)
        top_i.append(sel)
        vals = jnp.where(hit, -jnp.inf, vals)
        run = run + jnp.sum(jnp.where(hit, 1.0, 0.0), axis=1, keepdims=True)
    run_ref[...] = run
    ex = [jnp.exp(v - top_v[0]) for v in top_v]
    den = ex[0] + ex[1] + ex[2] + ex[3]
    zi = jnp.zeros_like(top_i[0])
    zf = jnp.zeros_like(den)
    idx_ref[...] = jnp.concatenate(top_i + [zi] * (8 - TOP_K), axis=0)
    gate_ref[...] = jnp.concatenate([e / den for e in ex] + [zf] * (8 - TOP_K), axis=0)
    cnt_ref[...] = jnp.broadcast_to(run, cnt_ref.shape).astype(jnp.int32)


def _router(h, router_w, router_b, tm=512):
    t = h.shape[0]
    tm = min(tm, t)
    tok = pl.BlockSpec((8, tm), lambda i: (0, i))
    return pl.pallas_call(
        _router_kernel,
        grid=(t // tm,),
        in_specs=[pl.BlockSpec((tm, D_MODEL), lambda i: (i, 0)),
                  pl.BlockSpec((N_EXPERTS, D_MODEL), lambda i: (0, 0)),
                  pl.BlockSpec((N_EXPERTS, 1), lambda i: (0, 0))],
        out_specs=[tok, tok, pl.BlockSpec((N_EXPERTS, LANES), lambda i: (0, 0))],
        out_shape=[jax.ShapeDtypeStruct((8, t), jnp.int32), jax.ShapeDtypeStruct((8, t), F32),
                   jax.ShapeDtypeStruct((N_EXPERTS, LANES), jnp.int32)],
        scratch_shapes=[pltpu.VMEM((N_EXPERTS, 1), F32)],
        compiler_params=_cparams(("arbitrary",)),
        name="router_topk",
    )(h, router_w.astype(F32).T, router_b.astype(F32)[:, None])


def _pack_rows(x):
    n = x.shape[1] // 2
    lo = pltpu.bitcast(x[:, :n].astype(F32), jnp.uint32) >> 16
    hi = pltpu.bitcast(x[:, n:].astype(F32), jnp.uint32)
    return lo | hi


def _unpack_rows(w):
    lo = pltpu.bitcast(w << 16, F32)
    hi = pltpu.bitcast(w & jnp.uint32(0xFFFF0000), F32)
    return lo, hi


TILE_ROWS = (D_MODEL // 2) // LANES


def _store_token_tiles(ref, first, words):
    m = words.shape[0]
    for s in range(TILE_ROWS):
        ref[pl.ds(first * TILE_ROWS + s, m, stride=TILE_ROWS), :] = words[:, s * LANES:(s + 1) * LANES]


def _load_token_tiles(ref, first, m):
    return jnp.concatenate([ref[pl.ds(first * TILE_ROWS + s, m, stride=TILE_ROWS), :] for s in range(TILE_ROWS)],
                           axis=1)


def _expert_kernel(be_ref, nb_ref, tab_ref, hp_ref, wgu_ref, bgu_ref, wd_ref, bd_ref, y_ref,
                   idx_ref, xbuf_ref, ybuf_ref, isem, gsem, ssem):
    i = pl.program_id(0)
    n_used = nb_ref[0]
    blk = MOE_BLOCK
    tr = TILE_ROWS

    def idx_copy(b):
        return pltpu.make_async_copy(tab_ref.at[b], idx_ref.at[b % 4], isem.at[b % 4])

    def tile(r):
        return pl.ds(r * tr if isinstance(r, int) else pl.multiple_of(r * tr, tr), tr)

    def queue(r):
        return r % 2 if isinstance(r, int) else 0

    def gather_start(b, r):
        src = pl.multiple_of(idx_ref[b % 4, r], tr)
        pltpu.make_async_copy(hp_ref.at[pl.ds(src, tr)], xbuf_ref.at[b % 2, tile(r)],
                              gsem.at[b % 2]).start(priority=queue(r))

    def scatter_start(b, r):
        dst = pl.multiple_of(idx_ref[b % 4, blk + r], tr)
        pltpu.make_async_copy(ybuf_ref.at[b % 2, tile(r)], y_ref.at[pl.ds(dst, tr)],
                              ssem.at[b % 2]).start(priority=queue(r))

    def gather_wait(b):
        pltpu.make_async_copy(hp_ref.at[pl.ds(0, blk * tr)], xbuf_ref.at[b % 2], gsem.at[b % 2]).wait()

    def scatter_wait(b):
        pltpu.make_async_copy(ybuf_ref.at[b % 2], y_ref.at[pl.ds(0, blk * tr)], ssem.at[b % 2]).wait()

    def ffn(b):
        lo, hi = _unpack_rows(_load_token_tiles(xbuf_ref.at[b % 2], 0, blk))
        xb = jnp.concatenate([lo.astype(BF16), hi.astype(BF16)], axis=1)
        hgu = jnp.dot(xb, wgu_ref[0], preferred_element_type=F32) + bgu_ref[0]
        g = jnp.minimum(hgu[:, :D_EXPERT], SWIGLU_LIMIT)
        u = jnp.clip(hgu[:, D_EXPERT:], -SWIGLU_LIMIT, SWIGLU_LIMIT)
        act = (u + 1.0) * (g * jax.nn.sigmoid(SWIGLU_ALPHA * g))
        out = jnp.dot(act.astype(BF16), wd_ref[0], preferred_element_type=F32) + bd_ref[0]
        packed = _pack_rows(out.astype(BF16))

        @pl.when(i >= 2)
        def _():
            scatter_wait(i - 2)

        _store_token_tiles(ybuf_ref.at[b % 2], 0, packed)

    @pl.when(i == 0)
    def _():
        idx_copy(0).start()
        ybuf_ref[1] = jnp.zeros(ybuf_ref.shape[1:], ybuf_ref.dtype)
        spare = pltpu.make_async_copy(ybuf_ref.at[1], y_ref.at[pl.ds(y_ref.shape[0] - blk * tr, blk * tr)],
                                      ssem.at[1])
        spare.start()
        idx_copy(0).wait()
        lax.fori_loop(0, blk, lambda r, c: (gather_start(0, r), c)[1], 0, unroll=8)
        idx_copy(1).start()
        spare.wait()

    @pl.when(i <= n_used)
    def _():
        idx_copy(i + 1).wait()
        gather_wait(i)

    @pl.when(i < n_used)
    def _():
        idx_copy(i + 2).start()

    @pl.when(i == 0)
    def _():
        for r in range(blk):
            gather_start(1, r)
        ffn(0)

    @pl.when((i > 0) & (i < n_used))
    def _():
        for r in range(blk):
            gather_start(i + 1, r)
            scatter_start(i - 1, r)
        ffn(i)

    @pl.when(i == n_used)
    def _():
        @pl.when(i >= 2)
        def _():
            scatter_wait(i - 2)

        lax.fori_loop(0, blk, lambda r, c: (scatter_start(i - 1, r), c)[1], 0, unroll=8)
        scatter_wait(i - 1)


def _expert_params(w_gate_up, b_gate_up, w_down, b_down):
    n = w_gate_up.shape[0] * N_EXPERTS
    return (w_gate_up.astype(BF16).reshape(n, D_MODEL, 2 * D_EXPERT), b_gate_up.astype(F32).reshape(n, 1, 2 * D_EXPERT),
            w_down.astype(BF16).reshape(n, D_EXPERT, D_MODEL), b_down.astype(F32).reshape(n, 1, D_MODEL))


def _experts(h_packed, table, block_expert, n_used, expert_params):
    w_gate_up, b_gate_up, w_down, b_down = expert_params
    blk = MOE_BLOCK
    n_blocks = block_expert.shape[0]
    n_out = TOP_K * (h_packed.shape[0] // TILE_ROWS) + blk
    grid_spec = pltpu.PrefetchScalarGridSpec(
        num_scalar_prefetch=2,
        grid=(n_blocks,),
        in_specs=[
            pl.BlockSpec(memory_space=pl.ANY),
            pl.BlockSpec(memory_space=pl.ANY),
            pl.BlockSpec((1, D_MODEL, 2 * D_EXPERT), lambda i, be, nb: (be[i], 0, 0)),
            pl.BlockSpec((1, 1, 2 * D_EXPERT), lambda i, be, nb: (be[i], 0, 0)),
            pl.BlockSpec((1, D_EXPERT, D_MODEL), lambda i, be, nb: (be[i], 0, 0)),
            pl.BlockSpec((1, 1, D_MODEL), lambda i, be, nb: (be[i], 0, 0)),
        ],
        out_specs=pl.BlockSpec(memory_space=pl.ANY),
        scratch_shapes=[pltpu.SMEM((4, 2 * blk), jnp.int32), pltpu.VMEM((2, blk * TILE_ROWS, LANES), jnp.uint32),
                        pltpu.VMEM((2, blk * TILE_ROWS, LANES), jnp.uint32), pltpu.SemaphoreType.DMA((4,)),
                        pltpu.SemaphoreType.DMA((2,)), pltpu.SemaphoreType.DMA((2,))],
    )
    return pl.pallas_call(
        _expert_kernel,
        grid_spec=grid_spec,
        out_shape=jax.ShapeDtypeStruct((n_out * TILE_ROWS, LANES), jnp.uint32),
        compiler_params=_cparams(("arbitrary",)),
        name="expert_ffn",
    )(block_expert, n_used, table, h_packed, w_gate_up, b_gate_up, w_down, b_down)


def _combine_kernel(y0_ref, y1_ref, y2_ref, y3_ref, gate_ref, h_ref, lw_ref, lb_ref, o_ref, ob_ref, *, tm):
    rows = min(tm, 16)
    for r0 in range(0, tm, rows):
        rs = slice(r0, r0 + rows)
        gates = gate_ref[rs, :]
        ffn = None
        for k, y_ref in enumerate((y0_ref, y1_ref, y2_ref, y3_ref)):
            lo, hi = _unpack_rows(_load_token_tiles(y_ref, r0, rows))
            term = gates[:, k:k + 1] * jnp.concatenate([lo, hi], axis=1)
            ffn = term if ffn is None else ffn + term
        out = _layer_norm(DEEPNORM_ALPHA * h_ref[rs, :] + ffn, lw_ref[...], lb_ref[...])
        o_ref[rs, :] = out
        ob_ref[rs, :] = out.astype(BF16)


def _combine(y, gates, h, ln_w, ln_b, tm=512):
    t = h.shape[0]
    tm = min(tm, t)
    nt = t // tm
    row = pl.BlockSpec((tm, D_MODEL), lambda i: (i, 0))
    vec = pl.BlockSpec((1, D_MODEL), lambda i: (0, 0))
    yspec = lambda k: pl.BlockSpec((tm * TILE_ROWS, LANES), lambda i: (k * nt + i, 0))
    return pl.pallas_call(
        functools.partial(_combine_kernel, tm=tm),
        grid=(nt,),
        in_specs=[yspec(k) for k in range(TOP_K)] + [pl.BlockSpec((tm, 8), lambda i: (i, 0)), row, vec, vec],
        out_specs=[row, row],
        out_shape=[jax.ShapeDtypeStruct((t, D_MODEL), F32), jax.ShapeDtypeStruct((t, D_MODEL), BF16)],
        compiler_params=_cparams(("parallel",)),
        name="moe_combine_ln",
    )(y, y, y, y, gates, h, ln_w.astype(F32)[None, :], ln_b.astype(F32)[None, :])


def _moe(h, h_packed, router_w, router_b, expert_params, layer, ln_w, ln_b):
    t = h.shape[0]
    blk = MOE_BLOCK
    top_idx, top_gate, counts = _router(h, router_w, router_b)
    counts = counts[:, 0]
    n_assign = TOP_K * t
    n_blocks = n_assign // blk + N_EXPERTS
    a_ids = jnp.arange(n_assign, dtype=jnp.int32)
    a_sorted = jnp.sort(top_idx[:TOP_K].reshape(-1) * n_assign + a_ids) % n_assign
    padded = (counts + blk - 1) // blk * blk
    pad_end = jnp.cumsum(padded)
    grp_end = jnp.cumsum(counts)
    blocks = jnp.arange(n_blocks, dtype=jnp.int32)
    block_expert = jnp.sum(((pad_end // blk)[None, :] <= blocks[:, None]).astype(jnp.int32), axis=1)
    block_expert = jnp.minimum(block_expert, N_EXPERTS - 1)
    n_used = (pad_end[-1] // blk).astype(jnp.int32).reshape(1)
    is_e = block_expert[:, None] == jnp.arange(N_EXPERTS, dtype=jnp.int32)[None, :]
    pick = lambda v: jnp.sum(jnp.where(is_e, v[None, :], 0), axis=1)
    offset = blocks * blk - pick(pad_end - padded)
    first = pick(grp_end - counts) + offset
    n_valid = jnp.clip(pick(counts) - offset, 0, blk)
    j = jnp.arange(blk, dtype=jnp.int32)[None, :]
    valid = j < n_valid[:, None]
    a_rows = a_sorted[jnp.clip(first[:, None] + j, 0, n_assign - 1)]
    tok_rows = jnp.where(valid, a_rows % t, 0)
    out_rows = jnp.where(valid, a_rows, n_assign + j)
    table = (jnp.concatenate([tok_rows, out_rows], axis=1) * TILE_ROWS).astype(jnp.int32)
    table = jnp.concatenate([table, jnp.zeros((2, 2 * blk), jnp.int32)], axis=0)
    y = _experts(h_packed, table, block_expert + layer * N_EXPERTS, n_used, expert_params)
    return _combine(y, top_gate.T, h, ln_w, ln_b)


def _ple_kernel(hb_ref, h_ref, p_ref, wg_ref, wp_ref, lw_ref, lb_ref, o_ref, ob_ref):
    for _, rs in _row_halves(hb_ref.shape[0]):
        gate = jax.nn.sigmoid(jnp.dot(hb_ref[rs, :], wg_ref[...], preferred_element_type=F32))
        emb = jnp.dot(p_ref[rs, :].astype(BF16), wp_ref[...], preferred_element_type=F32)
        out = _layer_norm(DEEPNORM_ALPHA * h_ref[rs, :] + gate * emb, lw_ref[...], lb_ref[...])
        o_ref[rs, :] = out
        ob_ref[rs, :] = out.astype(BF16)


def _ple(hb, h, p_i, w_gate, w_proj, ln_w, ln_b, tm=512):
    t = h.shape[0]
    tm = min(tm, t)
    row = pl.BlockSpec((tm, D_MODEL), lambda i: (i, 0))
    vec = pl.BlockSpec((1, D_MODEL), lambda i: (0, 0))
    return pl.pallas_call(
        _ple_kernel,
        grid=(t // tm,),
        in_specs=[row, row, pl.BlockSpec((tm, PLE_DIM), lambda i: (i, 0)),
                  pl.BlockSpec((D_MODEL, D_MODEL), lambda i: (0, 0)),
                  pl.BlockSpec((PLE_DIM, D_MODEL), lambda i: (0, 0)), vec, vec],
        out_specs=[row, row],
        out_shape=[jax.ShapeDtypeStruct((t, D_MODEL), F32), jax.ShapeDtypeStruct((t, D_MODEL), BF16)],
        compiler_params=_cparams(("parallel",)),
        name="ple_ln",
    )(hb, h, p_i, w_gate.astype(BF16), w_proj.astype(BF16), ln_w.astype(F32)[None, :], ln_b.astype(F32)[None, :])


def _token_mixer(h, hb, bsz, seq, positions, w_in, ssd_conv_w, ssd_conv_b, ssd_dt_bias, ssd_a_log, ssd_d,
                 ssd_norm_w, s5_lambda_re, s5_lambda_im, s5_log_dt, s5_b_re, s5_b_im, s5_c_re, s5_c_im, s5_d,
                 s5_w_glu, s5_b_glu, ret_norm_w, gla_w_alpha, gla_b_alpha, gla_norm_w, w_branch, s5_rows):
    w_main = _main_weight(w_in)
    w_u = w_in[:, _OFF_U:_OFF_U + S5_WIDTH].astype(BF16)
    w_small = jnp.concatenate([w_in[:, _OFF_DT:_OFF_DT + SSD_HEADS], w_in[:, _OFF_GC:_OFF_GC + GLA_GATE_RANK],
                               jnp.zeros((D_MODEL, LANES - SSD_HEADS - GLA_GATE_RANK), w_in.dtype)], axis=1)
    proj = _matmul(hb, w_main, BF16, 1024, 1536, name="in_proj_main")
    u = _matmul(hb, w_u, F32, 1024, 1024, name="in_proj_s5")
    small = _matmul_f32(h, w_small.astype(F32), 512, name="in_proj_small")
    y_a = _ssd_mixer(proj, small, bsz, seq, ssd_conv_w, ssd_conv_b, ssd_dt_bias, ssd_a_log, ssd_d, ssd_norm_w)
    tables = _s5_tables(s5_lambda_re, s5_lambda_im, s5_log_dt, s5_b_re, s5_b_im, s5_c_re, s5_c_im)
    y_b = _s5_post(_s5_scan(u, bsz, seq, tables, s5_rows), u, s5_d, s5_w_glu, s5_b_glu)
    y_c = _ret_mixer(proj, bsz, seq, positions, ret_norm_w)
    y_d = _gla_mixer(proj, small, bsz, seq, gla_w_alpha, gla_b_alpha, gla_norm_w)
    return _merge((y_a, y_b, y_c, y_d), proj, w_branch)


def kernel(x, p, positions, w_in, ssd_conv_w, ssd_conv_b, ssd_dt_bias, ssd_a_log, ssd_d, ssd_norm_w, s5_lambda_re, s5_lambda_im, s5_log_dt, s5_b_re, s5_b_im, s5_c_re, s5_c_im, s5_d, s5_w_glu, s5_b_glu, ret_norm_w, gla_w_alpha, gla_b_alpha, gla_norm_w, w_branch, w_out, ln1_w, ln1_b, router_w, router_b, moe_w_gate_up, moe_b_gate_up, moe_w_down, moe_b_down, ln2_w, ln2_b, ple_w_gate, ple_w_proj, ln3_w, ln3_b):
    s5_rows = 256
    bsz, seq, d = x.shape
    t = bsz * seq
    h = x.reshape(t, d).astype(F32)
    hb = h.astype(BF16)
    expert_params = _expert_params(moe_w_gate_up, moe_b_gate_up, moe_w_down, moe_b_down)
    for i in range(DEPTH):
        merged = _token_mixer(h, hb, bsz, seq, positions, w_in[i], ssd_conv_w[i], ssd_conv_b[i], ssd_dt_bias[i],
                              ssd_a_log[i], ssd_d[i], ssd_norm_w[i], s5_lambda_re[i], s5_lambda_im[i],
                              s5_log_dt[i], s5_b_re[i], s5_b_im[i], s5_c_re[i], s5_c_im[i], s5_d[i],
                              s5_w_glu[i], s5_b_glu[i], ret_norm_w[i], gla_w_alpha[i], gla_b_alpha[i],
                              gla_norm_w[i], w_branch[i], s5_rows)
        h, h_packed = _proj_ln(merged, w_out[i], h, ln1_w[i], ln1_b[i])
        h, hb = _moe(h, h_packed, router_w[i], router_b[i], expert_params, i, ln2_w[i], ln2_b[i])
        h, hb = _ple(hb, h, p[i].reshape(t, PLE_DIM), ple_w_gate[i], ple_w_proj[i], ln3_w[i], ln3_b[i])
    return h.reshape(bsz, seq, d).astype(x.dtype)
```
